```python
import jax, jax.numpy as jnp
from jax import lax
import numpy as np

D_MODEL = 1024
BATCH = 4
SEQ = 8192
DEPTH = 2

CHUNK = 64
MIX_WIDTH = D_MODEL
POOL_WIDTH = MIX_WIDTH // 2
POOL_WINDOWS = (2, 4, 8, 16)
POOL_GROUPS = len(POOL_WINDOWS)
POOL_GW = POOL_WIDTH // POOL_GROUPS
ATT_WIDTH = MIX_WIDTH - POOL_WIDTH
N_HEADS = 8
HEAD_DIM = ATT_WIDTH // N_HEADS
LEFT_CHUNKS = 8
LEFT = LEFT_CHUNKS * CHUNK
BAND = LEFT + CHUNK
MAX_REL = 128
N_REL = 2 * MAX_REL + 1
IN_WIDTH = POOL_WIDTH + 3 * ATT_WIDTH
N_GROUPS = 4
EXPERTS_PER_GROUP = 8
N_EXPERTS = N_GROUPS * EXPERTS_PER_GROUP
TOP_K = 2
D_EXPERT = D_MODEL // 2
EXPERT_BLOCK = 128
EPS = 1e-6
NEG_INF = -1e30

kernel_name = "hybrid_pool_chunkattn_hmoe_adaln"


def rmsnorm(x, g):
    xf = x.astype(jnp.float32)
    y = xf * lax.rsqrt(jnp.mean(xf * xf, axis=-1, keepdims=True) + EPS)
    return (y * g.astype(jnp.float32)).astype(x.dtype)


def pool_mixer(u, w, scale):
    B, S, _ = u.shape
    uf = u.astype(jnp.float32)
    cs0 = jnp.concatenate([jnp.zeros((B, 1, POOL_WIDTH), jnp.float32), jnp.cumsum(uf, axis=1)], axis=1)
    outs = []
    for gi, win in enumerate(POOL_WINDOWS):
        sl = slice(gi * POOL_GW, (gi + 1) * POOL_GW)
        csg = cs0[..., sl]
        lag = jnp.concatenate([jnp.zeros((B, win - 1, POOL_GW), jnp.float32), csg[:, :S + 1 - win]], axis=1)
        cnt = jnp.minimum(jnp.arange(S) + 1, win).astype(jnp.float32)[None, :, None]
        outs.append((csg[:, 1:] - lag) / cnt - uf[..., sl])
    p = jnp.stack(outs, axis=2).astype(u.dtype)
    p = jnp.einsum('bsgc,gcd->bsgd', p, w).reshape(B, S, POOL_WIDTH)
    return p * scale


def chunk_attention(q, k, v, bias):
    B, S, H, Dh = q.shape
    nc = S // CHUNK
    kp = jnp.pad(k, ((0, 0), (LEFT, 0), (0, 0), (0, 0)))
    vp = jnp.pad(v, ((0, 0), (LEFT, 0), (0, 0), (0, 0)))
    kpos = jnp.arange(BAND)
    sm_scale = HEAD_DIM ** -0.5

    def one_chunk(i):
        s0 = i * CHUNK
        qc = lax.dynamic_slice_in_dim(q, s0, CHUNK, axis=1)
        kc = lax.dynamic_slice_in_dim(kp, s0, BAND, axis=1)
        vc = lax.dynamic_slice_in_dim(vp, s0, BAND, axis=1)
        s = jnp.einsum('bqhd,bkhd->bhqk', qc, kc).astype(jnp.float32) * sm_scale + bias[None]
        valid = kpos >= LEFT - s0
        s = jnp.where(valid[None, None, None, :], s, NEG_INF)
        p = jax.nn.softmax(s, axis=-1).astype(v.dtype)
        return jnp.einsum('bhqk,bkhd->bqhd', p, vc)

    o = lax.map(one_chunk, jnp.arange(nc))
    return o.transpose(1, 0, 2, 3, 4).reshape(B, S, H * Dh)


def hier_moe(h, rg_w, rg_b, re_w, re_b, w_gate, w_up, w_down):
    B, S, D = h.shape
    N = B * S
    t = h.reshape(N, D)
    g_prob = jax.nn.softmax((t @ rg_w + rg_b).astype(jnp.float32), axis=-1)
    g_p, g_idx = lax.top_k(g_prob, 1)
    e_logits = (t @ re_w + re_b).astype(jnp.float32).reshape(N, N_GROUPS, EXPERTS_PER_GROUP)
    e_in = jnp.take_along_axis(e_logits, g_idx[:, :, None], axis=1)[:, 0]
    e_top, e_loc = lax.top_k(e_in, TOP_K)
    e_w = jax.nn.softmax(e_top, axis=-1) * g_p
    e_id = g_idx * EXPERTS_PER_GROUP + e_loc

    flat_e = e_id.reshape(-1)
    flat_tok = jnp.repeat(jnp.arange(N, dtype=jnp.int32), TOP_K)
    flat_w = e_w.reshape(-1)
    order = jnp.argsort(flat_e, stable=True)
    se = flat_e[order]
    counts = jnp.bincount(flat_e, length=N_EXPERTS)
    padded = ((counts + EXPERT_BLOCK - 1) // EXPERT_BLOCK) * EXPERT_BLOCK
    pad_end = jnp.cumsum(padded)
    pad_start = pad_end - padded
    start = jnp.cumsum(counts) - counts
    dest = pad_start[se] + jnp.arange(N * TOP_K) - start[se]
    P = N * TOP_K + N_EXPERTS * EXPERT_BLOCK
    nblk = P // EXPERT_BLOCK
    buf_tok = jnp.full((P,), N, jnp.int32).at[dest].set(flat_tok[order])
    buf_w = jnp.zeros((P,), jnp.float32).at[dest].set(flat_w[order])
    blk_e = jnp.minimum(jnp.searchsorted(pad_end, jnp.arange(nblk) * EXPERT_BLOCK, side='right'),
                        N_EXPERTS - 1)
    t_pad = jnp.concatenate([t, jnp.zeros((1, D), t.dtype)], axis=0)
    xb = t_pad[buf_tok].reshape(nblk, EXPERT_BLOCK, D)

    def expert_block(args):
        xblk, e = args
        return (jax.nn.silu(xblk @ w_gate[e]) * (xblk @ w_up[e])) @ w_down[e]

    yb = lax.map(expert_block, (xb, blk_e)).reshape(P, D)
    yb = yb * buf_w[:, None].astype(yb.dtype)
    out = jnp.zeros((N + 1, D), yb.dtype).at[buf_tok].add(yb)[:N]
    return out.reshape(B, S, D)


def setup_inputs(seed: int = 0) -> dict:
    key = jax.random.key(seed)
    ks = jax.random.split(key, 24)
    f32 = jnp.float32
    L, D = DEPTH, D_MODEL
    nrm = lambda k, shape, s: jax.random.normal(k, shape, f32) * s
    return {
        "x": nrm(ks[0], (BATCH, SEQ, D), 1.0),
        "c": nrm(ks[1], (BATCH, D), 1.0),
        "ada_w": nrm(ks[2], (L, D, 6 * D), 0.5 * D ** -0.5),
        "ada_b": nrm(ks[3], (L, 6 * D), 0.02),
        "norm1_g": 1.0 + nrm(ks[4], (L, D), 0.05),
        "norm2_g": 1.0 + nrm(ks[5], (L, D), 0.05),
        "w_in": nrm(ks[6], (L, D, IN_WIDTH), D ** -0.5),
        "pool_w": nrm(ks[7], (L, POOL_GROUPS, POOL_GW, POOL_GW), POOL_GW ** -0.5),
        "pool_scale": 1.0 + nrm(ks[8], (L, POOL_WIDTH), 0.1),
        "q_norm_g": 1.0 + nrm(ks[9], (L, HEAD_DIM), 0.05),
        "k_norm_g": 1.0 + nrm(ks[10], (L, HEAD_DIM), 0.05),
        "rel_bias": nrm(ks[11], (N_HEADS, N_REL), 0.1),
        "w_out": nrm(ks[12], (L, MIX_WIDTH, D), MIX_WIDTH ** -0.5),
        "router_group_w": nrm(ks[13], (L, D, N_GROUPS), D ** -0.5),
        "router_group_b": nrm(ks[14], (L, N_GROUPS), 0.01),
        "router_expert_w": nrm(ks[15], (L, D, N_EXPERTS), D ** -0.5),
        "router_expert_b": nrm(ks[16], (L, N_EXPERTS), 0.01),
        "moe_w_gate": nrm(ks[17], (L, N_EXPERTS, D, D_EXPERT), D ** -0.5),
        "moe_w_up": nrm(ks[18], (L, N_EXPERTS, D, D_EXPERT), D ** -0.5),
        "moe_w_down": nrm(ks[19], (L, N_EXPERTS, D_EXPERT, D), D_EXPERT ** -0.5),
    }


def reference(x, c, ada_w, ada_b, norm1_g, norm2_g, w_in, pool_w, pool_scale, q_norm_g, k_norm_g,
              rel_bias, w_out, router_group_w, router_group_b, router_expert_w, router_expert_b,
              moe_w_gate, moe_w_up, moe_w_down):
    B, S, D = x.shape
    rel = LEFT + jnp.arange(CHUNK)[:, None] - jnp.arange(BAND)[None, :]
    rel_idx = jnp.clip(rel, -MAX_REL, MAX_REL) + MAX_REL
    bias = rel_bias.astype(jnp.float32)[:, rel_idx]
    c_act = jax.nn.silu(c)
    for l in range(DEPTH):
        mod = (c_act @ ada_w[l] + ada_b[l])[:, None, :]
        sh1, sc1, g1, sh2, sc2, g2 = jnp.split(mod, 6, axis=-1)

        h = rmsnorm(x, norm1_g[l]) * (1.0 + sc1) + sh1
        z = h @ w_in[l]
        u = z[..., :POOL_WIDTH]
        q, k, v = jnp.split(z[..., POOL_WIDTH:], 3, axis=-1)
        q = rmsnorm(q.reshape(B, S, N_HEADS, HEAD_DIM), q_norm_g[l])
        k = rmsnorm(k.reshape(B, S, N_HEADS, HEAD_DIM), k_norm_g[l])
        v = v.reshape(B, S, N_HEADS, HEAD_DIM)
        pool_out = pool_mixer(u, pool_w[l], pool_scale[l])
        att_out = chunk_attention(q, k, v, bias)
        mix = jnp.concatenate([pool_out, att_out], axis=-1) @ w_out[l]
        x = x + g1 * mix

        h2 = rmsnorm(x, norm2_g[l]) * (1.0 + sc2) + sh2
        x = x + g2 * hier_moe(h2, router_group_w[l], router_group_b[l], router_expert_w[l],
                              router_expert_b[l], moe_w_gate[l], moe_w_up[l], moe_w_down[l])
    return x
```

```python
import functools

import jax
import jax.numpy as jnp
from jax import lax
from jax.experimental import pallas as pl
from jax.experimental.pallas import tpu as pltpu

F32 = jnp.float32
BF16 = jnp.bfloat16

D_MODEL = 1024
CHUNK = 64
POOL_WIDTH = 512
POOL_WINDOWS = (2, 4, 8, 16)
POOL_GW = 128
POOL_HALO = 16
ATT_WIDTH = 512
N_HEADS = 8
HEAD_DIM = 64
LEFT = 512
MAX_REL = 128
IN_WIDTH = 2048
N_GROUPS = 4
EXPERTS_PER_GROUP = 8
N_EXPERTS = 32
D_EXPERT = 512
EPS = 1e-6
NEG_INF = -1e30

LANES = 128
SEQ_TILE = 512
ATT_TILE = 512
ATT_GROUP = 2 * CHUNK
ATT_BAND = LEFT + ATT_GROUP
EXPERT_ROWS = 256
ROW_TILE = 256
VMEM_LIMIT = 56 * 1024 * 1024

META_E0, META_E1, META_W0, META_W1, META_R0, META_R1 = 0, 1, 2, 3, 4, 5
ROUTER_GROUP_LANE = 0
ROUTER_EXPERT_LANE = N_GROUPS


def _params(*sem):
    return pltpu.CompilerParams(dimension_semantics=sem, vmem_limit_bytes=VMEM_LIMIT)


def _ada_kernel(c_ref, w_ref, b_ref, o_ref):
    ca = jax.nn.silu(c_ref[...])
    o_ref[0] = jnp.dot(ca, w_ref[0], precision=lax.Precision.HIGHEST,
                       preferred_element_type=F32) + b_ref[0]


def _ada_call(c_pad, ada_w, ada_b):
    depth = ada_w.shape[0]
    tn = 1024
    return pl.pallas_call(
        _ada_kernel,
        grid=(depth, 6 * D_MODEL // tn),
        in_specs=[
            pl.BlockSpec((8, D_MODEL), lambda l, n: (0, 0)),
            pl.BlockSpec((1, D_MODEL, tn), lambda l, n: (l, 0, n)),
            pl.BlockSpec((1, 1, tn), lambda l, n: (l, 0, n)),
        ],
        out_specs=pl.BlockSpec((1, 8, tn), lambda l, n: (l, 0, n)),
        out_shape=jax.ShapeDtypeStruct((depth, 8, 6 * D_MODEL), F32),
        compiler_params=_params("arbitrary", "arbitrary"),
        name="ada_mod",
    )(c_pad, ada_w, ada_b.reshape(depth, 1, 6 * D_MODEL))


def _in_kernel(x_ref, mod_ref, g_ref, w_ref, pw_ref, ps_ref, qg_ref, kg_ref, bd_ref,
               pool_ref, q_ref, k_ref, v_ref, ext_ref):
    j = pl.program_id(1)
    t = x_ref.shape[1]
    x = x_ref[0]
    ms = jnp.mean(x * x, axis=-1, keepdims=True)
    y = x * lax.rsqrt(ms + EPS) * g_ref[...]
    h = y * (1.0 + mod_ref[0, 1:2, :]) + mod_ref[0, 0:1, :]
    z = jnp.dot(h.astype(BF16), w_ref[...], preferred_element_type=F32)

    bd = bd_ref[...]
    q = z[:, POOL_WIDTH:POOL_WIDTH + ATT_WIDTH]
    k = z[:, POOL_WIDTH + ATT_WIDTH:POOL_WIDTH + 2 * ATT_WIDTH]
    qss = jnp.dot((q * q).astype(BF16), bd, preferred_element_type=F32)
    kss = jnp.dot((k * k).astype(BF16), bd, preferred_element_type=F32)
    q_ref[0] = (q * lax.rsqrt(qss * (1.0 / HEAD_DIM) + EPS) * qg_ref[...]).astype(BF16)
    k_ref[0] = (k * lax.rsqrt(kss * (1.0 / HEAD_DIM) + EPS) * kg_ref[...]).astype(BF16)
    v_ref[0] = z[:, POOL_WIDTH + 2 * ATT_WIDTH:].astype(BF16)

    @pl.when(j == 0)
    def _():
        ext_ref[0:POOL_HALO, :] = jnp.zeros((POOL_HALO, POOL_WIDTH), F32)

    ext_ref[POOL_HALO:, :] = z[:, :POOL_WIDTH]
    pos = j * t + lax.broadcasted_iota(jnp.int32, (t, 1), 0)
    outs = []
    for gi, win in enumerate(POOL_WINDOWS):
        a = ext_ref[:, gi * POOL_GW:(gi + 1) * POOL_GW]
        s = a
        shift = 1
        while shift < win:
            s = s + pltpu.roll(s, shift, 0)
            shift *= 2
        cnt = jnp.minimum(pos + 1, win).astype(F32)
        pooled = s[POOL_HALO:, :] / cnt - a[POOL_HALO:, :]
        outs.append(jnp.dot(pooled.astype(BF16), pw_ref[gi], preferred_element_type=F32))
    pool_ref[0] = (jnp.concatenate(outs, axis=1) * ps_ref[...]).astype(BF16)
    ext_ref[0:POOL_HALO, :] = ext_ref[t:t + POOL_HALO, :]


def _in_call(x, mod, g, w_in, pool_w, pool_scale, qg, kg, bd):
    b, s, d = x.shape
    t = SEQ_TILE
    tok = lambda w: pl.BlockSpec((1, t, w), lambda bi, j: (bi, j, 0))
    const2 = lambda shape: pl.BlockSpec(shape, lambda bi, j: (0, 0))
    out = jax.ShapeDtypeStruct((b, s, ATT_WIDTH), BF16)
    return pl.pallas_call(
        _in_kernel,
        grid=(b, s // t),
        in_specs=[
            tok(d),
            pl.BlockSpec((1, 6, d), lambda bi, j: (bi, 0, 0)),
            const2((1, d)),
            const2((d, IN_WIDTH)),
            pl.BlockSpec((len(POOL_WINDOWS), POOL_GW, POOL_GW), lambda bi, j: (0, 0, 0)),
            const2((1, POOL_WIDTH)),
            const2((1, ATT_WIDTH)),
            const2((1, ATT_WIDTH)),
            const2((ATT_WIDTH, ATT_WIDTH)),
        ],
        out_specs=[tok(POOL_WIDTH), tok(ATT_WIDTH), tok(ATT_WIDTH), tok(ATT_WIDTH)],
        out_shape=[jax.ShapeDtypeStruct((b, s, POOL_WIDTH), BF16), out, out, out],
        scratch_shapes=[pltpu.VMEM((t + POOL_HALO, POOL_WIDTH), F32)],
        compiler_params=_params("arbitrary", "arbitrary"),
        name="in_proj",
    )(x, mod, g, w_in, pool_w, pool_scale, qg, kg, bd)


def _att_kernel(q_ref, kp_ref, kc_ref, vp_ref, vc_ref, bias_ref, o_ref):
    i = pl.program_id(2)
    tq = q_ref.shape[1]
    lane = lax.broadcasted_iota(jnp.int32, (1, LANES), 1)
    low_half = lane < HEAD_DIM
    col = lax.broadcasted_iota(jnp.int32, (1, ATT_BAND), 1)
    for g in range(tq // ATT_GROUP):
        r0 = g * ATT_GROUP
        qg = q_ref[0, r0:r0 + ATT_GROUP, :]
        if r0 < LEFT:
            kb = jnp.concatenate([kp_ref[0, r0:LEFT, :], kc_ref[0, 0:r0 + ATT_GROUP, :]], axis=0)
            vb = jnp.concatenate([vp_ref[0, r0:LEFT, :], vc_ref[0, 0:r0 + ATT_GROUP, :]], axis=0)
        else:
            kb = kc_ref[0, r0 - LEFT:r0 + ATT_GROUP, :]
            vb = vc_ref[0, r0 - LEFT:r0 + ATT_GROUP, :]
        valid = col >= LEFT - r0 - i * tq
        halves = []
        for hh in range(2):
            qm = jnp.where(low_half if hh == 0 else jnp.logical_not(low_half), qg, jnp.zeros_like(qg))
            s = lax.dot_general(qm, kb, (((1,), (1,)), ((), ())), preferred_element_type=F32)
            s = jnp.where(valid, s + bias_ref[hh], NEG_INF)
            m = jnp.max(s, axis=-1, keepdims=True)
            e = jnp.exp(s - m)
            l = jnp.sum(e, axis=-1, keepdims=True)
            pv = jnp.dot(e.astype(BF16), vb, preferred_element_type=F32)
            halves.append(pv / l)
        o_ref[0, r0:r0 + ATT_GROUP, :] = jnp.where(low_half, halves[0], halves[1]).astype(BF16)


def _att_call(q, k, v, bias):
    b, s, _ = q.shape
    tq = ATT_TILE
    lb = tq // LEFT
    cur = pl.BlockSpec((1, tq, LANES), lambda bi, hp, i: (bi, i, hp))
    prev = pl.BlockSpec((1, LEFT, LANES), lambda bi, hp, i: (bi, jnp.maximum(i * lb - 1, 0), hp))
    return pl.pallas_call(
        _att_kernel,
        grid=(b, N_HEADS // 2, s // tq),
        in_specs=[cur, prev, cur, prev, cur,
                  pl.BlockSpec((2, ATT_GROUP, ATT_BAND), lambda bi, hp, i: (hp, 0, 0))],
        out_specs=cur,
        out_shape=jax.ShapeDtypeStruct((b, s, ATT_WIDTH), BF16),
        compiler_params=_params("arbitrary", "arbitrary", "arbitrary"),
        name="chunk_attn",
    )(q, k, k, v, v, bias)


def _group_bias(rel_bias):
    qi = jnp.arange(ATT_GROUP)[:, None]
    kj = jnp.arange(ATT_BAND)[None, :]
    rel = LEFT + qi - kj
    idx = jnp.clip(rel, -MAX_REL, MAX_REL) + MAX_REL
    first = (qi // CHUNK) * CHUNK
    visible = (kj >= first) & (kj < first + LEFT + CHUNK)
    bias = rel_bias.astype(F32)[:, idx]
    return jnp.where(visible[None], bias, NEG_INF)


def _out_kernel(x_ref, pool_ref, att_ref, mod_ref, g_ref, wo_ref, wr_ref, br_ref,
                x1_ref, h2_ref, meta_ref, cnt_ref, carry_ref):
    first = jnp.logical_and(pl.program_id(0) == 0, pl.program_id(1) == 0)
    t = x_ref.shape[1]

    @pl.when(first)
    def _():
        carry_ref[...] = jnp.zeros_like(carry_ref)

    mix = (jnp.dot(pool_ref[0], wo_ref[:POOL_WIDTH, :], preferred_element_type=F32)
           + jnp.dot(att_ref[0], wo_ref[POOL_WIDTH:, :], preferred_element_type=F32))
    x1 = x_ref[0] + mod_ref[0, 2:3, :] * mix
    x1_ref[0] = x1
    ms = jnp.mean(x1 * x1, axis=-1, keepdims=True)
    y = x1 * lax.rsqrt(ms + EPS) * g_ref[...]
    h2 = y * (1.0 + mod_ref[0, 4:5, :]) + mod_ref[0, 3:4, :]

    h2_ref[0] = h2
    hb = h2.astype(BF16)

    h_lo = (h2 - hb.astype(F32)).astype(BF16)
    r = (jnp.dot(hb, wr_ref[...], preferred_element_type=F32)
         + jnp.dot(h_lo, wr_ref[...], preferred_element_type=F32))
    logits = r[:, :LANES] + r[:, LANES:] + br_ref[...]

    lane = lax.broadcasted_iota(jnp.int32, (t, LANES), 1).astype(F32)
    ninf = jnp.float32(-jnp.inf)
    gmask = lane < N_GROUPS
    gl = jnp.where(gmask, logits, ninf)
    gmax = jnp.max(gl, axis=-1, keepdims=True)
    gsum = jnp.sum(jnp.where(gmask, jnp.exp(logits - gmax), 0.0), axis=-1, keepdims=True)
    g_p = 1.0 / gsum
    g_idx = jnp.min(jnp.where(gl == gmax, lane, float(LANES)), axis=-1, keepdims=True)
    e_lo = ROUTER_EXPERT_LANE + EXPERTS_PER_GROUP * g_idx
    el = jnp.where((lane >= e_lo) & (lane < e_lo + EXPERTS_PER_GROUP), logits, ninf)
    m1 = jnp.max(el, axis=-1, keepdims=True)
    i1 = jnp.min(jnp.where(el == m1, lane, float(LANES)), axis=-1, keepdims=True)
    el2 = jnp.where(lane == i1, ninf, el)
    m2 = jnp.max(el2, axis=-1, keepdims=True)
    i2 = jnp.min(jnp.where(el2 == m2, lane, float(LANES)), axis=-1, keepdims=True)
    e2 = jnp.exp(m2 - m1)
    w1 = g_p / (1.0 + e2)
    w2 = g_p * e2 / (1.0 + e2)

    hot = ((lane == i1) | (lane == i2))
    hot_f = hot.astype(F32)
    row = lax.broadcasted_iota(jnp.int32, (t, t), 0)
    colt = lax.broadcasted_iota(jnp.int32, (t, t), 1)
    lower = (colt < row).astype(BF16)
    before = jnp.dot(lower, hot.astype(BF16), preferred_element_type=F32) + carry_ref[0:1, :]
    r1 = jnp.sum(jnp.where(lane == i1, before, 0.0), axis=-1, keepdims=True)
    r2 = jnp.sum(jnp.where(lane == i2, before, 0.0), axis=-1, keepdims=True)
    carry_ref[0:1, :] = carry_ref[0:1, :] + jnp.sum(hot_f, axis=0, keepdims=True)
    cnt_ref[...] = jnp.broadcast_to(carry_ref[0:1, :], cnt_ref.shape)

    meta = jnp.where(lane == META_E0, i1 - ROUTER_EXPERT_LANE, 0.0)
    meta = jnp.where(lane == META_E1, i2 - ROUTER_EXPERT_LANE, meta)
    meta = jnp.where(lane == META_W0, w1, meta)
    meta = jnp.where(lane == META_W1, w2, meta)
    meta = jnp.where(lane == META_R0, r1, meta)
    meta = jnp.where(lane == META_R1, r2, meta)
    meta_ref[0] = meta


def _out_call(x, pool, att, mod, g, w_out, wr, br):
    b, s, d = x.shape
    t = SEQ_TILE
    tok = lambda w: pl.BlockSpec((1, t, w), lambda bi, j: (bi, j, 0))
    const2 = lambda shape: pl.BlockSpec(shape, lambda bi, j: (0, 0))
    return pl.pallas_call(
        _out_kernel,
        grid=(b, s // t),
        in_specs=[tok(d), tok(POOL_WIDTH), tok(ATT_WIDTH),
                  pl.BlockSpec((1, 6, d), lambda bi, j: (bi, 0, 0)),
                  const2((1, d)), const2((d, d)), const2((d, 2 * LANES)), const2((1, LANES))],
        out_specs=[tok(d), tok(d), tok(LANES), const2((8, LANES))],
        out_shape=[jax.ShapeDtypeStruct((b, s, d), F32),
                   jax.ShapeDtypeStruct((b, s, d), F32),
                   jax.ShapeDtypeStruct((b, s, LANES), F32),
                   jax.ShapeDtypeStruct((8, LANES), F32)],
        scratch_shapes=[pltpu.VMEM((8, LANES), F32)],
        compiler_params=_params("arbitrary", "arbitrary"),
        name="out_proj_router",
    )(x, pool, att, mod, g, w_out, wr, br)


def _row_copies(src_ref, dst_ref, idx_ref, sem, rows, gather):
    def issue(r, carry):
        tgt = idx_ref[0, 0, r]
        if gather:
            pltpu.make_async_copy(src_ref.at[pl.ds(tgt, 1)], dst_ref.at[pl.ds(r, 1)], sem).start()
        else:
            pltpu.make_async_copy(src_ref.at[pl.ds(r, 1)], dst_ref.at[pl.ds(tgt, 1)], sem).start()
        return carry
    lax.fori_loop(0, rows, issue, 0, unroll=8)


def _wait_rows(vmem_ref, hbm_ref, sem, rows, gather):
    hbm_rows = hbm_ref.at[pl.ds(0, rows)]
    if gather:
        pltpu.make_async_copy(hbm_rows, vmem_ref, sem).wait()
    else:
        pltpu.make_async_copy(vmem_ref, hbm_rows, sem).wait()


def _disp_kernel(d0_ref, d1_ref, h_ref, xs_in_ref, xs_ref, sems):
    del xs_in_ref
    rows = h_ref.shape[0]
    _row_copies(h_ref, xs_ref, d0_ref, sems.at[0], rows, gather=False)
    _row_copies(h_ref, xs_ref, d1_ref, sems.at[1], rows, gather=False)
    _wait_rows(h_ref, xs_ref, sems.at[0], rows, gather=False)
    _wait_rows(h_ref, xs_ref, sems.at[1], rows, gather=False)


def _disp_call(h2, d0, d1, xs_init):
    n, w = h2.shape
    t = ROW_TILE
    idx = pl.BlockSpec((1, 1, t), lambda i: (i, 0, 0), memory_space=pltpu.SMEM)
    return pl.pallas_call(
        _disp_kernel,
        grid=(n // t,),
        in_specs=[idx, idx, pl.BlockSpec((t, w), lambda i: (i, 0)),
                  pl.BlockSpec(memory_space=pl.ANY)],
        out_specs=pl.BlockSpec(memory_space=pl.ANY),
        out_shape=jax.ShapeDtypeStruct(xs_init.shape, xs_init.dtype),
        scratch_shapes=[pltpu.SemaphoreType.DMA((2,))],
        input_output_aliases={3: 0},
        compiler_params=_params("arbitrary"),
        name="dispatch_rows",
    )(d0.reshape(n // t, 1, t), d1.reshape(n // t, 1, t), h2, xs_init)


def _exp_kernel(be_ref, nu_ref, xs_ref, wg_ref, wu_ref, wd_ref, y_ref):
    del be_ref
    i = pl.program_id(0)

    @pl.when(i < nu_ref[0])
    def _():
        xb = xs_ref[...].astype(BF16)
        gate = jnp.dot(xb, wg_ref[0], preferred_element_type=F32)
        up = jnp.dot(xb, wu_ref[0], preferred_element_type=F32)
        mid = (jax.nn.silu(gate) * up).astype(BF16)
        y_ref[...] = jnp.dot(mid, wd_ref[0], preferred_element_type=F32)

    @pl.when(i >= nu_ref[0])
    def _():
        y_ref[...] = jnp.zeros_like(y_ref)


def _exp_call(blk_e, n_used, xs, wg, wu, wd):
    p, w = xs.shape
    r = EXPERT_ROWS
    wspec = lambda shape: pl.BlockSpec((1,) + shape, lambda i, be, nu: (be[i], 0, 0))
    grid_spec = pltpu.PrefetchScalarGridSpec(
        num_scalar_prefetch=2,
        grid=(p // r,),
        in_specs=[pl.BlockSpec((r, w), lambda i, be, nu: (i, 0)),
                  wspec((D_MODEL, D_EXPERT)), wspec((D_MODEL, D_EXPERT)), wspec((D_EXPERT, D_MODEL))],
        out_specs=pl.BlockSpec((r, D_MODEL), lambda i, be, nu: (i, 0)),
    )
    return pl.pallas_call(
        _exp_kernel,
        grid_spec=grid_spec,
        out_shape=jax.ShapeDtypeStruct((p, D_MODEL), F32),
        compiler_params=_params("arbitrary"),
        name="experts",
    )(blk_e, n_used, xs, wg, wu, wd)


def _comb_kernel(d0_ref, d1_ref, x_ref, meta_ref, mod_ref, ys_ref, o_ref, y0_ref, y1_ref, sems):
    rows = x_ref.shape[1]
    _row_copies(ys_ref, y0_ref, d0_ref, sems.at[0], rows, gather=True)
    _row_copies(ys_ref, y1_ref, d1_ref, sems.at[1], rows, gather=True)
    _wait_rows(y0_ref, ys_ref, sems.at[0], rows, gather=True)
    _wait_rows(y1_ref, ys_ref, sems.at[1], rows, gather=True)
    w0 = meta_ref[0, :, META_W0:META_W0 + 1]
    w1 = meta_ref[0, :, META_W1:META_W1 + 1]
    moe = y0_ref[...] * w0 + y1_ref[...] * w1
    o_ref[0] = x_ref[0] + mod_ref[0, 5:6, :] * moe


def _comb_call(x1, meta, mod, ys, d0, d1):
    b, s, d = x1.shape
    t = ROW_TILE
    steps = s // t
    idx = pl.BlockSpec((1, 1, t), lambda bi, j: (bi * steps + j, 0, 0), memory_space=pltpu.SMEM)
    tok = lambda w: pl.BlockSpec((1, t, w), lambda bi, j: (bi, j, 0))
    return pl.pallas_call(
        _comb_kernel,
        grid=(b, steps),
        in_specs=[idx, idx, tok(d), tok(LANES),
                  pl.BlockSpec((1, 6, d), lambda bi, j: (bi, 0, 0)),
                  pl.BlockSpec(memory_space=pl.ANY)],
        out_specs=tok(d),
        out_shape=jax.ShapeDtypeStruct((b, s, d), F32),
        scratch_shapes=[pltpu.VMEM((t, d), F32), pltpu.VMEM((t, d), F32),
                        pltpu.SemaphoreType.DMA((2,))],
        compiler_params=_params("arbitrary", "arbitrary"),
        name="combine_rows",
    )(d0.reshape(b * steps, 1, t), d1.reshape(b * steps, 1, t), x1, meta, mod, ys)


def _dispatch_plan(meta, counts, n_rows):
    r = EXPERT_ROWS
    e0 = meta[:, META_E0].astype(jnp.int32)
    e1 = meta[:, META_E1].astype(jnp.int32)
    cnt = counts[0, ROUTER_EXPERT_LANE:ROUTER_EXPERT_LANE + N_EXPERTS].astype(jnp.int32)
    padded = ((cnt + r - 1) // r) * r
    pad_end = jnp.cumsum(padded)
    pad_start = pad_end - padded
    d0 = pad_start[e0] + meta[:, META_R0].astype(jnp.int32)
    d1 = pad_start[e1] + meta[:, META_R1].astype(jnp.int32)
    nblk = n_rows // r
    blk_row = jnp.arange(nblk, dtype=jnp.int32) * r
    blk_e = jnp.minimum(jnp.sum((pad_end[None, :] <= blk_row[:, None]).astype(jnp.int32), axis=1),
                        N_EXPERTS - 1)
    n_used = (pad_end[-1:] // r).astype(jnp.int32)
    return d0, d1, blk_e, n_used


def kernel(x, c, ada_w, ada_b, norm1_g, norm2_g, w_in, pool_w, pool_scale, q_norm_g, k_norm_g, rel_bias,
           w_out, router_group_w, router_group_b, router_expert_w, router_expert_b,
           moe_w_gate, moe_w_up, moe_w_down):
    b, s, d = x.shape
    depth = ada_w.shape[0]
    n = b * s
    assert d == D_MODEL and b <= 8 and s % SEQ_TILE == 0 and s % ATT_TILE == 0 and n % ROW_TILE == 0
    p_rows = n * 2 + N_EXPERTS * EXPERT_ROWS

    c_pad = jnp.zeros((8, d), F32).at[:b].set(c)
    mod_all = _ada_call(c_pad, ada_w, ada_b)
    bias = _group_bias(rel_bias)
    head_block = jnp.arange(ATT_WIDTH) // HEAD_DIM
    bd = (head_block[:, None] == head_block[None, :]).astype(BF16)
    sm_scale = HEAD_DIM ** -0.5

    for l in range(depth):
        mod = mod_all[l, :b].reshape(b, 6, d)
        qg = (jnp.tile(q_norm_g[l], N_HEADS) * sm_scale).reshape(1, ATT_WIDTH)
        kg = jnp.tile(k_norm_g[l], N_HEADS).reshape(1, ATT_WIDTH)
        pool, q, k, v = _in_call(x, mod, norm1_g[l].reshape(1, d), w_in[l].astype(BF16),
                                 pool_w[l].astype(BF16), pool_scale[l].reshape(1, POOL_WIDTH), qg, kg, bd)
        att = _att_call(q, k, v, bias)

        wr = jnp.zeros((d, LANES), F32)
        wr = wr.at[:, ROUTER_GROUP_LANE:ROUTER_GROUP_LANE + N_GROUPS].set(router_group_w[l])
        wr = wr.at[:, ROUTER_EXPERT_LANE:ROUTER_EXPERT_LANE + N_EXPERTS].set(router_expert_w[l])
        wr_hi = wr.astype(BF16)
        wr_lo = (wr - wr_hi.astype(F32)).astype(BF16)
        br = jnp.zeros((1, LANES), F32)
        br = br.at[0, ROUTER_GROUP_LANE:ROUTER_GROUP_LANE + N_GROUPS].set(router_group_b[l])
        br = br.at[0, ROUTER_EXPERT_LANE:ROUTER_EXPERT_LANE + N_EXPERTS].set(router_expert_b[l])
        x1, h2, meta, counts = _out_call(x, pool, att, mod, norm2_g[l].reshape(1, d),
                                         w_out[l].astype(BF16), jnp.concatenate([wr_hi, wr_lo], axis=1), br)

        d0, d1, blk_e, n_used = _dispatch_plan(meta.reshape(n, LANES), counts, p_rows)
        xs = _disp_call(h2.reshape(n, d), d0, d1, jnp.zeros((p_rows, d), F32))
        ys = _exp_call(blk_e, n_used, xs, moe_w_gate[l].astype(BF16), moe_w_up[l].astype(BF16),
                       moe_w_down[l].astype(BF16))
        x = _comb_call(x1, meta, mod, ys, d0, d1)
    return x
```

```python
import functools

import jax
import jax.numpy as jnp
from jax import lax
from jax.experimental import pallas as pl
from jax.experimental.pallas import tpu as pltpu

F32 = jnp.float32
BF16 = jnp.bfloat16

D_MODEL = 1024
CHUNK = 64
POOL_WIDTH = 512
POOL_WINDOWS = (2, 4, 8, 16)
POOL_GW = 128
POOL_HALO = 16
ATT_WIDTH = 512
N_HEADS = 8
HEAD_DIM = 64
LEFT = 512
MAX_REL = 128
IN_WIDTH = 2048
N_GROUPS = 4
EXPERTS_PER_GROUP = 8
N_EXPERTS = 32
D_EXPERT = 512
EPS = 1e-6
NEG_INF = -1e30

LANES = 128
SEQ_TILE = 512
ATT_TILE = 512
ATT_GROUP = 2 * CHUNK
ATT_BAND = LEFT + ATT_GROUP
EXPERT_ROWS = 512
ROW_TILE = 256
ROW_UNROLL = 8
VMEM_LIMIT = 56 * 1024 * 1024

META_E0, META_E1, META_W0, META_W1, META_R0, META_R1 = 0, 1, 2, 3, 4, 5
ROUTER_GROUP_LANE = 0
ROUTER_EXPERT_LANE = N_GROUPS


def _params(*sem):
    return pltpu.CompilerParams(dimension_semantics=sem, vmem_limit_bytes=VMEM_LIMIT)


def _ada_kernel(c_ref, w_ref, b_ref, o_ref):
    ca = jax.nn.silu(c_ref[...])
    o_ref[0] = jnp.dot(ca, w_ref[0], precision=lax.Precision.HIGHEST,
                       preferred_element_type=F32) + b_ref[0]


def _ada_call(c_pad, ada_w, ada_b):
    depth = ada_w.shape[0]
    tn = 1024
    return pl.pallas_call(
        _ada_kernel,
        grid=(depth, 6 * D_MODEL // tn),
        in_specs=[
            pl.BlockSpec((8, D_MODEL), lambda l, n: (0, 0)),
            pl.BlockSpec((1, D_MODEL, tn), lambda l, n: (l, 0, n)),
            pl.BlockSpec((1, 1, tn), lambda l, n: (l, 0, n)),
        ],
        out_specs=pl.BlockSpec((1, 8, tn), lambda l, n: (l, 0, n)),
        out_shape=jax.ShapeDtypeStruct((depth, 8, 6 * D_MODEL), F32),
        compiler_params=_params("arbitrary", "arbitrary"),
        name="ada_mod",
    )(c_pad, ada_w, ada_b.reshape(depth, 1, 6 * D_MODEL))


def _in_kernel(x_ref, mod_ref, g_ref, w_ref, pw_ref, ps_ref, qg_ref, kg_ref, bd_ref,
               pool_ref, q_ref, k_ref, v_ref, ext_ref):
    j = pl.program_id(1)
    t = x_ref.shape[1]
    x = x_ref[0]
    ms = jnp.mean(x * x, axis=-1, keepdims=True)
    y = x * lax.rsqrt(ms + EPS) * g_ref[...]
    h = y * (1.0 + mod_ref[0, 1:2, :]) + mod_ref[0, 0:1, :]
    z = jnp.dot(h.astype(BF16), w_ref[...], preferred_element_type=F32)

    bd = bd_ref[...]
    q = z[:, POOL_WIDTH:POOL_WIDTH + ATT_WIDTH]
    k = z[:, POOL_WIDTH + ATT_WIDTH:POOL_WIDTH + 2 * ATT_WIDTH]
    qss = jnp.dot((q * q).astype(BF16), bd, preferred_element_type=F32)
    kss = jnp.dot((k * k).astype(BF16), bd, preferred_element_type=F32)
    q_ref[0] = (q * lax.rsqrt(qss * (1.0 / HEAD_DIM) + EPS) * qg_ref[...]).astype(BF16)
    k_ref[0] = (k * lax.rsqrt(kss * (1.0 / HEAD_DIM) + EPS) * kg_ref[...]).astype(BF16)
    v_ref[0] = z[:, POOL_WIDTH + 2 * ATT_WIDTH:].astype(BF16)

    @pl.when(j == 0)
    def _():
        ext_ref[0:POOL_HALO, :] = jnp.zeros((POOL_HALO, POOL_WIDTH), F32)

    ext_ref[POOL_HALO:, :] = z[:, :POOL_WIDTH]
    pos = j * t + lax.broadcasted_iota(jnp.int32, (t, 1), 0)
    outs = []
    for gi, win in enumerate(POOL_WINDOWS):
        a = ext_ref[:, gi * POOL_GW:(gi + 1) * POOL_GW]
        s = a
        shift = 1
        while shift < win:
            s = s + pltpu.roll(s, shift, 0)
            shift *= 2
        cnt = jnp.minimum(pos + 1, win).astype(F32)
        pooled = s[POOL_HALO:, :] / cnt - a[POOL_HALO:, :]
        outs.append(jnp.dot(pooled.astype(BF16), pw_ref[gi], preferred_element_type=F32))
    pool_ref[0] = (jnp.concatenate(outs, axis=1) * ps_ref[...]).astype(BF16)
    ext_ref[0:POOL_HALO, :] = ext_ref[t:t + POOL_HALO, :]


def _in_call(x, mod, g, w_in, pool_w, pool_scale, qg, kg, bd):
    b, s, d = x.shape
    t = SEQ_TILE
    tok = lambda w: pl.BlockSpec((1, t, w), lambda bi, j: (bi, j, 0))
    const2 = lambda shape: pl.BlockSpec(shape, lambda bi, j: (0, 0))
    out = jax.ShapeDtypeStruct((b, s, ATT_WIDTH), BF16)
    return pl.pallas_call(
        _in_kernel,
        grid=(b, s // t),
        in_specs=[
            tok(d),
            pl.BlockSpec((1, 6, d), lambda bi, j: (bi, 0, 0)),
            const2((1, d)),
            const2((d, IN_WIDTH)),
            pl.BlockSpec((len(POOL_WINDOWS), POOL_GW, POOL_GW), lambda bi, j: (0, 0, 0)),
            const2((1, POOL_WIDTH)),
            const2((1, ATT_WIDTH)),
            const2((1, ATT_WIDTH)),
            const2((ATT_WIDTH, ATT_WIDTH)),
        ],
        out_specs=[tok(POOL_WIDTH), tok(ATT_WIDTH), tok(ATT_WIDTH), tok(ATT_WIDTH)],
        out_shape=[jax.ShapeDtypeStruct((b, s, POOL_WIDTH), BF16), out, out, out],
        scratch_shapes=[pltpu.VMEM((t + POOL_HALO, POOL_WIDTH), F32)],
        compiler_params=_params("arbitrary", "arbitrary"),
        name="in_proj",
    )(x, mod, g, w_in, pool_w, pool_scale, qg, kg, bd)


def _att_kernel(q_ref, kp_ref, kc_ref, vp_ref, vc_ref, bias_ref, o_ref):
    i = pl.program_id(2)
    tq = q_ref.shape[1]
    lane = lax.broadcasted_iota(jnp.int32, (1, LANES), 1)
    low_half = lane < HEAD_DIM
    col = lax.broadcasted_iota(jnp.int32, (1, ATT_BAND), 1)
    for g in range(tq // ATT_GROUP):
        r0 = g * ATT_GROUP
        qg = q_ref[0, r0:r0 + ATT_GROUP, :]
        if r0 < LEFT:
            kb = jnp.concatenate([kp_ref[0, r0:LEFT, :], kc_ref[0, 0:r0 + ATT_GROUP, :]], axis=0)
            vb = jnp.concatenate([vp_ref[0, r0:LEFT, :], vc_ref[0, 0:r0 + ATT_GROUP, :]], axis=0)
        else:
            kb = kc_ref[0, r0 - LEFT:r0 + ATT_GROUP, :]
            vb = vc_ref[0, r0 - LEFT:r0 + ATT_GROUP, :]
        valid = col >= LEFT - r0 - i * tq
        halves = []
        for hh in range(2):
            qm = jnp.where(low_half if hh == 0 else jnp.logical_not(low_half), qg, jnp.zeros_like(qg))
            s = lax.dot_general(qm, kb, (((1,), (1,)), ((), ())), preferred_element_type=F32)
            s = jnp.where(valid, s + bias_ref[hh], NEG_INF)
            m = jnp.max(s, axis=-1, keepdims=True)
            e = jnp.exp(s - m)
            l = jnp.sum(e, axis=-1, keepdims=True)
            pv = jnp.dot(e.astype(BF16), vb, preferred_element_type=F32)
            halves.append(pv / l)
        o_ref[0, r0:r0 + ATT_GROUP, :] = jnp.where(low_half, halves[0], halves[1]).astype(BF16)


def _att_call(q, k, v, bias):
    b, s, _ = q.shape
    tq = ATT_TILE
    lb = tq // LEFT
    cur = pl.BlockSpec((1, tq, LANES), lambda bi, hp, i: (bi, i, hp))
    prev = pl.BlockSpec((1, LEFT, LANES), lambda bi, hp, i: (bi, jnp.maximum(i * lb - 1, 0), hp))
    return pl.pallas_call(
        _att_kernel,
        grid=(b, N_HEADS // 2, s // tq),
        in_specs=[cur, prev, cur, prev, cur,
                  pl.BlockSpec((2, ATT_GROUP, ATT_BAND), lambda bi, hp, i: (hp, 0, 0))],
        out_specs=cur,
        out_shape=jax.ShapeDtypeStruct((b, s, ATT_WIDTH), BF16),
        compiler_params=_params("arbitrary", "arbitrary", "arbitrary"),
        name="chunk_attn",
    )(q, k, k, v, v, bias)


def _group_bias(rel_bias):
    qi = jnp.arange(ATT_GROUP)[:, None]
    kj = jnp.arange(ATT_BAND)[None, :]
    first = (qi // CHUNK) * CHUNK
    visible = (kj >= first) & (kj < first + LEFT + CHUNK)
    lo = ATT_GROUP - 1
    assert lo <= MAX_REL
    tab = rel_bias.astype(F32)
    n_clipped = ATT_BAND - 1 - MAX_REL
    by_rel = jnp.concatenate([tab[:, MAX_REL - lo:], jnp.repeat(tab[:, -1:], n_clipped, axis=1)], axis=1)
    rev = by_rel[:, ::-1]
    bias = jnp.stack([rev[:, lo - q:lo - q + ATT_BAND] for q in range(ATT_GROUP)], axis=1)
    return jnp.where(visible[None], bias, NEG_INF)


def _out_kernel(x_ref, pool_ref, att_ref, mod_ref, g_ref, wo_ref, wr_ref, br_ref,
                x1_ref, h2_ref, meta_ref, cnt_ref, carry_ref):
    first = jnp.logical_and(pl.program_id(0) == 0, pl.program_id(1) == 0)
    t = x_ref.shape[1]

    @pl.when(first)
    def _():
        carry_ref[...] = jnp.zeros_like(carry_ref)

    mix = (jnp.dot(pool_ref[0], wo_ref[:POOL_WIDTH, :], preferred_element_type=F32)
           + jnp.dot(att_ref[0], wo_ref[POOL_WIDTH:, :], preferred_element_type=F32))
    x1 = x_ref[0] + mod_ref[0, 2:3, :] * mix
    x1_ref[0] = x1
    ms = jnp.mean(x1 * x1, axis=-1, keepdims=True)
    y = x1 * lax.rsqrt(ms + EPS) * g_ref[...]
    h2 = y * (1.0 + mod_ref[0, 4:5, :]) + mod_ref[0, 3:4, :]

    h2_ref[0] = h2
    hb = h2.astype(BF16)

    h_lo = (h2 - hb.astype(F32)).astype(BF16)
    r = (jnp.dot(hb, wr_ref[...], preferred_element_type=F32)
         + jnp.dot(h_lo, wr_ref[...], preferred_element_type=F32))
    logits = r[:, :LANES] + r[:, LANES:] + br_ref[...]

    lane = lax.broadcasted_iota(jnp.int32, (t, LANES), 1).astype(F32)
    ninf = jnp.float32(-jnp.inf)
    gmask = lane < N_GROUPS
    gl = jnp.where(gmask, logits, ninf)
    gmax = jnp.max(gl, axis=-1, keepdims=True)
    gsum = jnp.sum(jnp.where(gmask, jnp.exp(logits - gmax), 0.0), axis=-1, keepdims=True)
    g_p = 1.0 / gsum
    g_idx = jnp.min(jnp.where(gl == gmax, lane, float(LANES)), axis=-1, keepdims=True)
    e_lo = ROUTER_EXPERT_LANE + EXPERTS_PER_GROUP * g_idx
    el = jnp.where((lane >= e_lo) & (lane < e_lo + EXPERTS_PER_GROUP), logits, ninf)
    m1 = jnp.max(el, axis=-1, keepdims=True)
    i1 = jnp.min(jnp.where(el == m1, lane, float(LANES)), axis=-1, keepdims=True)
    el2 = jnp.where(lane == i1, ninf, el)
    m2 = jnp.max(el2, axis=-1, keepdims=True)
    i2 = jnp.min(jnp.where(el2 == m2, lane, float(LANES)), axis=-1, keepdims=True)
    e2 = jnp.exp(m2 - m1)
    w1 = g_p / (1.0 + e2)
    w2 = g_p * e2 / (1.0 + e2)

    hot = ((lane == i1) | (lane == i2))
    hot_f = hot.astype(F32)
    row = lax.broadcasted_iota(jnp.int32, (t, t), 0)
    colt = lax.broadcasted_iota(jnp.int32, (t, t), 1)
    lower = (colt < row).astype(BF16)
    before = jnp.dot(lower, hot.astype(BF16), preferred_element_type=F32) + carry_ref[0:1, :]
    r1 = jnp.sum(jnp.where(lane == i1, before, 0.0), axis=-1, keepdims=True)
    r2 = jnp.sum(jnp.where(lane == i2, before, 0.0), axis=-1, keepdims=True)
    carry_ref[0:1, :] = carry_ref[0:1, :] + jnp.sum(hot_f, axis=0, keepdims=True)
    cnt_ref[...] = jnp.broadcast_to(carry_ref[0:1, :], cnt_ref.shape)

    meta = jnp.where(lane == META_E0, i1 - ROUTER_EXPERT_LANE, 0.0)
    meta = jnp.where(lane == META_E1, i2 - ROUTER_EXPERT_LANE, meta)
    meta = jnp.where(lane == META_W0, w1, meta)
    meta = jnp.where(lane == META_W1, w2, meta)
    meta = jnp.where(lane == META_R0, r1, meta)
    meta = jnp.where(lane == META_R1, r2, meta)
    meta_ref[0] = meta


def _out_call(x, pool, att, mod, g, w_out, wr, br):
    b, s, d = x.shape
    t = SEQ_TILE
    tok = lambda w: pl.BlockSpec((1, t, w), lambda bi, j: (bi, j, 0))
    const2 = lambda shape: pl.BlockSpec(shape, lambda bi, j: (0, 0))
    return pl.pallas_call(
        _out_kernel,
        grid=(b, s // t),
        in_specs=[tok(d), tok(POOL_WIDTH), tok(ATT_WIDTH),
                  pl.BlockSpec((1, 6, d), lambda bi, j: (bi, 0, 0)),
                  const2((1, d)), const2((d, d)), const2((d, 2 * LANES)), const2((1, LANES))],
        out_specs=[tok(d), tok(d), tok(LANES), const2((8, LANES))],
        out_shape=[jax.ShapeDtypeStruct((b, s, d), F32),
                   jax.ShapeDtypeStruct((b, s, d), F32),
                   jax.ShapeDtypeStruct((b, s, LANES), F32),
                   jax.ShapeDtypeStruct((8, LANES), F32)],
        scratch_shapes=[pltpu.VMEM((8, LANES), F32)],
        compiler_params=_params("arbitrary", "arbitrary"),
        name="out_proj_router",
    )(x, pool, att, mod, g, w_out, wr, br)


def _row_copies(src_ref, dst_ref, idx_ref, sem, rows, gather):
    def issue(blk, carry):
        for u in range(ROW_UNROLL):
            r = blk * ROW_UNROLL + u
            tgt = idx_ref[0, 0, r]
            if gather:
                cp = pltpu.make_async_copy(src_ref.at[pl.ds(tgt, 1)], dst_ref.at[pl.ds(r, 1)], sem)
            else:
                cp = pltpu.make_async_copy(src_ref.at[pl.ds(r, 1)], dst_ref.at[pl.ds(tgt, 1)], sem)
            cp.start(priority=u % 2)
        return carry
    lax.fori_loop(0, rows // ROW_UNROLL, issue, 0)


def _wait_rows(vmem_ref, hbm_ref, sem, rows, gather):
    hbm_rows = hbm_ref.at[pl.ds(0, rows)]
    if gather:
        pltpu.make_async_copy(hbm_rows, vmem_ref, sem).wait()
    else:
        pltpu.make_async_copy(vmem_ref, hbm_rows, sem).wait()


def _disp_kernel(d0_ref, d1_ref, h_ref, xs_in_ref, xs_ref, sems):
    del xs_in_ref
    rows = h_ref.shape[0]
    _row_copies(h_ref, xs_ref, d0_ref, sems.at[0], rows, gather=False)
    _row_copies(h_ref, xs_ref, d1_ref, sems.at[1], rows, gather=False)
    _wait_rows(h_ref, xs_ref, sems.at[0], rows, gather=False)
    _wait_rows(h_ref, xs_ref, sems.at[1], rows, gather=False)


def _disp_call(h2, d0, d1, xs_init):
    n, w = h2.shape
    t = ROW_TILE
    idx = pl.BlockSpec((1, 1, t), lambda i: (i, 0, 0), memory_space=pltpu.SMEM)
    return pl.pallas_call(
        _disp_kernel,
        grid=(n // t,),
        in_specs=[idx, idx, pl.BlockSpec((t, w), lambda i: (i, 0)),
                  pl.BlockSpec(memory_space=pl.ANY)],
        out_specs=pl.BlockSpec(memory_space=pl.ANY),
        out_shape=jax.ShapeDtypeStruct(xs_init.shape, xs_init.dtype),
        scratch_shapes=[pltpu.SemaphoreType.DMA((2,))],
        input_output_aliases={3: 0},
        compiler_params=_params("arbitrary"),
        name="dispatch_rows",
    )(d0.reshape(n // t, 1, t), d1.reshape(n // t, 1, t), h2, xs_init)


def _exp_kernel(be_ref, nu_ref, xs_ref, wg_ref, wu_ref, wd_ref, y_ref):
    del be_ref
    i = pl.program_id(0)

    @pl.when(i < nu_ref[0])
    def _():
        xb = xs_ref[...].astype(BF16)
        gate = jnp.dot(xb, wg_ref[0], preferred_element_type=F32)
        up = jnp.dot(xb, wu_ref[0], preferred_element_type=F32)
        mid = (jax.nn.silu(gate) * up).astype(BF16)
        y_ref[...] = jnp.dot(mid, wd_ref[0], preferred_element_type=F32)

    @pl.when(i >= nu_ref[0])
    def _():
        y_ref[...] = jnp.zeros_like(y_ref)


def _exp_call(blk_e, n_used, xs, wg, wu, wd):
    p, w = xs.shape
    r = EXPERT_ROWS
    wspec = lambda shape: pl.BlockSpec((1,) + shape, lambda i, be, nu: (be[i], 0, 0))
    grid_spec = pltpu.PrefetchScalarGridSpec(
        num_scalar_prefetch=2,
        grid=(p // r,),
        in_specs=[pl.BlockSpec((r, w), lambda i, be, nu: (i, 0)),
                  wspec((D_MODEL, D_EXPERT)), wspec((D_MODEL, D_EXPERT)), wspec((D_EXPERT, D_MODEL))],
        out_specs=pl.BlockSpec((r, D_MODEL), lambda i, be, nu: (i, 0)),
    )
    return pl.pallas_call(
        _exp_kernel,
        grid_spec=grid_spec,
        out_shape=jax.ShapeDtypeStruct((p, D_MODEL), F32),
        compiler_params=_params("arbitrary"),
        name="experts",
    )(blk_e, n_used, xs, wg, wu, wd)


def _comb_kernel(d0_ref, d1_ref, x_ref, meta_ref, mod_ref, ys_ref, o_ref, y0_ref, y1_ref, sems):
    rows = x_ref.shape[1]
    _row_copies(ys_ref, y0_ref, d0_ref, sems.at[0], rows, gather=True)
    _row_copies(ys_ref, y1_ref, d1_ref, sems.at[1], rows, gather=True)
    _wait_rows(y0_ref, ys_ref, sems.at[0], rows, gather=True)
    _wait_rows(y1_ref, ys_ref, sems.at[1], rows, gather=True)
    w0 = meta_ref[0, :, META_W0:META_W0 + 1]
    w1 = meta_ref[0, :, META_W1:META_W1 + 1]
    moe = y0_ref[...] * w0 + y1_ref[...] * w1
    o_ref[0] = x_ref[0] + mod_ref[0, 5:6, :] * moe


def _comb_call(x1, meta, mod, ys, d0, d1):
    b, s, d = x1.shape
    t = ROW_TILE
    steps = s // t
    idx = pl.BlockSpec((1, 1, t), lambda bi, j: (bi * steps + j, 0, 0), memory_space=pltpu.SMEM)
    tok = lambda w: pl.BlockSpec((1, t, w), lambda bi, j: (bi, j, 0))
    return pl.pallas_call(
        _comb_kernel,
        grid=(b, steps),
        in_specs=[idx, idx, tok(d), tok(LANES),
                  pl.BlockSpec((1, 6, d), lambda bi, j: (bi, 0, 0)),
                  pl.BlockSpec(memory_space=pl.ANY)],
        out_specs=tok(d),
        out_shape=jax.ShapeDtypeStruct((b, s, d), F32),
        scratch_shapes=[pltpu.VMEM((t, d), F32), pltpu.VMEM((t, d), F32),
                        pltpu.SemaphoreType.DMA((2,))],
        compiler_params=_params("arbitrary", "arbitrary"),
        name="combine_rows",
    )(d0.reshape(b * steps, 1, t), d1.reshape(b * steps, 1, t), x1, meta, mod, ys)


def _dispatch_plan(meta, counts, n_rows):
    r = EXPERT_ROWS
    e0 = meta[:, META_E0].astype(jnp.int32)
    e1 = meta[:, META_E1].astype(jnp.int32)
    cnt = counts[0, ROUTER_EXPERT_LANE:ROUTER_EXPERT_LANE + N_EXPERTS].astype(jnp.int32)
    padded = ((cnt + r - 1) // r) * r
    pad_end = jnp.cumsum(padded)
    pad_start = pad_end - padded
    experts = jnp.arange(N_EXPERTS, dtype=jnp.int32)[None, :]
    start_of = lambda e: jnp.sum(jnp.where(e[:, None] == experts, pad_start[None, :], 0), axis=1)
    d0 = start_of(e0) + meta[:, META_R0].astype(jnp.int32)
    d1 = start_of(e1) + meta[:, META_R1].astype(jnp.int32)
    nblk = n_rows // r
    blk_row = jnp.arange(nblk, dtype=jnp.int32) * r
    blk_e = jnp.minimum(jnp.sum((pad_end[None, :] <= blk_row[:, None]).astype(jnp.int32), axis=1),
                        N_EXPERTS - 1)
    n_used = (pad_end[-1:] // r).astype(jnp.int32)
    return d0, d1, blk_e, n_used


def kernel(x, c, ada_w, ada_b, norm1_g, norm2_g, w_in, pool_w, pool_scale, q_norm_g, k_norm_g, rel_bias,
           w_out, router_group_w, router_group_b, router_expert_w, router_expert_b,
           moe_w_gate, moe_w_up, moe_w_down):
    b, s, d = x.shape
    depth = ada_w.shape[0]
    n = b * s
    assert d == D_MODEL and b <= 8 and s % SEQ_TILE == 0 and s % ATT_TILE == 0 and n % ROW_TILE == 0
    p_rows = n * 2 + N_EXPERTS * EXPERT_ROWS

    c_pad = jnp.zeros((8, d), F32).at[:b].set(c)
    mod_all = _ada_call(c_pad, ada_w, ada_b)
    bias = _group_bias(rel_bias)
    head_block = jnp.arange(ATT_WIDTH) // HEAD_DIM
    bd = (head_block[:, None] == head_block[None, :]).astype(BF16)
    sm_scale = HEAD_DIM ** -0.5

    for l in range(depth):
        mod = mod_all[l, :b].reshape(b, 6, d)
        qg = (jnp.tile(q_norm_g[l], N_HEADS) * sm_scale).reshape(1, ATT_WIDTH)
        kg = jnp.tile(k_norm_g[l], N_HEADS).reshape(1, ATT_WIDTH)
        pool, q, k, v = _in_call(x, mod, norm1_g[l].reshape(1, d), w_in[l].astype(BF16),
                                 pool_w[l].astype(BF16), pool_scale[l].reshape(1, POOL_WIDTH), qg, kg, bd)
        att = _att_call(q, k, v, bias)

        wr = jnp.zeros((d, LANES), F32)
        wr = wr.at[:, ROUTER_GROUP_LANE:ROUTER_GROUP_LANE + N_GROUPS].set(router_group_w[l])
        wr = wr.at[:, ROUTER_EXPERT_LANE:ROUTER_EXPERT_LANE + N_EXPERTS].set(router_expert_w[l])
        wr_hi = wr.astype(BF16)
        wr_lo = (wr - wr_hi.astype(F32)).astype(BF16)
        br = jnp.zeros((1, LANES), F32)
        br = br.at[0, ROUTER_GROUP_LANE:ROUTER_GROUP_LANE + N_GROUPS].set(router_group_b[l])
        br = br.at[0, ROUTER_EXPERT_LANE:ROUTER_EXPERT_LANE + N_EXPERTS].set(router_expert_b[l])
        x1, h2, meta, counts = _out_call(x, pool, att, mod, norm2_g[l].reshape(1, d),
                                         w_out[l].astype(BF16), jnp.concatenate([wr_hi, wr_lo], axis=1), br)

        d0, d1, blk_e, n_used = _dispatch_plan(meta.reshape(n, LANES), counts, p_rows)
        xs = _disp_call(h2.reshape(n, d), d0, d1, jnp.zeros((p_rows, d), F32))
        ys = _exp_call(blk_e, n_used, xs, moe_w_gate[l].astype(BF16), moe_w_up[l].astype(BF16),
                       moe_w_down[l].astype(BF16))
        x = _comb_call(x1, meta, mod, ys, d0, d1)
    return x
```

```python
import functools

import jax
import jax.numpy as jnp
from jax import lax
from jax.experimental import pallas as pl
from jax.experimental.pallas import tpu as pltpu

F32 = jnp.float32
BF16 = jnp.bfloat16

D_MODEL = 1024
CHUNK = 64
POOL_WIDTH = 512
POOL_WINDOWS = (2, 4, 8, 16)
POOL_GW = 128
POOL_HALO = 16
ATT_WIDTH = 512
N_HEADS = 8
HEAD_DIM = 64
LEFT = 512
MAX_REL = 128
IN_WIDTH = 2048
N_GROUPS = 4
EXPERTS_PER_GROUP = 8
N_EXPERTS = 32
D_EXPERT = 512
EPS = 1e-6
NEG_INF = -1e30

LANES = 128
SEQ_TILE = 512
ATT_TILE = 512
ATT_GROUP = 2 * CHUNK
ATT_BAND = LEFT + ATT_GROUP
EXPERT_ROWS = 512
ROW_TILE = 1024
ROW_UNROLL = 8
VMEM_LIMIT = 56 * 1024 * 1024

META_E0, META_E1, META_W0, META_W1, META_R0, META_R1 = 0, 1, 2, 3, 4, 5
ROUTER_GROUP_LANE = 0
ROUTER_EXPERT_LANE = N_GROUPS


def _params(*sem):
    return pltpu.CompilerParams(dimension_semantics=sem, vmem_limit_bytes=VMEM_LIMIT)


def _ada_kernel(c_ref, w_ref, b_ref, o_ref):
    ca = jax.nn.silu(c_ref[...])
    o_ref[0] = jnp.dot(ca, w_ref[0], precision=lax.Precision.HIGHEST,
                       preferred_element_type=F32) + b_ref[0]


def _ada_call(c_pad, ada_w, ada_b):
    depth = ada_w.shape[0]
    tn = 1024
    return pl.pallas_call(
        _ada_kernel,
        grid=(depth, 6 * D_MODEL // tn),
        in_specs=[
            pl.BlockSpec((8, D_MODEL), lambda l, n: (0, 0)),
            pl.BlockSpec((1, D_MODEL, tn), lambda l, n: (l, 0, n)),
            pl.BlockSpec((1, 1, tn), lambda l, n: (l, 0, n)),
        ],
        out_specs=pl.BlockSpec((1, 8, tn), lambda l, n: (l, 0, n)),
        out_shape=jax.ShapeDtypeStruct((depth, 8, 6 * D_MODEL), F32),
        compiler_params=_params("arbitrary", "arbitrary"),
        name="ada_mod",
    )(c_pad, ada_w, ada_b.reshape(depth, 1, 6 * D_MODEL))


def _in_kernel(x_ref, *refs):
    _in_body(x_ref[0], *refs)


def _in_comb_kernel(d0c_ref, d1c_ref, d0n_ref, d1n_ref, x1_ref, meta_ref, modp_ref, ys_ref,
                    mod_ref, g_ref, w_ref, pw_ref, ps_ref, qg_ref, kg_ref, bd_ref,
                    xn_ref, pool_ref, q_ref, k_ref, v_ref, ext_ref, ybuf_ref, sems):
    x = _moe_residual(d0c_ref, d1c_ref, d0n_ref, d1n_ref, x1_ref, meta_ref, modp_ref, ys_ref, ybuf_ref, sems)
    xn_ref[0] = x
    _in_body(x, mod_ref, g_ref, w_ref, pw_ref, ps_ref, qg_ref, kg_ref, bd_ref,
             pool_ref, q_ref, k_ref, v_ref, ext_ref)


def _in_body(x, mod_ref, g_ref, w_ref, pw_ref, ps_ref, qg_ref, kg_ref, bd_ref,
             pool_ref, q_ref, k_ref, v_ref, ext_ref):
    j = pl.program_id(1)
    t = x.shape[0]
    ms = jnp.mean(x * x, axis=-1, keepdims=True)
    y = x * lax.rsqrt(ms + EPS) * g_ref[...]
    h = y * (1.0 + mod_ref[0, 1:2, :]) + mod_ref[0, 0:1, :]
    z = jnp.dot(h.astype(BF16), w_ref[...], preferred_element_type=F32)

    bd = bd_ref[...]
    q = z[:, POOL_WIDTH:POOL_WIDTH + ATT_WIDTH]
    k = z[:, POOL_WIDTH + ATT_WIDTH:POOL_WIDTH + 2 * ATT_WIDTH]
    qss = jnp.dot((q * q).astype(BF16), bd, preferred_element_type=F32)
    kss = jnp.dot((k * k).astype(BF16), bd, preferred_element_type=F32)
    q_ref[0] = (q * lax.rsqrt(qss * (1.0 / HEAD_DIM) + EPS) * qg_ref[...]).astype(BF16)
    k_ref[0] = (k * lax.rsqrt(kss * (1.0 / HEAD_DIM) + EPS) * kg_ref[...]).astype(BF16)
    v_ref[0] = z[:, POOL_WIDTH + 2 * ATT_WIDTH:].astype(BF16)

    @pl.when(j == 0)
    def _():
        ext_ref[0:POOL_HALO, :] = jnp.zeros((POOL_HALO, POOL_WIDTH), F32)

    ext_ref[POOL_HALO:, :] = z[:, :POOL_WIDTH]
    pos = j * t + lax.broadcasted_iota(jnp.int32, (t, 1), 0)
    outs = []
    for gi, win in enumerate(POOL_WINDOWS):
        a = ext_ref[:, gi * POOL_GW:(gi + 1) * POOL_GW]
        s = a
        shift = 1
        while shift < win:
            s = s + pltpu.roll(s, shift, 0)
            shift *= 2
        cnt = jnp.minimum(pos + 1, win).astype(F32)
        pooled = s[POOL_HALO:, :] / cnt - a[POOL_HALO:, :]
        outs.append(jnp.dot(pooled.astype(BF16), pw_ref[gi], preferred_element_type=F32))
    pool_ref[0] = (jnp.concatenate(outs, axis=1) * ps_ref[...]).astype(BF16)
    ext_ref[0:POOL_HALO, :] = ext_ref[t:t + POOL_HALO, :]


def _in_call(x, mod, g, w_in, pool_w, pool_scale, qg, kg, bd, moe=None):
    b, s, d = x.shape
    t = SEQ_TILE
    steps = s // t
    tok = lambda w: pl.BlockSpec((1, t, w), lambda bi, j: (bi, j, 0))
    const2 = lambda shape: pl.BlockSpec(shape, lambda bi, j: (0, 0))
    mod_spec = pl.BlockSpec((1, 6, d), lambda bi, j: (bi, 0, 0))
    out = jax.ShapeDtypeStruct((b, s, ATT_WIDTH), BF16)
    in_specs = [
        mod_spec,
        const2((1, d)),
        const2((d, IN_WIDTH)),
        pl.BlockSpec((len(POOL_WINDOWS), POOL_GW, POOL_GW), lambda bi, j: (0, 0, 0)),
        const2((1, POOL_WIDTH)),
        const2((1, ATT_WIDTH)),
        const2((1, ATT_WIDTH)),
        const2((ATT_WIDTH, ATT_WIDTH)),
    ]
    args = (mod, g, w_in, pool_w, pool_scale, qg, kg, bd)
    out_specs = [tok(POOL_WIDTH), tok(ATT_WIDTH), tok(ATT_WIDTH), tok(ATT_WIDTH)]
    out_shape = [jax.ShapeDtypeStruct((b, s, POOL_WIDTH), BF16), out, out, out]
    scratch = [pltpu.VMEM((t + POOL_HALO, POOL_WIDTH), F32)]
    if moe is None:
        return pl.pallas_call(
            _in_kernel, grid=(b, steps), in_specs=[tok(d)] + in_specs, out_specs=out_specs, out_shape=out_shape,
            scratch_shapes=scratch, compiler_params=_params("arbitrary", "arbitrary"), name="in_proj",
        )(x, *args)
    meta, mod_prev, ys, d0, d1 = moe
    idx_specs, idx_args = _row_index_specs(d0, d1, b, steps, t)
    return pl.pallas_call(
        _in_comb_kernel, grid=(b, steps),
        in_specs=idx_specs + [tok(d), tok(LANES), mod_spec, pl.BlockSpec(memory_space=pl.ANY)] + in_specs,
        out_specs=[tok(d)] + out_specs,
        out_shape=[jax.ShapeDtypeStruct((b, s, d), F32)] + out_shape,
        scratch_shapes=scratch + [pltpu.VMEM((2, 2, t, d), F32), pltpu.SemaphoreType.DMA((2, 2))],
        compiler_params=_params("arbitrary", "arbitrary"), name="combine_in_proj",
    )(*idx_args, x, meta, mod_prev, ys, *args)


def _att_kernel(q_ref, kp_ref, kc_ref, vp_ref, vc_ref, bias_ref, o_ref):
    i = pl.program_id(2)
    tq = q_ref.shape[1]
    lane = lax.broadcasted_iota(jnp.int32, (1, LANES), 1)
    low_half = lane < HEAD_DIM
    col = lax.broadcasted_iota(jnp.int32, (1, ATT_BAND), 1)
    for g in range(tq // ATT_GROUP):
        r0 = g * ATT_GROUP
        qg = q_ref[0, r0:r0 + ATT_GROUP, :]
        if r0 < LEFT:
            kb = jnp.concatenate([kp_ref[0, r0:LEFT, :], kc_ref[0, 0:r0 + ATT_GROUP, :]], axis=0)
            vb = jnp.concatenate([vp_ref[0, r0:LEFT, :], vc_ref[0, 0:r0 + ATT_GROUP, :]], axis=0)
        else:
            kb = kc_ref[0, r0 - LEFT:r0 + ATT_GROUP, :]
            vb = vc_ref[0, r0 - LEFT:r0 + ATT_GROUP, :]
        valid = col >= LEFT - r0 - i * tq
        halves = []
        for hh in range(2):
            qm = jnp.where(low_half if hh == 0 else jnp.logical_not(low_half), qg, jnp.zeros_like(qg))
            s = lax.dot_general(qm, kb, (((1,), (1,)), ((), ())), preferred_element_type=F32)
            s = jnp.where(valid, s + bias_ref[hh], NEG_INF)
            m = jnp.max(s, axis=-1, keepdims=True)
            e = jnp.exp(s - m)
            l = jnp.sum(e, axis=-1, keepdims=True)
            pv = jnp.dot(e.astype(BF16), vb, preferred_element_type=F32)
            halves.append(pv / l)
        o_ref[0, r0:r0 + ATT_GROUP, :] = jnp.where(low_half, halves[0], halves[1]).astype(BF16)


def _att_call(q, k, v, bias):
    b, s, _ = q.shape
    tq = ATT_TILE
    lb = tq // LEFT
    cur = pl.BlockSpec((1, tq, LANES), lambda bi, hp, i: (bi, i, hp))
    prev = pl.BlockSpec((1, LEFT, LANES), lambda bi, hp, i: (bi, jnp.maximum(i * lb - 1, 0), hp))
    return pl.pallas_call(
        _att_kernel,
        grid=(b, N_HEADS // 2, s // tq),
        in_specs=[cur, prev, cur, prev, cur,
                  pl.BlockSpec((2, ATT_GROUP, ATT_BAND), lambda bi, hp, i: (hp, 0, 0))],
        out_specs=cur,
        out_shape=jax.ShapeDtypeStruct((b, s, ATT_WIDTH), BF16),
        compiler_params=_params("arbitrary", "arbitrary", "arbitrary"),
        name="chunk_attn",
    )(q, k, k, v, v, bias)


def _group_bias(rel_bias):
    qi = jnp.arange(ATT_GROUP)[:, None]
    kj = jnp.arange(ATT_BAND)[None, :]
    first = (qi // CHUNK) * CHUNK
    visible = (kj >= first) & (kj < first + LEFT + CHUNK)
    lo = ATT_GROUP - 1
    assert lo <= MAX_REL
    tab = rel_bias.astype(F32)
    n_clipped = ATT_BAND - 1 - MAX_REL
    by_rel = jnp.concatenate([tab[:, MAX_REL - lo:], jnp.repeat(tab[:, -1:], n_clipped, axis=1)], axis=1)
    rev = by_rel[:, ::-1]
    bias = jnp.stack([rev[:, lo - q:lo - q + ATT_BAND] for q in range(ATT_GROUP)], axis=1)
    return jnp.where(visible[None], bias, NEG_INF)


def _out_kernel(x_ref, pool_ref, att_ref, mod_ref, g_ref, wo_ref, wr_ref, br_ref,
                x1_ref, h2_ref, meta_ref, cnt_ref, carry_ref):
    first = jnp.logical_and(pl.program_id(0) == 0, pl.program_id(1) == 0)
    t = x_ref.shape[1]

    @pl.when(first)
    def _():
        carry_ref[...] = jnp.zeros_like(carry_ref)

    mix = (jnp.dot(pool_ref[0], wo_ref[:POOL_WIDTH, :], preferred_element_type=F32)
           + jnp.dot(att_ref[0], wo_ref[POOL_WIDTH:, :], preferred_element_type=F32))
    x1 = x_ref[0] + mod_ref[0, 2:3, :] * mix
    x1_ref[0] = x1
    ms = jnp.mean(x1 * x1, axis=-1, keepdims=True)
    y = x1 * lax.rsqrt(ms + EPS) * g_ref[...]
    h2 = y * (1.0 + mod_ref[0, 4:5, :]) + mod_ref[0, 3:4, :]

    h2_ref[0] = h2
    hb = h2.astype(BF16)

    h_lo = (h2 - hb.astype(F32)).astype(BF16)
    r = (jnp.dot(hb, wr_ref[...], preferred_element_type=F32)
         + jnp.dot(h_lo, wr_ref[...], preferred_element_type=F32))
    logits = r[:, :LANES] + r[:, LANES:] + br_ref[...]

    lane = lax.broadcasted_iota(jnp.int32, (t, LANES), 1).astype(F32)
    ninf = jnp.float32(-jnp.inf)
    gmask = lane < N_GROUPS
    gl = jnp.where(gmask, logits, ninf)
    gmax = jnp.max(gl, axis=-1, keepdims=True)
    gsum = jnp.sum(jnp.where(gmask, jnp.exp(logits - gmax), 0.0), axis=-1, keepdims=True)
    g_p = 1.0 / gsum
    g_idx = jnp.min(jnp.where(gl == gmax, lane, float(LANES)), axis=-1, keepdims=True)
    e_lo = ROUTER_EXPERT_LANE + EXPERTS_PER_GROUP * g_idx
    el = jnp.where((lane >= e_lo) & (lane < e_lo + EXPERTS_PER_GROUP), logits, ninf)
    m1 = jnp.max(el, axis=-1, keepdims=True)
    i1 = jnp.min(jnp.where(el == m1, lane, float(LANES)), axis=-1, keepdims=True)
    el2 = jnp.where(lane == i1, ninf, el)
    m2 = jnp.max(el2, axis=-1, keepdims=True)
    i2 = jnp.min(jnp.where(el2 == m2, lane, float(LANES)), axis=-1, keepdims=True)
    e2 = jnp.exp(m2 - m1)
    w1 = g_p / (1.0 + e2)
    w2 = g_p * e2 / (1.0 + e2)

    hot = ((lane == i1) | (lane == i2))
    hot_f = hot.astype(F32)
    row = lax.broadcasted_iota(jnp.int32, (t, t), 0)
    colt = lax.broadcasted_iota(jnp.int32, (t, t), 1)
    lower = (colt < row).astype(BF16)
    before = jnp.dot(lower, hot.astype(BF16), preferred_element_type=F32) + carry_ref[0:1, :]
    r1 = jnp.sum(jnp.where(lane == i1, before, 0.0), axis=-1, keepdims=True)
    r2 = jnp.sum(jnp.where(lane == i2, before, 0.0), axis=-1, keepdims=True)
    carry_ref[0:1, :] = carry_ref[0:1, :] + jnp.sum(hot_f, axis=0, keepdims=True)
    cnt_ref[...] = jnp.broadcast_to(carry_ref[0:1, :], cnt_ref.shape)

    meta = jnp.where(lane == META_E0, i1 - ROUTER_EXPERT_LANE, 0.0)
    meta = jnp.where(lane == META_E1, i2 - ROUTER_EXPERT_LANE, meta)
    meta = jnp.where(lane == META_W0, w1, meta)
    meta = jnp.where(lane == META_W1, w2, meta)
    meta = jnp.where(lane == META_R0, r1, meta)
    meta = jnp.where(lane == META_R1, r2, meta)
    meta_ref[0] = meta


def _out_call(x, pool, att, mod, g, w_out, wr, br):
    b, s, d = x.shape
    t = SEQ_TILE
    tok = lambda w: pl.BlockSpec((1, t, w), lambda bi, j: (bi, j, 0))
    const2 = lambda shape: pl.BlockSpec(shape, lambda bi, j: (0, 0))
    return pl.pallas_call(
        _out_kernel,
        grid=(b, s // t),
        in_specs=[tok(d), tok(POOL_WIDTH), tok(ATT_WIDTH),
                  pl.BlockSpec((1, 6, d), lambda bi, j: (bi, 0, 0)),
                  const2((1, d)), const2((d, d)), const2((d, 2 * LANES)), const2((1, LANES))],
        out_specs=[tok(d), tok(d), tok(LANES), const2((8, LANES))],
        out_shape=[jax.ShapeDtypeStruct((b, s, d), F32),
                   jax.ShapeDtypeStruct((b, s, d), F32),
                   jax.ShapeDtypeStruct((b, s, LANES), F32),
                   jax.ShapeDtypeStruct((8, LANES), F32)],
        scratch_shapes=[pltpu.VMEM((8, LANES), F32)],
        compiler_params=_params("arbitrary", "arbitrary"),
        name="out_proj_router",
    )(x, pool, att, mod, g, w_out, wr, br)


def _row_copies(src_ref, dst_ref, idx_ref, sem, rows, gather):
    def issue(blk, carry):
        for u in range(ROW_UNROLL):
            r = blk * ROW_UNROLL + u
            tgt = idx_ref[0, 0, r]
            if gather:
                cp = pltpu.make_async_copy(src_ref.at[pl.ds(tgt, 1)], dst_ref.at[pl.ds(r, 1)], sem)
            else:
                cp = pltpu.make_async_copy(src_ref.at[pl.ds(r, 1)], dst_ref.at[pl.ds(tgt, 1)], sem)
            cp.start(priority=u % 2)
        return carry
    lax.fori_loop(0, rows // ROW_UNROLL, issue, 0)


def _wait_rows(vmem_ref, hbm_ref, sem, rows, gather):
    hbm_rows = hbm_ref.at[pl.ds(0, rows)]
    if gather:
        pltpu.make_async_copy(hbm_rows, vmem_ref, sem).wait()
    else:
        pltpu.make_async_copy(vmem_ref, hbm_rows, sem).wait()


def _disp_kernel(d0_ref, d1_ref, h_ref, xs_in_ref, xs_ref, sems):
    del xs_in_ref
    rows = h_ref.shape[0]
    _row_copies(h_ref, xs_ref, d0_ref, sems.at[0], rows, gather=False)
    _row_copies(h_ref, xs_ref, d1_ref, sems.at[1], rows, gather=False)
    _wait_rows(h_ref, xs_ref, sems.at[0], rows, gather=False)
    _wait_rows(h_ref, xs_ref, sems.at[1], rows, gather=False)


def _disp_call(h2, d0, d1, xs_init):
    n, w = h2.shape
    t = ROW_TILE
    idx = pl.BlockSpec((1, 1, t), lambda i: (i, 0, 0), memory_space=pltpu.SMEM)
    return pl.pallas_call(
        _disp_kernel,
        grid=(n // t,),
        in_specs=[idx, idx, pl.BlockSpec((t, w), lambda i: (i, 0)),
                  pl.BlockSpec(memory_space=pl.ANY)],
        out_specs=pl.BlockSpec(memory_space=pl.ANY),
        out_shape=jax.ShapeDtypeStruct(xs_init.shape, xs_init.dtype),
        scratch_shapes=[pltpu.SemaphoreType.DMA((2,))],
        input_output_aliases={3: 0},
        compiler_params=_params("arbitrary"),
        name="dispatch_rows",
    )(d0.reshape(n // t, 1, t), d1.reshape(n // t, 1, t), h2, xs_init)


def _exp_kernel(be_ref, nu_ref, xs_ref, wg_ref, wu_ref, wd_ref, y_ref):
    del be_ref
    i = pl.program_id(0)

    @pl.when(i < nu_ref[0])
    def _():
        xb = xs_ref[...].astype(BF16)
        gate = jnp.dot(xb, wg_ref[0], preferred_element_type=F32)
        up = jnp.dot(xb, wu_ref[0], preferred_element_type=F32)
        mid = (jax.nn.silu(gate) * up).astype(BF16)
        y_ref[...] = jnp.dot(mid, wd_ref[0], preferred_element_type=F32)

    @pl.when(i >= nu_ref[0])
    def _():
        y_ref[...] = jnp.zeros_like(y_ref)


def _exp_call(blk_e, n_used, xs, wg, wu, wd):
    p, w = xs.shape
    r = EXPERT_ROWS
    wspec = lambda shape: pl.BlockSpec((1,) + shape, lambda i, be, nu: (be[i], 0, 0))
    grid_spec = pltpu.PrefetchScalarGridSpec(
        num_scalar_prefetch=2,
        grid=(p // r,),
        in_specs=[pl.BlockSpec((r, w), lambda i, be, nu: (i, 0)),
                  wspec((D_MODEL, D_EXPERT)), wspec((D_MODEL, D_EXPERT)), wspec((D_EXPERT, D_MODEL))],
        out_specs=pl.BlockSpec((r, D_MODEL), lambda i, be, nu: (i, 0)),
    )
    return pl.pallas_call(
        _exp_kernel,
        grid_spec=grid_spec,
        out_shape=jax.ShapeDtypeStruct((p, D_MODEL), F32),
        compiler_params=_params("arbitrary"),
        name="experts",
    )(blk_e, n_used, xs, wg, wu, wd)


def _moe_residual(d0c_ref, d1c_ref, d0n_ref, d1n_ref, x_ref, meta_ref, mod_ref, ys_ref, ybuf_ref, sems):
    rows = x_ref.shape[1]
    n = pl.program_id(0) * pl.num_programs(1) + pl.program_id(1)
    total = pl.num_programs(0) * pl.num_programs(1)

    def start(i0_ref, i1_ref, slot):
        _row_copies(ys_ref, ybuf_ref.at[slot, 0], i0_ref, sems.at[slot, 0], rows, gather=True)
        _row_copies(ys_ref, ybuf_ref.at[slot, 1], i1_ref, sems.at[slot, 1], rows, gather=True)

    @pl.when(n == 0)
    def _():
        start(d0c_ref, d1c_ref, 0)

    @pl.when(n + 1 < total)
    def _():
        start(d0n_ref, d1n_ref, (n + 1) % 2)

    slot = n % 2
    _wait_rows(ybuf_ref.at[slot, 0], ys_ref, sems.at[slot, 0], rows, gather=True)
    _wait_rows(ybuf_ref.at[slot, 1], ys_ref, sems.at[slot, 1], rows, gather=True)
    w0 = meta_ref[0, :, META_W0:META_W0 + 1]
    w1 = meta_ref[0, :, META_W1:META_W1 + 1]
    moe = ybuf_ref[slot, 0] * w0 + ybuf_ref[slot, 1] * w1
    return x_ref[0] + mod_ref[0, 5:6, :] * moe


def _row_index_specs(d0, d1, b, steps, t):
    last = b * steps - 1
    cur = pl.BlockSpec((1, 1, t), lambda bi, j: (bi * steps + j, 0, 0), memory_space=pltpu.SMEM)
    nxt = pl.BlockSpec((1, 1, t), lambda bi, j: (jnp.minimum(bi * steps + j + 1, last), 0, 0),
                       memory_space=pltpu.SMEM)
    d0 = d0.reshape(b * steps, 1, t)
    d1 = d1.reshape(b * steps, 1, t)
    return [cur, cur, nxt, nxt], (d0, d1, d0, d1)


def _comb_kernel(d0c_ref, d1c_ref, d0n_ref, d1n_ref, x_ref, meta_ref, mod_ref, ys_ref, o_ref, ybuf_ref, sems):
    o_ref[0] = _moe_residual(d0c_ref, d1c_ref, d0n_ref, d1n_ref, x_ref, meta_ref, mod_ref, ys_ref,
                             ybuf_ref, sems)


def _comb_call(x1, meta, mod, ys, d0, d1):
    b, s, d = x1.shape
    t = SEQ_TILE
    steps = s // t
    idx_specs, idx_args = _row_index_specs(d0, d1, b, steps, t)
    tok = lambda w: pl.BlockSpec((1, t, w), lambda bi, j: (bi, j, 0))
    return pl.pallas_call(
        _comb_kernel,
        grid=(b, steps),
        in_specs=idx_specs + [tok(d), tok(LANES),
                              pl.BlockSpec((1, 6, d), lambda bi, j: (bi, 0, 0)),
                              pl.BlockSpec(memory_space=pl.ANY)],
        out_specs=tok(d),
        out_shape=jax.ShapeDtypeStruct((b, s, d), F32),
        scratch_shapes=[pltpu.VMEM((2, 2, t, d), F32), pltpu.SemaphoreType.DMA((2, 2))],
        compiler_params=_params("arbitrary", "arbitrary"),
        name="combine_rows",
    )(*idx_args, x1, meta, mod, ys)


def _dispatch_plan(meta, counts, n_rows):
    r = EXPERT_ROWS
    e0 = meta[:, META_E0].astype(jnp.int32)
    e1 = meta[:, META_E1].astype(jnp.int32)
    cnt = counts[0, ROUTER_EXPERT_LANE:ROUTER_EXPERT_LANE + N_EXPERTS].astype(jnp.int32)
    padded = ((cnt + r - 1) // r) * r
    pad_end = jnp.cumsum(padded)
    pad_start = pad_end - padded
    experts = jnp.arange(N_EXPERTS, dtype=jnp.int32)[None, :]
    start_of = lambda e: jnp.sum(jnp.where(e[:, None] == experts, pad_start[None, :], 0), axis=1)
    d0 = start_of(e0) + meta[:, META_R0].astype(jnp.int32)
    d1 = start_of(e1) + meta[:, META_R1].astype(jnp.int32)
    nblk = n_rows // r
    blk_row = jnp.arange(nblk, dtype=jnp.int32) * r
    blk_e = jnp.minimum(jnp.sum((pad_end[None, :] <= blk_row[:, None]).astype(jnp.int32), axis=1),
                        N_EXPERTS - 1)
    n_used = (pad_end[-1:] // r).astype(jnp.int32)
    return d0, d1, blk_e, n_used


def kernel(x, c, ada_w, ada_b, norm1_g, norm2_g, w_in, pool_w, pool_scale, q_norm_g, k_norm_g, rel_bias,
           w_out, router_group_w, router_group_b, router_expert_w, router_expert_b,
           moe_w_gate, moe_w_up, moe_w_down):
    b, s, d = x.shape
    depth = ada_w.shape[0]
    n = b * s
    assert d == D_MODEL and b <= 8 and s % SEQ_TILE == 0 and s % ATT_TILE == 0 and n % ROW_TILE == 0
    p_rows = n * 2 + N_EXPERTS * EXPERT_ROWS

    c_pad = jnp.zeros((8, d), F32).at[:b].set(c)
    mod_all = _ada_call(c_pad, ada_w, ada_b)
    bias = _group_bias(rel_bias)
    head_block = jnp.arange(ATT_WIDTH) // HEAD_DIM
    bd = (head_block[:, None] == head_block[None, :]).astype(BF16)
    sm_scale = HEAD_DIM ** -0.5

    moe = None
    for l in range(depth):
        mod = mod_all[l, :b].reshape(b, 6, d)
        qg = (jnp.tile(q_norm_g[l], N_HEADS) * sm_scale).reshape(1, ATT_WIDTH)
        kg = jnp.tile(k_norm_g[l], N_HEADS).reshape(1, ATT_WIDTH)
        outs = _in_call(x, mod, norm1_g[l].reshape(1, d), w_in[l].astype(BF16), pool_w[l].astype(BF16),
                        pool_scale[l].reshape(1, POOL_WIDTH), qg, kg, bd, moe)
        if moe is not None:
            x, outs = outs[0], outs[1:]
        pool, q, k, v = outs
        att = _att_call(q, k, v, bias)

        wr = jnp.zeros((d, LANES), F32)
        wr = wr.at[:, ROUTER_GROUP_LANE:ROUTER_GROUP_LANE + N_GROUPS].set(router_group_w[l])
        wr = wr.at[:, ROUTER_EXPERT_LANE:ROUTER_EXPERT_LANE + N_EXPERTS].set(router_expert_w[l])
        wr_hi = wr.astype(BF16)
        wr_lo = (wr - wr_hi.astype(F32)).astype(BF16)
        br = jnp.zeros((1, LANES), F32)
        br = br.at[0, ROUTER_GROUP_LANE:ROUTER_GROUP_LANE + N_GROUPS].set(router_group_b[l])
        br = br.at[0, ROUTER_EXPERT_LANE:ROUTER_EXPERT_LANE + N_EXPERTS].set(router_expert_b[l])
        x1, h2, meta, counts = _out_call(x, pool, att, mod, norm2_g[l].reshape(1, d),
                                         w_out[l].astype(BF16), jnp.concatenate([wr_hi, wr_lo], axis=1), br)

        d0, d1, blk_e, n_used = _dispatch_plan(meta.reshape(n, LANES), counts, p_rows)
        xs = _disp_call(h2.reshape(n, d), d0, d1, jnp.zeros((p_rows, d), F32))
        ys = _exp_call(blk_e, n_used, xs, moe_w_gate[l].astype(BF16), moe_w_up[l].astype(BF16),
                       moe_w_down[l].astype(BF16))
        x = x1
        moe = (meta, mod, ys, d0, d1)
    return _comb_call(x, *moe)
```

```python
import jax
import jax.numpy as jnp
from jax import lax
from jax.experimental import pallas as pl
from jax.experimental.pallas import tpu as pltpu

F32 = jnp.float32
BF16 = jnp.bfloat16

D_MODEL = 1024
CHUNK = 64
POOL_WIDTH = 512
POOL_WINDOWS = (2, 4, 8, 16)
POOL_GW = 128
POOL_HALO = 16
ATT_WIDTH = 512
N_HEADS = 8
HEAD_DIM = 64
LEFT = 512
MAX_REL = 128
IN_WIDTH = 2048
N_GROUPS = 4
EXPERTS_PER_GROUP = 8
N_EXPERTS = 32
D_EXPERT = 512
EPS = 1e-6
NEG_INF = -1e30

LANES = 128
TOKEN_SLABS = D_MODEL // LANES
SEQ_TILE = 512
ATT_TILE = 512
ATT_GROUP = 2 * CHUNK
ATT_BAND = LEFT + ATT_GROUP
EXPERT_ROWS = 512
ROW_TILE = 1024
ROW_UNROLL = 8
VMEM_LIMIT = 56 * 1024 * 1024

META_E0, META_E1, META_W0, META_W1, META_R0, META_R1 = 0, 1, 2, 3, 4, 5
ROUTER_GROUP_LANE = 0
ROUTER_EXPERT_LANE = N_GROUPS


def _params(*sem):
    return pltpu.CompilerParams(dimension_semantics=sem, vmem_limit_bytes=VMEM_LIMIT)


def _first_step():
    return jnp.logical_and(pl.program_id(0) == 0, pl.program_id(1) == 0)


def _load_token_tiles(ref, rows):
    return jnp.concatenate([ref[pl.ds(s, rows, stride=TOKEN_SLABS), :] for s in range(TOKEN_SLABS)], axis=1)


def _store_token_tiles(ref, val):
    rows = val.shape[0]
    for s in range(TOKEN_SLABS):
        ref[pl.ds(s, rows, stride=TOKEN_SLABS), :] = val[:, s * LANES:(s + 1) * LANES]


def _ada_kernel(c_ref, w_ref, b_ref, o_ref):
    ca = jax.nn.silu(c_ref[...])
    o_ref[0] = jnp.dot(ca, w_ref[0], precision=lax.Precision.HIGHEST,
                       preferred_element_type=F32) + b_ref[0]


def _ada_call(c_pad, ada_w, ada_b):
    depth = ada_w.shape[0]
    tn = 1024
    return pl.pallas_call(
        _ada_kernel,
        grid=(depth, 6 * D_MODEL // tn),
        in_specs=[
            pl.BlockSpec((8, D_MODEL), lambda l, n: (0, 0)),
            pl.BlockSpec((1, D_MODEL, tn), lambda l, n: (l, 0, n)),
            pl.BlockSpec((1, 1, tn), lambda l, n: (l, 0, n)),
        ],
        out_specs=pl.BlockSpec((1, 8, tn), lambda l, n: (l, 0, n)),
        out_shape=jax.ShapeDtypeStruct((depth, 8, 6 * D_MODEL), F32),
        compiler_params=_params("arbitrary", "arbitrary"),
        name="ada_mod",
    )(c_pad, ada_w, ada_b.reshape(depth, 1, 6 * D_MODEL))


def _in_kernel(x_ref, *refs):
    _in_body(x_ref[0], *refs)


def _in_comb_kernel(d0c_ref, d1c_ref, d0n_ref, d1n_ref, x1_ref, meta_ref, modp_ref, ys_ref,
                    mod_ref, g_ref, w_ref, pw_ref, ps_ref, qg_ref, kg_ref, bd_ref,
                    xn_ref, pool_ref, q_ref, k_ref, v_ref, ext_ref, wb_ref, ybuf_ref, sems):
    x = _moe_residual(d0c_ref, d1c_ref, d0n_ref, d1n_ref, x1_ref, meta_ref, modp_ref, ys_ref, ybuf_ref, sems)
    xn_ref[0] = x
    _in_body(x, mod_ref, g_ref, w_ref, pw_ref, ps_ref, qg_ref, kg_ref, bd_ref,
             pool_ref, q_ref, k_ref, v_ref, ext_ref, wb_ref)


def _in_body(x, mod_ref, g_ref, w_ref, pw_ref, ps_ref, qg_ref, kg_ref, bd_ref,
             pool_ref, q_ref, k_ref, v_ref, ext_ref, wb_ref):
    j = pl.program_id(1)
    t = x.shape[0]

    @pl.when(_first_step())
    def _():
        wb_ref[...] = w_ref[0].astype(BF16)

    ms = jnp.mean(x * x, axis=-1, keepdims=True)
    y = x * lax.rsqrt(ms + EPS) * g_ref[...]
    h = y * (1.0 + mod_ref[0, 1:2, :]) + mod_ref[0, 0:1, :]
    z = jnp.dot(h.astype(BF16), wb_ref[...], preferred_element_type=F32)

    bd = bd_ref[...]
    q = z[:, POOL_WIDTH:POOL_WIDTH + ATT_WIDTH]
    k = z[:, POOL_WIDTH + ATT_WIDTH:POOL_WIDTH + 2 * ATT_WIDTH]
    qss = jnp.dot((q * q).astype(BF16), bd, preferred_element_type=F32)
    kss = jnp.dot((k * k).astype(BF16), bd, preferred_element_type=F32)
    q_ref[0] = (q * lax.rsqrt(qss * (1.0 / HEAD_DIM) + EPS) * qg_ref[...]).astype(BF16)
    k_ref[0] = (k * lax.rsqrt(kss * (1.0 / HEAD_DIM) + EPS) * kg_ref[...]).astype(BF16)
    v_ref[0] = z[:, POOL_WIDTH + 2 * ATT_WIDTH:].astype(BF16)

    @pl.when(j == 0)
    def _():
        ext_ref[0:POOL_HALO, :] = jnp.zeros((POOL_HALO, POOL_WIDTH), F32)

    ext_ref[POOL_HALO:, :] = z[:, :POOL_WIDTH]
    pos = j * t + lax.broadcasted_iota(jnp.int32, (t, 1), 0)
    outs = []
    for gi, win in enumerate(POOL_WINDOWS):
        a = ext_ref[:, gi * POOL_GW:(gi + 1) * POOL_GW]
        s = a
        shift = 1
        while shift < win:
            s = s + pltpu.roll(s, shift, 0)
            shift *= 2
        cnt = jnp.minimum(pos + 1, win).astype(F32)
        pooled = s[POOL_HALO:, :] / cnt - a[POOL_HALO:, :]
        outs.append(jnp.dot(pooled.astype(BF16), pw_ref[0, gi].astype(BF16), preferred_element_type=F32))
    pool_ref[0] = (jnp.concatenate(outs, axis=1) * ps_ref[...]).astype(BF16)
    ext_ref[0:POOL_HALO, :] = ext_ref[t:t + POOL_HALO, :]


def _in_call(x, mod, g, w_in, pool_w, layer, pool_scale, qg, kg, bd, moe=None):
    b, s, d = x.shape
    t = SEQ_TILE
    steps = s // t
    tok = lambda w: pl.BlockSpec((1, t, w), lambda bi, j: (bi, j, 0))
    const2 = lambda shape: pl.BlockSpec(shape, lambda bi, j: (0, 0))
    mod_spec = pl.BlockSpec((1, 6, d), lambda bi, j: (bi, 0, 0))
    out = jax.ShapeDtypeStruct((b, s, ATT_WIDTH), BF16)
    in_specs = [
        mod_spec,
        const2((1, d)),
        pl.BlockSpec((1, d, IN_WIDTH), lambda bi, j: (layer, 0, 0)),
        pl.BlockSpec((1, len(POOL_WINDOWS), POOL_GW, POOL_GW), lambda bi, j: (layer, 0, 0, 0)),
        const2((1, POOL_WIDTH)),
        const2((1, ATT_WIDTH)),
        const2((1, ATT_WIDTH)),
        const2((ATT_WIDTH, ATT_WIDTH)),
    ]
    args = (mod, g, w_in, pool_w, pool_scale, qg, kg, bd)
    out_specs = [tok(POOL_WIDTH), tok(ATT_WIDTH), tok(ATT_WIDTH), tok(ATT_WIDTH)]
    out_shape = [jax.ShapeDtypeStruct((b, s, POOL_WIDTH), BF16), out, out, out]
    scratch = [pltpu.VMEM((t + POOL_HALO, POOL_WIDTH), F32), pltpu.VMEM((d, IN_WIDTH), BF16)]
    if moe is None:
        return pl.pallas_call(
            _in_kernel, grid=(b, steps), in_specs=[tok(d)] + in_specs, out_specs=out_specs, out_shape=out_shape,
            scratch_shapes=scratch, compiler_params=_params("arbitrary", "arbitrary"), name="in_proj",
        )(x, *args)
    meta, mod_prev, ys, d0, d1 = moe
    idx_specs, idx_args = _row_index_specs(d0, d1, b, steps, t)
    return pl.pallas_call(
        _in_comb_kernel, grid=(b, steps),
        in_specs=idx_specs + [tok(d), tok(LANES), mod_spec, pl.BlockSpec(memory_space=pl.ANY)] + in_specs,
        out_specs=[tok(d)] + out_specs,
        out_shape=[jax.ShapeDtypeStruct((b, s, d), F32)] + out_shape,
        scratch_shapes=scratch + _gather_scratch(t),
        compiler_params=_params("arbitrary", "arbitrary"), name="combine_in_proj",
    )(*idx_args, x, meta, mod_prev, ys, *args)


def _att_kernel(q_ref, kp_ref, kc_ref, vp_ref, vc_ref, bias_ref, o_ref):
    i = pl.program_id(2)
    tq = q_ref.shape[1]
    lane = lax.broadcasted_iota(jnp.int32, (1, LANES), 1)
    low_half = lane < HEAD_DIM
    col = lax.broadcasted_iota(jnp.int32, (1, ATT_BAND), 1)
    for g in range(tq // ATT_GROUP):
        r0 = g * ATT_GROUP
        qg = q_ref[0, r0:r0 + ATT_GROUP, :]
        if r0 < LEFT:
            kb = jnp.concatenate([kp_ref[0, r0:LEFT, :], kc_ref[0, 0:r0 + ATT_GROUP, :]], axis=0)
            vb = jnp.concatenate([vp_ref[0, r0:LEFT, :], vc_ref[0, 0:r0 + ATT_GROUP, :]], axis=0)
        else:
            kb = kc_ref[0, r0 - LEFT:r0 + ATT_GROUP, :]
            vb = vc_ref[0, r0 - LEFT:r0 + ATT_GROUP, :]
        valid = col >= LEFT - r0 - i * tq
        halves = []
        for hh in range(2):
            qm = jnp.where(low_half if hh == 0 else jnp.logical_not(low_half), qg, jnp.zeros_like(qg))
            s = lax.dot_general(qm, kb, (((1,), (1,)), ((), ())), preferred_element_type=F32)
            s = jnp.where(valid, s + bias_ref[hh], NEG_INF)
            m = jnp.max(s, axis=-1, keepdims=True)
            e = jnp.exp(s - m)
            l = jnp.sum(e, axis=-1, keepdims=True)
            pv = jnp.dot(e.astype(BF16), vb, preferred_element_type=F32)
            halves.append(pv / l)
        o_ref[0, r0:r0 + ATT_GROUP, :] = jnp.where(low_half, halves[0], halves[1]).astype(BF16)


def _att_call(q, k, v, bias):
    b, s, _ = q.shape
    tq = ATT_TILE
    lb = tq // LEFT
    cur = pl.BlockSpec((1, tq, LANES), lambda bi, hp, i: (bi, i, hp))
    prev = pl.BlockSpec((1, LEFT, LANES), lambda bi, hp, i: (bi, jnp.maximum(i * lb - 1, 0), hp))
    return pl.pallas_call(
        _att_kernel,
        grid=(b, N_HEADS // 2, s // tq),
        in_specs=[cur, prev, cur, prev, cur,
                  pl.BlockSpec((2, ATT_GROUP, ATT_BAND), lambda bi, hp, i: (hp, 0, 0))],
        out_specs=cur,
        out_shape=jax.ShapeDtypeStruct((b, s, ATT_WIDTH), BF16),
        compiler_params=_params("arbitrary", "arbitrary", "arbitrary"),
        name="chunk_attn",
    )(q, k, k, v, v, bias)


def _group_bias(rel_bias):
    qi = jnp.arange(ATT_GROUP)[:, None]
    kj = jnp.arange(ATT_BAND)[None, :]
    first = (qi // CHUNK) * CHUNK
    visible = (kj >= first) & (kj < first + LEFT + CHUNK)
    lo = ATT_GROUP - 1
    assert lo <= MAX_REL
    tab = rel_bias.astype(F32)
    n_clipped = ATT_BAND - 1 - MAX_REL
    by_rel = jnp.concatenate([tab[:, MAX_REL - lo:], jnp.repeat(tab[:, -1:], n_clipped, axis=1)], axis=1)
    rev = by_rel[:, ::-1]
    bias = jnp.stack([rev[:, lo - q:lo - q + ATT_BAND] for q in range(ATT_GROUP)], axis=1)
    return jnp.where(visible[None], bias, NEG_INF)


def _out_kernel(x_ref, pool_ref, att_ref, mod_ref, g_ref, wo_ref, wr_ref, br_ref,
                x1_ref, h2_ref, meta_ref, cnt_ref, carry_ref, wob_ref):
    t = x_ref.shape[1]

    @pl.when(_first_step())
    def _():
        carry_ref[...] = jnp.zeros_like(carry_ref)
        wob_ref[...] = wo_ref[0].astype(BF16)

    mix = (jnp.dot(pool_ref[0], wob_ref[:POOL_WIDTH, :], preferred_element_type=F32)
           + jnp.dot(att_ref[0], wob_ref[POOL_WIDTH:, :], preferred_element_type=F32))
    x1 = x_ref[0] + mod_ref[0, 2:3, :] * mix
    x1_ref[0] = x1
    ms = jnp.mean(x1 * x1, axis=-1, keepdims=True)
    y = x1 * lax.rsqrt(ms + EPS) * g_ref[...]
    h2 = y * (1.0 + mod_ref[0, 4:5, :]) + mod_ref[0, 3:4, :]

    _store_token_tiles(h2_ref, h2)
    hb = h2.astype(BF16)

    h_lo = (h2 - hb.astype(F32)).astype(BF16)
    r = (jnp.dot(hb, wr_ref[...], preferred_element_type=F32)
         + jnp.dot(h_lo, wr_ref[...], preferred_element_type=F32))
    logits = r[:, :LANES] + r[:, LANES:] + br_ref[...]

    lane = lax.broadcasted_iota(jnp.int32, (t, LANES), 1).astype(F32)
    ninf = jnp.float32(-jnp.inf)
    gmask = lane < N_GROUPS
    gl = jnp.where(gmask, logits, ninf)
    gmax = jnp.max(gl, axis=-1, keepdims=True)
    gsum = jnp.sum(jnp.where(gmask, jnp.exp(logits - gmax), 0.0), axis=-1, keepdims=True)
    g_p = 1.0 / gsum
    g_idx = jnp.min(jnp.where(gl == gmax, lane, float(LANES)), axis=-1, keepdims=True)
    e_lo = ROUTER_EXPERT_LANE + EXPERTS_PER_GROUP * g_idx
    el = jnp.where((lane >= e_lo) & (lane < e_lo + EXPERTS_PER_GROUP), logits, ninf)
    m1 = jnp.max(el, axis=-1, keepdims=True)
    i1 = jnp.min(jnp.where(el == m1, lane, float(LANES)), axis=-1, keepdims=True)
    el2 = jnp.where(lane == i1, ninf, el)
    m2 = jnp.max(el2, axis=-1, keepdims=True)
    i2 = jnp.min(jnp.where(el2 == m2, lane, float(LANES)), axis=-1, keepdims=True)
    e2 = jnp.exp(m2 - m1)
    w1 = g_p / (1.0 + e2)
    w2 = g_p * e2 / (1.0 + e2)

    hot = ((lane == i1) | (lane == i2))
    hot_f = hot.astype(F32)
    row = lax.broadcasted_iota(jnp.int32, (t, t), 0)
    colt = lax.broadcasted_iota(jnp.int32, (t, t), 1)
    lower = (colt < row).astype(BF16)
    before = jnp.dot(lower, hot.astype(BF16), preferred_element_type=F32) + carry_ref[0:1, :]
    r1 = jnp.sum(jnp.where(lane == i1, before, 0.0), axis=-1, keepdims=True)
    r2 = jnp.sum(jnp.where(lane == i2, before, 0.0), axis=-1, keepdims=True)
    carry_ref[0:1, :] = carry_ref[0:1, :] + jnp.sum(hot_f, axis=0, keepdims=True)
    cnt_ref[...] = jnp.broadcast_to(carry_ref[0:1, :], cnt_ref.shape)

    meta = jnp.where(lane == META_E0, i1 - ROUTER_EXPERT_LANE, 0.0)
    meta = jnp.where(lane == META_E1, i2 - ROUTER_EXPERT_LANE, meta)
    meta = jnp.where(lane == META_W0, w1, meta)
    meta = jnp.where(lane == META_W1, w2, meta)
    meta = jnp.where(lane == META_R0, r1, meta)
    meta = jnp.where(lane == META_R1, r2, meta)
    meta_ref[0] = meta


def _out_call(x, pool, att, mod, g, w_out, layer, wr, br):
    b, s, d = x.shape
    t = SEQ_TILE
    steps = s // t
    tok = lambda w: pl.BlockSpec((1, t, w), lambda bi, j: (bi, j, 0))
    const2 = lambda shape: pl.BlockSpec(shape, lambda bi, j: (0, 0))
    return pl.pallas_call(
        _out_kernel,
        grid=(b, steps),
        in_specs=[tok(d), tok(POOL_WIDTH), tok(ATT_WIDTH),
                  pl.BlockSpec((1, 6, d), lambda bi, j: (bi, 0, 0)),
                  const2((1, d)), pl.BlockSpec((1, d, d), lambda bi, j: (layer, 0, 0)),
                  const2((d, 2 * LANES)), const2((1, LANES))],
        out_specs=[tok(d), pl.BlockSpec((t * TOKEN_SLABS, LANES), lambda bi, j: (bi * steps + j, 0)),
                   tok(LANES), const2((8, LANES))],
        out_shape=[jax.ShapeDtypeStruct((b, s, d), F32),
                   jax.ShapeDtypeStruct((b * s * TOKEN_SLABS, LANES), F32),
                   jax.ShapeDtypeStruct((b, s, LANES), F32),
                   jax.ShapeDtypeStruct((8, LANES), F32)],
        scratch_shapes=[pltpu.VMEM((8, LANES), F32), pltpu.VMEM((d, d), BF16)],
        compiler_params=_params("arbitrary", "arbitrary"),
        name="out_proj_router",
    )(x, pool, att, mod, g, w_out, wr, br)


def _disp_kernel(zrow_ref, d0_ref, d1_ref, h_ref, xs_ref, zbuf_ref, sems):
    rows = h_ref.shape[0]

    @pl.when(pl.program_id(0) == 0)
    def _():
        zbuf_ref[...] = jnp.zeros_like(zbuf_ref)

        def clear(row):
            return pltpu.make_async_copy(zbuf_ref, xs_ref.at[pl.ds(row, EXPERT_ROWS)], sems.at[2])

        def each_clear(act):
            for e in range(N_EXPERTS):
                @pl.when(zrow_ref[e] >= 0)
                def _():
                    act(clear(zrow_ref[e]))

            def tail(blk, carry):
                act(clear(pl.multiple_of(blk * EXPERT_ROWS, EXPERT_ROWS)))
                return carry
            lax.fori_loop(zrow_ref[N_EXPERTS], xs_ref.shape[0] // EXPERT_ROWS, tail, 0)

        each_clear(lambda cp: cp.start())
        each_clear(lambda cp: cp.wait())

    def issue(blk, carry):
        for u in range(ROW_UNROLL):
            r = blk * ROW_UNROLL + u
            pltpu.make_async_copy(h_ref.at[r], xs_ref.at[d0_ref[0, 0, r]], sems.at[0]).start(priority=0)
            pltpu.make_async_copy(h_ref.at[r], xs_ref.at[d1_ref[0, 0, r]], sems.at[1]).start(priority=1)
        return carry
    lax.fori_loop(0, rows // ROW_UNROLL, issue, 0)
    pltpu.make_async_copy(h_ref, xs_ref.at[pl.ds(0, rows)], sems.at[0]).wait()
    pltpu.make_async_copy(h_ref, xs_ref.at[pl.ds(0, rows)], sems.at[1]).wait()


def _disp_call(h2t, d0, d1, zrow, p_rows):
    n = h2t.shape[0]
    t = ROW_TILE
    idx = pl.BlockSpec((1, 1, t), lambda i, z: (i, 0, 0), memory_space=pltpu.SMEM)
    tile = (TOKEN_SLABS, LANES)
    grid_spec = pltpu.PrefetchScalarGridSpec(
        num_scalar_prefetch=1,
        grid=(n // t,),
        in_specs=[idx, idx, pl.BlockSpec((t,) + tile, lambda i, z: (i, 0, 0))],
        out_specs=pl.BlockSpec(memory_space=pl.ANY),
        scratch_shapes=[pltpu.VMEM((EXPERT_ROWS,) + tile, F32), pltpu.SemaphoreType.DMA((3,))],
    )
    return pl.pallas_call(
        _disp_kernel,
        grid_spec=grid_spec,
        out_shape=jax.ShapeDtypeStruct((p_rows,) + tile, F32),
        compiler_params=_params("arbitrary"),
        name="dispatch_rows",
    )(zrow, d0.reshape(n // t, 1, t), d1.reshape(n // t, 1, t), h2t)


def _exp_kernel(be_ref, nu_ref, xs_ref, wg_ref, wu_ref, wd_ref, y_ref, wgb_ref, wub_ref, wdb_ref):
    i = pl.program_id(0)
    rows = xs_ref.shape[0] // TOKEN_SLABS
    used = i < nu_ref[0]
    new_expert = jnp.logical_or(i == 0, be_ref[i] != be_ref[jnp.maximum(i - 1, 0)])

    @pl.when(jnp.logical_and(used, new_expert))
    def _():
        wgb_ref[...] = wg_ref[0, 0].astype(BF16)
        wub_ref[...] = wu_ref[0, 0].astype(BF16)
        wdb_ref[...] = wd_ref[0, 0].astype(BF16)

    @pl.when(used)
    def _():
        xb = _load_token_tiles(xs_ref, rows).astype(BF16)
        gate = jnp.dot(xb, wgb_ref[...], preferred_element_type=F32)
        up = jnp.dot(xb, wub_ref[...], preferred_element_type=F32)
        mid = (jax.nn.silu(gate) * up).astype(BF16)
        _store_token_tiles(y_ref, jnp.dot(mid, wdb_ref[...], preferred_element_type=F32))

    @pl.when(jnp.logical_not(used))
    def _():
        y_ref[...] = jnp.zeros_like(y_ref)


def _exp_call(blk_e, n_used, xs, wg, wu, wd, layer):
    p = xs.shape[0] // TOKEN_SLABS
    r = EXPERT_ROWS
    blk = (r * TOKEN_SLABS, LANES)
    wspec = lambda shape: pl.BlockSpec((1, 1) + shape, lambda i, be, nu: (layer, be[i], 0, 0))
    grid_spec = pltpu.PrefetchScalarGridSpec(
        num_scalar_prefetch=2,
        grid=(p // r,),
        in_specs=[pl.BlockSpec(blk, lambda i, be, nu: (jnp.minimum(i, nu[0] - 1), 0)),
                  wspec((D_MODEL, D_EXPERT)), wspec((D_MODEL, D_EXPERT)), wspec((D_EXPERT, D_MODEL))],
        out_specs=pl.BlockSpec(blk, lambda i, be, nu: (i, 0)),
        scratch_shapes=[pltpu.VMEM((D_MODEL, D_EXPERT), BF16), pltpu.VMEM((D_MODEL, D_EXPERT), BF16),
                        pltpu.VMEM((D_EXPERT, D_MODEL), BF16)],
    )
    return pl.pallas_call(
        _exp_kernel,
        grid_spec=grid_spec,
        out_shape=jax.ShapeDtypeStruct(xs.shape, F32),
        compiler_params=_params("arbitrary"),
        name="experts",
    )(blk_e, n_used, xs, wg, wu, wd)


def _gather_scratch(t):
    return [pltpu.VMEM((2, 2, t * TOKEN_SLABS, LANES), F32), pltpu.SemaphoreType.DMA((2, 2))]


def _moe_residual(d0c_ref, d1c_ref, d0n_ref, d1n_ref, x_ref, meta_ref, mod_ref, ys_ref, ybuf_ref, sems):
    rows = x_ref.shape[1]
    n = pl.program_id(0) * pl.num_programs(1) + pl.program_id(1)
    last = pl.num_programs(0) * pl.num_programs(1) - 1

    def copies(i0_ref, i1_ref, slot, r, vmem_rows):
        return (pltpu.make_async_copy(ys_ref.at[i0_ref[0, 0, r]], ybuf_ref.at[slot, 0, vmem_rows], sems.at[slot, 0]),
                pltpu.make_async_copy(ys_ref.at[i1_ref[0, 0, r]], ybuf_ref.at[slot, 1, vmem_rows], sems.at[slot, 1]))

    def wait(slot):
        for k in range(2):
            buf = ybuf_ref.at[slot, k]
            pltpu.make_async_copy(buf, buf, sems.at[slot, k]).wait()

    @pl.when(n == 0)
    def _():
        def issue(r, carry):
            c0, c1 = copies(d0c_ref, d1c_ref, 0, r, pl.ds(pl.multiple_of(r * TOKEN_SLABS, TOKEN_SLABS), TOKEN_SLABS))
            c0.start(priority=0)
            c1.start(priority=1)
            return carry
        lax.fori_loop(0, rows, issue, 0)

    nxt = (n + 1) % 2
    for r in range(rows):
        c0, c1 = copies(d0n_ref, d1n_ref, nxt, r, pl.ds(r * TOKEN_SLABS, TOKEN_SLABS))
        c0.start(priority=0)
        c1.start(priority=1)

    slot = n % 2
    wait(slot)

    @pl.when(n == last)
    def _():
        wait(nxt)

    w0 = meta_ref[0, :, META_W0:META_W0 + 1]
    w1 = meta_ref[0, :, META_W1:META_W1 + 1]
    moe = _load_token_tiles(ybuf_ref.at[slot, 0], rows) * w0 + _load_token_tiles(ybuf_ref.at[slot, 1], rows) * w1
    return x_ref[0] + mod_ref[0, 5:6, :] * moe


def _row_index_specs(d0, d1, b, steps, t):
    last = b * steps - 1
    cur = pl.BlockSpec((1, 1, t), lambda bi, j: (bi * steps + j, 0, 0), memory_space=pltpu.SMEM)
    nxt = pl.BlockSpec((1, 1, t), lambda bi, j: (jnp.minimum(bi * steps + j + 1, last), 0, 0),
                       memory_space=pltpu.SMEM)
    d0 = d0.reshape(b * steps, 1, t)
    d1 = d1.reshape(b * steps, 1, t)
    return [cur, cur, nxt, nxt], (d0, d1, d0, d1)


def _comb_kernel(d0c_ref, d1c_ref, d0n_ref, d1n_ref, x_ref, meta_ref, mod_ref, ys_ref, o_ref, ybuf_ref, sems):
    o_ref[0] = _moe_residual(d0c_ref, d1c_ref, d0n_ref, d1n_ref, x_ref, meta_ref, mod_ref, ys_ref,
                             ybuf_ref, sems)


def _comb_call(x1, meta, mod, ys, d0, d1):
    b, s, d = x1.shape
    t = SEQ_TILE
    steps = s // t
    idx_specs, idx_args = _row_index_specs(d0, d1, b, steps, t)
    tok = lambda w: pl.BlockSpec((1, t, w), lambda bi, j: (bi, j, 0))
    return pl.pallas_call(
        _comb_kernel,
        grid=(b, steps),
        in_specs=idx_specs + [tok(d), tok(LANES),
                              pl.BlockSpec((1, 6, d), lambda bi, j: (bi, 0, 0)),
                              pl.BlockSpec(memory_space=pl.ANY)],
        out_specs=tok(d),
        out_shape=jax.ShapeDtypeStruct((b, s, d), F32),
        scratch_shapes=_gather_scratch(t),
        compiler_params=_params("arbitrary", "arbitrary"),
        name="combine_rows",
    )(*idx_args, x1, meta, mod, ys)


def _dispatch_plan(meta, counts, n_rows):
    r = EXPERT_ROWS
    e0 = meta[:, META_E0].astype(jnp.int32)
    e1 = meta[:, META_E1].astype(jnp.int32)
    cnt = counts[0, ROUTER_EXPERT_LANE:ROUTER_EXPERT_LANE + N_EXPERTS].astype(jnp.int32)
    padded = ((cnt + r - 1) // r) * r
    pad_end = jnp.cumsum(padded)
    pad_start = pad_end - padded
    experts = jnp.arange(N_EXPERTS, dtype=jnp.int32)[None, :]
    start_of = lambda e: jnp.sum(jnp.where(e[:, None] == experts, pad_start[None, :], 0), axis=1)
    d0 = start_of(e0) + meta[:, META_R0].astype(jnp.int32)
    d1 = start_of(e1) + meta[:, META_R1].astype(jnp.int32)
    n_used = (pad_end[-1:] // r).astype(jnp.int32)
    blk_row = jnp.minimum(jnp.arange(n_rows // r, dtype=jnp.int32), n_used - 1) * r
    blk_e = jnp.sum((pad_end[None, :] <= blk_row[:, None]).astype(jnp.int32), axis=1)
    zrow = jnp.concatenate([jnp.where(padded > 0, pad_end - r, -1).astype(jnp.int32), n_used])
    return d0, d1, blk_e, n_used, zrow


def kernel(x, c, ada_w, ada_b, norm1_g, norm2_g, w_in, pool_w, pool_scale, q_norm_g, k_norm_g, rel_bias,
           w_out, router_group_w, router_group_b, router_expert_w, router_expert_b,
           moe_w_gate, moe_w_up, moe_w_down):
    b, s, d = x.shape
    depth = ada_w.shape[0]
    n = b * s
    assert d == D_MODEL and b <= 8 and s % SEQ_TILE == 0 and s % ATT_TILE == 0 and n % ROW_TILE == 0
    p_rows = n * 2 + N_EXPERTS * EXPERT_ROWS
    tile = (TOKEN_SLABS, LANES)

    c_pad = jnp.zeros((8, d), F32).at[:b].set(c)
    mod_all = _ada_call(c_pad, ada_w, ada_b)
    bias = _group_bias(rel_bias)
    head_block = jnp.arange(ATT_WIDTH) // HEAD_DIM
    bd = (head_block[:, None] == head_block[None, :]).astype(BF16)
    sm_scale = HEAD_DIM ** -0.5

    moe = None
    for l in range(depth):
        mod = mod_all[l, :b].reshape(b, 6, d)
        qg = (jnp.tile(q_norm_g[l], N_HEADS) * sm_scale).reshape(1, ATT_WIDTH)
        kg = jnp.tile(k_norm_g[l], N_HEADS).reshape(1, ATT_WIDTH)
        outs = _in_call(x, mod, norm1_g[l].reshape(1, d), w_in, pool_w, l,
                        pool_scale[l].reshape(1, POOL_WIDTH), qg, kg, bd, moe)
        if moe is not None:
            x, outs = outs[0], outs[1:]
        pool, q, k, v = outs
        att = _att_call(q, k, v, bias)

        wr = jnp.zeros((d, LANES), F32)
        wr = wr.at[:, ROUTER_GROUP_LANE:ROUTER_GROUP_LANE + N_GROUPS].set(router_group_w[l])
        wr = wr.at[:, ROUTER_EXPERT_LANE:ROUTER_EXPERT_LANE + N_EXPERTS].set(router_expert_w[l])
        wr_hi = wr.astype(BF16)
        wr_lo = (wr - wr_hi.astype(F32)).astype(BF16)
        br = jnp.zeros((1, LANES), F32)
        br = br.at[0, ROUTER_GROUP_LANE:ROUTER_GROUP_LANE + N_GROUPS].set(router_group_b[l])
        br = br.at[0, ROUTER_EXPERT_LANE:ROUTER_EXPERT_LANE + N_EXPERTS].set(router_expert_b[l])
        x1, h2t, meta, counts = _out_call(x, pool, att, mod, norm2_g[l].reshape(1, d), w_out, l,
                                          jnp.concatenate([wr_hi, wr_lo], axis=1), br)

        d0, d1, blk_e, n_used, zrow = _dispatch_plan(meta.reshape(n, LANES), counts, p_rows)
        xs = _disp_call(h2t.reshape((n,) + tile), d0, d1, zrow, p_rows)
        ys = _exp_call(blk_e, n_used, xs.reshape(p_rows * TOKEN_SLABS, LANES),
                       moe_w_gate, moe_w_up, moe_w_down, l)
        x = x1
        moe = (meta, mod, ys.reshape((p_rows,) + tile), d0, d1)
    return _comb_call(x, *moe)
```

```python
import jax
import jax.numpy as jnp
from jax import lax
from jax.experimental import pallas as pl
from jax.experimental.pallas import tpu as pltpu

F32 = jnp.float32
BF16 = jnp.bfloat16

D_MODEL = 1024
CHUNK = 64
POOL_WIDTH = 512
POOL_WINDOWS = (2, 4, 8, 16)
POOL_GW = 128
POOL_HALO = 16
ATT_WIDTH = 512
N_HEADS = 8
HEAD_DIM = 64
LEFT = 512
MAX_REL = 128
IN_WIDTH = 2048
N_GROUPS = 4
EXPERTS_PER_GROUP = 8
N_EXPERTS = 32
D_EXPERT = 512
EPS = 1e-6
NEG_INF = -1e30

LANES = 128
TOKEN_SLABS = D_MODEL // LANES
SEQ_TILE = 512
ATT_TILE = 1024
ATT_GROUP = 4 * CHUNK
ATT_BAND = LEFT + ATT_GROUP
ATT_VARIANTS = LEFT // ATT_GROUP + 1
LOG2_E = 1.4426950408889634
EXPERT_ROWS = 512
ROW_TILE = 1024
ROW_UNROLL = 8
VMEM_LIMIT = 56 * 1024 * 1024

META_E0, META_E1, META_W0, META_W1, META_R0, META_R1 = 0, 1, 2, 3, 4, 5
ROUTER_GROUP_LANE = 0
ROUTER_EXPERT_LANE = N_GROUPS


def _params(*sem):
    return pltpu.CompilerParams(dimension_semantics=sem, vmem_limit_bytes=VMEM_LIMIT)


def _first_step():
    return jnp.logical_and(pl.program_id(0) == 0, pl.program_id(1) == 0)


def _load_token_tiles(ref, rows):
    return jnp.concatenate([ref[pl.ds(s, rows, stride=TOKEN_SLABS), :] for s in range(TOKEN_SLABS)], axis=1)


def _store_token_tiles(ref, val):
    rows = val.shape[0]
    for s in range(TOKEN_SLABS):
        ref[pl.ds(s, rows, stride=TOKEN_SLABS), :] = val[:, s * LANES:(s + 1) * LANES]


def _ada_kernel(c_ref, w_ref, b_ref, o_ref):
    ca = jax.nn.silu(c_ref[...])
    o_ref[0] = jnp.dot(ca, w_ref[0], precision=lax.Precision.HIGHEST,
                       preferred_element_type=F32) + b_ref[0]


def _ada_call(c_pad, ada_w, ada_b):
    depth = ada_w.shape[0]
    tn = 1024
    return pl.pallas_call(
        _ada_kernel,
        grid=(depth, 6 * D_MODEL // tn),
        in_specs=[
            pl.BlockSpec((8, D_MODEL), lambda l, n: (0, 0)),
            pl.BlockSpec((1, D_MODEL, tn), lambda l, n: (l, 0, n)),
            pl.BlockSpec((1, 1, tn), lambda l, n: (l, 0, n)),
        ],
        out_specs=pl.BlockSpec((1, 8, tn), lambda l, n: (l, 0, n)),
        out_shape=jax.ShapeDtypeStruct((depth, 8, 6 * D_MODEL), F32),
        compiler_params=_params("arbitrary", "arbitrary"),
        name="ada_mod",
    )(c_pad, ada_w, ada_b.reshape(depth, 1, 6 * D_MODEL))


def _in_kernel(x_ref, *refs):
    _in_body(x_ref[0], *refs)


def _in_comb_kernel(d0c_ref, d1c_ref, d0n_ref, d1n_ref, x1_ref, meta_ref, modp_ref, ys_ref,
                    mod_ref, g_ref, w_ref, pw_ref, ps_ref, qg_ref, kg_ref, bd_ref,
                    xn_ref, pool_ref, q_ref, k_ref, v_ref, ext_ref, wb_ref, ybuf_ref, sems):
    x = _moe_residual(d0c_ref, d1c_ref, d0n_ref, d1n_ref, x1_ref, meta_ref, modp_ref, ys_ref, ybuf_ref, sems)
    xn_ref[0] = x
    _in_body(x, mod_ref, g_ref, w_ref, pw_ref, ps_ref, qg_ref, kg_ref, bd_ref,
             pool_ref, q_ref, k_ref, v_ref, ext_ref, wb_ref)


def _in_body(x, mod_ref, g_ref, w_ref, pw_ref, ps_ref, qg_ref, kg_ref, bd_ref,
             pool_ref, q_ref, k_ref, v_ref, ext_ref, wb_ref):
    j = pl.program_id(1)
    t = x.shape[0]

    @pl.when(_first_step())
    def _():
        wb_ref[...] = w_ref[0].astype(BF16)

    ms = jnp.mean(x * x, axis=-1, keepdims=True)
    y = x * lax.rsqrt(ms + EPS) * g_ref[...]
    h = y * (1.0 + mod_ref[0, 1:2, :]) + mod_ref[0, 0:1, :]
    z = jnp.dot(h.astype(BF16), wb_ref[...], preferred_element_type=F32)

    bd = bd_ref[...]
    q = z[:, POOL_WIDTH:POOL_WIDTH + ATT_WIDTH]
    k = z[:, POOL_WIDTH + ATT_WIDTH:POOL_WIDTH + 2 * ATT_WIDTH]
    qss = jnp.dot((q * q).astype(BF16), bd, preferred_element_type=F32)
    kss = jnp.dot((k * k).astype(BF16), bd, preferred_element_type=F32)
    q_ref[0] = (q * lax.rsqrt(qss * (1.0 / HEAD_DIM) + EPS) * qg_ref[...]).astype(BF16)
    k_ref[0] = (k * lax.rsqrt(kss * (1.0 / HEAD_DIM) + EPS) * kg_ref[...]).astype(BF16)
    v_ref[0] = z[:, POOL_WIDTH + 2 * ATT_WIDTH:].astype(BF16)

    @pl.when(j == 0)
    def _():
        ext_ref[0:POOL_HALO, :] = jnp.zeros((POOL_HALO, POOL_WIDTH), F32)

    ext_ref[POOL_HALO:, :] = z[:, :POOL_WIDTH]
    pos = j * t + lax.broadcasted_iota(jnp.int32, (t, 1), 0)
    outs = []
    for gi, win in enumerate(POOL_WINDOWS):
        a = ext_ref[:, gi * POOL_GW:(gi + 1) * POOL_GW]
        s = a
        shift = 1
        while shift < win:
            s = s + pltpu.roll(s, shift, 0)
            shift *= 2
        cnt = jnp.minimum(pos + 1, win).astype(F32)
        pooled = s[POOL_HALO:, :] / cnt - a[POOL_HALO:, :]
        outs.append(jnp.dot(pooled.astype(BF16), pw_ref[0, gi].astype(BF16), preferred_element_type=F32))
    pool_ref[0] = (jnp.concatenate(outs, axis=1) * ps_ref[...]).astype(BF16)
    ext_ref[0:POOL_HALO, :] = ext_ref[t:t + POOL_HALO, :]


def _in_call(x, mod, g, w_in, pool_w, layer, pool_scale, qg, kg, bd, moe=None):
    b, s, d = x.shape
    t = SEQ_TILE
    steps = s // t
    tok = lambda w: pl.BlockSpec((1, t, w), lambda bi, j: (bi, j, 0))
    const2 = lambda shape: pl.BlockSpec(shape, lambda bi, j: (0, 0))
    mod_spec = pl.BlockSpec((1, 6, d), lambda bi, j: (bi, 0, 0))
    out = jax.ShapeDtypeStruct((b, s, ATT_WIDTH), BF16)
    in_specs = [
        mod_spec,
        const2((1, d)),
        pl.BlockSpec((1, d, IN_WIDTH), lambda bi, j: (layer, 0, 0)),
        pl.BlockSpec((1, len(POOL_WINDOWS), POOL_GW, POOL_GW), lambda bi, j: (layer, 0, 0, 0)),
        const2((1, POOL_WIDTH)),
        const2((1, ATT_WIDTH)),
        const2((1, ATT_WIDTH)),
        const2((ATT_WIDTH, ATT_WIDTH)),
    ]
    args = (mod, g, w_in, pool_w, pool_scale, qg, kg, bd)
    out_specs = [tok(POOL_WIDTH), tok(ATT_WIDTH), tok(ATT_WIDTH), tok(ATT_WIDTH)]
    out_shape = [jax.ShapeDtypeStruct((b, s, POOL_WIDTH), BF16), out, out, out]
    scratch = [pltpu.VMEM((t + POOL_HALO, POOL_WIDTH), F32), pltpu.VMEM((d, IN_WIDTH), BF16)]
    if moe is None:
        return pl.pallas_call(
            _in_kernel, grid=(b, steps), in_specs=[tok(d)] + in_specs, out_specs=out_specs, out_shape=out_shape,
            scratch_shapes=scratch, compiler_params=_params("arbitrary", "arbitrary"), name="in_proj",
        )(x, *args)
    meta, mod_prev, ys, d0, d1 = moe
    idx_specs, idx_args = _row_index_specs(d0, d1, b, steps, t)
    return pl.pallas_call(
        _in_comb_kernel, grid=(b, steps),
        in_specs=idx_specs + [tok(d), tok(LANES), mod_spec, pl.BlockSpec(memory_space=pl.ANY)] + in_specs,
        out_specs=[tok(d)] + out_specs,
        out_shape=[jax.ShapeDtypeStruct((b, s, d), F32)] + out_shape,
        scratch_shapes=scratch + _gather_scratch(t),
        compiler_params=_params("arbitrary", "arbitrary"), name="combine_in_proj",
    )(*idx_args, x, meta, mod_prev, ys, *args)


def _att_kernel(q_ref, kp_ref, kc_ref, vp_ref, vc_ref, bias_ref, o_ref):
    i = pl.program_id(2)
    tq = q_ref.shape[1]
    lane = lax.broadcasted_iota(jnp.int32, (1, LANES), 1)
    low_half = lane < HEAD_DIM
    for g in range(tq // ATT_GROUP):
        r0 = g * ATT_GROUP
        qg = q_ref[0, r0:r0 + ATT_GROUP, :]
        if r0 < LEFT:
            kb = jnp.concatenate([kp_ref[0, r0:LEFT, :], kc_ref[0, 0:r0 + ATT_GROUP, :]], axis=0)
            vb = jnp.concatenate([vp_ref[0, r0:LEFT, :], vc_ref[0, 0:r0 + ATT_GROUP, :]], axis=0)
            variant = jnp.where(i == 0, g + 1, 0)
        else:
            kb = kc_ref[0, r0 - LEFT:r0 + ATT_GROUP, :]
            vb = vc_ref[0, r0 - LEFT:r0 + ATT_GROUP, :]
            variant = 0
        zero = jnp.zeros_like(qg)
        qs = jnp.concatenate([jnp.where(low_half, qg, zero), jnp.where(low_half, zero, qg)], axis=0)
        s = lax.dot_general(qs, kb, (((1,), (1,)), ((), ())), preferred_element_type=F32)
        s = s + bias_ref[variant]
        m = jnp.max(s, axis=-1, keepdims=True)
        e = jnp.exp2(s - m)
        l = jnp.sum(e, axis=-1, keepdims=True)
        pv = jnp.dot(e.astype(BF16), vb, preferred_element_type=F32) / l
        o_ref[0, r0:r0 + ATT_GROUP, :] = jnp.where(low_half, pv[:ATT_GROUP], pv[ATT_GROUP:]).astype(BF16)


def _att_call(q, k, v, bias):
    b, s, _ = q.shape
    tq = ATT_TILE
    lb = tq // LEFT
    cur = pl.BlockSpec((1, tq, LANES), lambda bi, hp, i: (bi, i, hp))
    prev = pl.BlockSpec((1, LEFT, LANES), lambda bi, hp, i: (bi, jnp.maximum(i * lb - 1, 0), hp))
    return pl.pallas_call(
        _att_kernel,
        grid=(b, N_HEADS // 2, s // tq),
        in_specs=[cur, prev, cur, prev, cur,
                  pl.BlockSpec((ATT_VARIANTS, 2 * ATT_GROUP, ATT_BAND), lambda bi, hp, i: (0, hp, 0))],
        out_specs=cur,
        out_shape=jax.ShapeDtypeStruct((b, s, ATT_WIDTH), BF16),
        compiler_params=_params("arbitrary", "arbitrary", "arbitrary"),
        name="chunk_attn",
    )(q, k, k, v, v, bias)


def _group_bias(rel_bias):
    qi = jnp.arange(ATT_GROUP)[:, None]
    kj = jnp.arange(ATT_BAND)[None, :]
    first = (qi // CHUNK) * CHUNK
    visible = (kj >= first) & (kj < first + LEFT + CHUNK)
    lo = ATT_GROUP - 1
    assert lo >= MAX_REL and ATT_BAND - 1 >= MAX_REL
    tab = rel_bias.astype(F32)
    by_rel = jnp.concatenate([jnp.repeat(tab[:, :1], lo - MAX_REL, axis=1), tab,
                              jnp.repeat(tab[:, -1:], ATT_BAND - 1 - MAX_REL, axis=1)], axis=1)
    rev = by_rel[:, ::-1]
    bias = jnp.stack([rev[:, lo - q:lo - q + ATT_BAND] for q in range(ATT_GROUP)], axis=1) * LOG2_E
    tables = []
    for variant in range(ATT_VARIANTS):
        ok = visible if variant == 0 else visible & (kj >= LEFT - (variant - 1) * ATT_GROUP)
        tables.append(jnp.where(ok[None], bias, NEG_INF))
    return jnp.stack(tables, axis=0).reshape(ATT_VARIANTS, N_HEADS * ATT_GROUP, ATT_BAND)


def _out_kernel(x_ref, pool_ref, att_ref, mod_ref, g_ref, wo_ref, wr_ref, br_ref,
                x1_ref, h2_ref, meta_ref, cnt_ref, carry_ref, wob_ref):
    t = x_ref.shape[1]

    @pl.when(_first_step())
    def _():
        carry_ref[...] = jnp.zeros_like(carry_ref)
        wob_ref[...] = wo_ref[0].astype(BF16)

    mix = (jnp.dot(pool_ref[0], wob_ref[:POOL_WIDTH, :], preferred_element_type=F32)
           + jnp.dot(att_ref[0], wob_ref[POOL_WIDTH:, :], preferred_element_type=F32))
    x1 = x_ref[0] + mod_ref[0, 2:3, :] * mix
    x1_ref[0] = x1
    ms = jnp.mean(x1 * x1, axis=-1, keepdims=True)
    y = x1 * lax.rsqrt(ms + EPS) * g_ref[...]
    h2 = y * (1.0 + mod_ref[0, 4:5, :]) + mod_ref[0, 3:4, :]

    _store_token_tiles(h2_ref, h2)
    hb = h2.astype(BF16)

    h_lo = (h2 - hb.astype(F32)).astype(BF16)
    r = (jnp.dot(hb, wr_ref[...], preferred_element_type=F32)
         + jnp.dot(h_lo, wr_ref[...], preferred_element_type=F32))
    logits = r[:, :LANES] + r[:, LANES:] + br_ref[...]

    lane = lax.broadcasted_iota(jnp.int32, (t, LANES), 1).astype(F32)
    ninf = jnp.float32(-jnp.inf)
    gmask = lane < N_GROUPS
    gl = jnp.where(gmask, logits, ninf)
    gmax = jnp.max(gl, axis=-1, keepdims=True)
    gsum = jnp.sum(jnp.where(gmask, jnp.exp(logits - gmax), 0.0), axis=-1, keepdims=True)
    g_p = 1.0 / gsum
    g_idx = jnp.min(jnp.where(gl == gmax, lane, float(LANES)), axis=-1, keepdims=True)
    e_lo = ROUTER_EXPERT_LANE + EXPERTS_PER_GROUP * g_idx
    el = jnp.where((lane >= e_lo) & (lane < e_lo + EXPERTS_PER_GROUP), logits, ninf)
    m1 = jnp.max(el, axis=-1, keepdims=True)
    i1 = jnp.min(jnp.where(el == m1, lane, float(LANES)), axis=-1, keepdims=True)
    el2 = jnp.where(lane == i1, ninf, el)
    m2 = jnp.max(el2, axis=-1, keepdims=True)
    i2 = jnp.min(jnp.where(el2 == m2, lane, float(LANES)), axis=-1, keepdims=True)
    e2 = jnp.exp(m2 - m1)
    w1 = g_p / (1.0 + e2)
    w2 = g_p * e2 / (1.0 + e2)

    hot = ((lane == i1) | (lane == i2))
    hot_f = hot.astype(F32)
    row = lax.broadcasted_iota(jnp.int32, (t, t), 0)
    colt = lax.broadcasted_iota(jnp.int32, (t, t), 1)
    lower = (colt < row).astype(BF16)
    before = jnp.dot(lower, hot.astype(BF16), preferred_element_type=F32) + carry_ref[0:1, :]
    r1 = jnp.sum(jnp.where(lane == i1, before, 0.0), axis=-1, keepdims=True)
    r2 = jnp.sum(jnp.where(lane == i2, before, 0.0), axis=-1, keepdims=True)
    carry_ref[0:1, :] = carry_ref[0:1, :] + jnp.sum(hot_f, axis=0, keepdims=True)
    cnt_ref[...] = jnp.broadcast_to(carry_ref[0:1, :], cnt_ref.shape)

    meta = jnp.where(lane == META_E0, i1 - ROUTER_EXPERT_LANE, 0.0)
    meta = jnp.where(lane == META_E1, i2 - ROUTER_EXPERT_LANE, meta)
    meta = jnp.where(lane == META_W0, w1, meta)
    meta = jnp.where(lane == META_W1, w2, meta)
    meta = jnp.where(lane == META_R0, r1, meta)
    meta = jnp.where(lane == META_R1, r2, meta)
    meta_ref[0] = meta


def _out_call(x, pool, att, mod, g, w_out, layer, wr, br):
    b, s, d = x.shape
    t = SEQ_TILE
    steps = s // t
    tok = lambda w: pl.BlockSpec((1, t, w), lambda bi, j: (bi, j, 0))
    const2 = lambda shape: pl.BlockSpec(shape, lambda bi, j: (0, 0))
    return pl.pallas_call(
        _out_kernel,
        grid=(b, steps),
        in_specs=[tok(d), tok(POOL_WIDTH), tok(ATT_WIDTH),
                  pl.BlockSpec((1, 6, d), lambda bi, j: (bi, 0, 0)),
                  const2((1, d)), pl.BlockSpec((1, d, d), lambda bi, j: (layer, 0, 0)),
                  const2((d, 2 * LANES)), const2((1, LANES))],
        out_specs=[tok(d), pl.BlockSpec((t * TOKEN_SLABS, LANES), lambda bi, j: (bi * steps + j, 0)),
                   tok(LANES), const2((8, LANES))],
        out_shape=[jax.ShapeDtypeStruct((b, s, d), F32),
                   jax.ShapeDtypeStruct((b * s * TOKEN_SLABS, LANES), F32),
                   jax.ShapeDtypeStruct((b, s, LANES), F32),
                   jax.ShapeDtypeStruct((8, LANES), F32)],
        scratch_shapes=[pltpu.VMEM((8, LANES), F32), pltpu.VMEM((d, d), BF16)],
        compiler_params=_params("arbitrary", "arbitrary"),
        name="out_proj_router",
    )(x, pool, att, mod, g, w_out, wr, br)


def _disp_kernel(zrow_ref, d0_ref, d1_ref, h_ref, xs_ref, zbuf_ref, sems):
    rows = h_ref.shape[0]

    @pl.when(pl.program_id(0) == 0)
    def _():
        zbuf_ref[...] = jnp.zeros_like(zbuf_ref)

        def clear(row):
            return pltpu.make_async_copy(zbuf_ref, xs_ref.at[pl.ds(row, EXPERT_ROWS)], sems.at[2])

        def each_clear(act):
            for e in range(N_EXPERTS):
                @pl.when(zrow_ref[e] >= 0)
                def _():
                    act(clear(zrow_ref[e]))

            def tail(blk, carry):
                act(clear(pl.multiple_of(blk * EXPERT_ROWS, EXPERT_ROWS)))
                return carry
            lax.fori_loop(zrow_ref[N_EXPERTS], xs_ref.shape[0] // EXPERT_ROWS, tail, 0)

        each_clear(lambda cp: cp.start())
        each_clear(lambda cp: cp.wait())

    def issue(blk, carry):
        for u in range(ROW_UNROLL):
            r = blk * ROW_UNROLL + u
            pltpu.make_async_copy(h_ref.at[r], xs_ref.at[d0_ref[0, 0, r]], sems.at[0]).start(priority=0)
            pltpu.make_async_copy(h_ref.at[r], xs_ref.at[d1_ref[0, 0, r]], sems.at[1]).start(priority=1)
        return carry
    lax.fori_loop(0, rows // ROW_UNROLL, issue, 0)
    pltpu.make_async_copy(h_ref, xs_ref.at[pl.ds(0, rows)], sems.at[0]).wait()
    pltpu.make_async_copy(h_ref, xs_ref.at[pl.ds(0, rows)], sems.at[1]).wait()


def _disp_call(h2t, d0, d1, zrow, p_rows):
    n = h2t.shape[0]
    t = ROW_TILE
    idx = pl.BlockSpec((1, 1, t), lambda i, z: (i, 0, 0), memory_space=pltpu.SMEM)
    tile = (TOKEN_SLABS, LANES)
    grid_spec = pltpu.PrefetchScalarGridSpec(
        num_scalar_prefetch=1,
        grid=(n // t,),
        in_specs=[idx, idx, pl.BlockSpec((t,) + tile, lambda i, z: (i, 0, 0))],
        out_specs=pl.BlockSpec(memory_space=pl.ANY),
        scratch_shapes=[pltpu.VMEM((EXPERT_ROWS,) + tile, F32), pltpu.SemaphoreType.DMA((3,))],
    )
    return pl.pallas_call(
        _disp_kernel,
        grid_spec=grid_spec,
        out_shape=jax.ShapeDtypeStruct((p_rows,) + tile, F32),
        compiler_params=_params("arbitrary"),
        name="dispatch_rows",
    )(zrow, d0.reshape(n // t, 1, t), d1.reshape(n // t, 1, t), h2t)


def _exp_kernel(be_ref, nu_ref, xs_ref, wg_ref, wu_ref, wd_ref, y_ref, wgb_ref, wub_ref, wdb_ref):
    i = pl.program_id(0)
    rows = xs_ref.shape[0] // TOKEN_SLABS
    used = i < nu_ref[0]
    new_expert = jnp.logical_or(i == 0, be_ref[i] != be_ref[jnp.maximum(i - 1, 0)])

    @pl.when(jnp.logical_and(used, new_expert))
    def _():
        wgb_ref[...] = wg_ref[0, 0].astype(BF16)
        wub_ref[...] = wu_ref[0, 0].astype(BF16)
        wdb_ref[...] = wd_ref[0, 0].astype(BF16)

    @pl.when(used)
    def _():
        xb = _load_token_tiles(xs_ref, rows).astype(BF16)
        gate = jnp.dot(xb, wgb_ref[...], preferred_element_type=F32)
        up = jnp.dot(xb, wub_ref[...], preferred_element_type=F32)
        mid = (jax.nn.silu(gate) * up).astype(BF16)
        _store_token_tiles(y_ref, jnp.dot(mid, wdb_ref[...], preferred_element_type=F32))

    @pl.when(jnp.logical_not(used))
    def _():
        y_ref[...] = jnp.zeros_like(y_ref)


def _exp_call(blk_e, n_used, xs, wg, wu, wd, layer):
    p = xs.shape[0] // TOKEN_SLABS
    r = EXPERT_ROWS
    blk = (r * TOKEN_SLABS, LANES)
    wspec = lambda shape: pl.BlockSpec((1, 1) + shape, lambda i, be, nu: (layer, be[i], 0, 0))
    grid_spec = pltpu.PrefetchScalarGridSpec(
        num_scalar_prefetch=2,
        grid=(p // r,),
        in_specs=[pl.BlockSpec(blk, lambda i, be, nu: (jnp.minimum(i, nu[0] - 1), 0)),
                  wspec((D_MODEL, D_EXPERT)), wspec((D_MODEL, D_EXPERT)), wspec((D_EXPERT, D_MODEL))],
        out_specs=pl.BlockSpec(blk, lambda i, be, nu: (i, 0)),
        scratch_shapes=[pltpu.VMEM((D_MODEL, D_EXPERT), BF16), pltpu.VMEM((D_MODEL, D_EXPERT), BF16),
                        pltpu.VMEM((D_EXPERT, D_MODEL), BF16)],
    )
    return pl.pallas_call(
        _exp_kernel,
        grid_spec=grid_spec,
        out_shape=jax.ShapeDtypeStruct(xs.shape, F32),
        compiler_params=_params("arbitrary"),
        name="experts",
    )(blk_e, n_used, xs, wg, wu, wd)


def _gather_scratch(t):
    return [pltpu.VMEM((2, 2, t * TOKEN_SLABS, LANES), F32), pltpu.SemaphoreType.DMA((2, 2))]


def _moe_residual(d0c_ref, d1c_ref, d0n_ref, d1n_ref, x_ref, meta_ref, mod_ref, ys_ref, ybuf_ref, sems):
    rows = x_ref.shape[1]
    n = pl.program_id(0) * pl.num_programs(1) + pl.program_id(1)
    last = pl.num_programs(0) * pl.num_programs(1) - 1

    def copies(i0_ref, i1_ref, slot, r, vmem_rows):
        return (pltpu.make_async_copy(ys_ref.at[i0_ref[0, 0, r]], ybuf_ref.at[slot, 0, vmem_rows], sems.at[slot, 0]),
                pltpu.make_async_copy(ys_ref.at[i1_ref[0, 0, r]], ybuf_ref.at[slot, 1, vmem_rows], sems.at[slot, 1]))

    def wait(slot):
        for k in range(2):
            buf = ybuf_ref.at[slot, k]
            pltpu.make_async_copy(buf, buf, sems.at[slot, k]).wait()

    @pl.when(n == 0)
    def _():
        def issue(r, carry):
            c0, c1 = copies(d0c_ref, d1c_ref, 0, r, pl.ds(pl.multiple_of(r * TOKEN_SLABS, TOKEN_SLABS), TOKEN_SLABS))
            c0.start(priority=0)
            c1.start(priority=1)
            return carry
        lax.fori_loop(0, rows, issue, 0)

    nxt = (n + 1) % 2
    for r in range(rows):
        c0, c1 = copies(d0n_ref, d1n_ref, nxt, r, pl.ds(r * TOKEN_SLABS, TOKEN_SLABS))
        c0.start(priority=0)
        c1.start(priority=1)

    slot = n % 2
    wait(slot)

    @pl.when(n == last)
    def _():
        wait(nxt)

    w0 = meta_ref[0, :, META_W0:META_W0 + 1]
    w1 = meta_ref[0, :, META_W1:META_W1 + 1]
    moe = _load_token_tiles(ybuf_ref.at[slot, 0], rows) * w0 + _load_token_tiles(ybuf_ref.at[slot, 1], rows) * w1
    return x_ref[0] + mod_ref[0, 5:6, :] * moe


def _row_index_specs(d0, d1, b, steps, t):
    last = b * steps - 1
    cur = pl.BlockSpec((1, 1, t), lambda bi, j: (bi * steps + j, 0, 0), memory_space=pltpu.SMEM)
    nxt = pl.BlockSpec((1, 1, t), lambda bi, j: (jnp.minimum(bi * steps + j + 1, last), 0, 0),
                       memory_space=pltpu.SMEM)
    d0 = d0.reshape(b * steps, 1, t)
    d1 = d1.reshape(b * steps, 1, t)
    return [cur, cur, nxt, nxt], (d0, d1, d0, d1)


def _comb_kernel(d0c_ref, d1c_ref, d0n_ref, d1n_ref, x_ref, meta_ref, mod_ref, ys_ref, o_ref, ybuf_ref, sems):
    o_ref[0] = _moe_residual(d0c_ref, d1c_ref, d0n_ref, d1n_ref, x_ref, meta_ref, mod_ref, ys_ref,
                             ybuf_ref, sems)


def _comb_call(x1, meta, mod, ys, d0, d1):
    b, s, d = x1.shape
    t = SEQ_TILE
    steps = s // t
    idx_specs, idx_args = _row_index_specs(d0, d1, b, steps, t)
    tok = lambda w: pl.BlockSpec((1, t, w), lambda bi, j: (bi, j, 0))
    return pl.pallas_call(
        _comb_kernel,
        grid=(b, steps),
        in_specs=idx_specs + [tok(d), tok(LANES),
                              pl.BlockSpec((1, 6, d), lambda bi, j: (bi, 0, 0)),
                              pl.BlockSpec(memory_space=pl.ANY)],
        out_specs=tok(d),
        out_shape=jax.ShapeDtypeStruct((b, s, d), F32),
        scratch_shapes=_gather_scratch(t),
        compiler_params=_params("arbitrary", "arbitrary"),
        name="combine_rows",
    )(*idx_args, x1, meta, mod, ys)


def _dispatch_plan(meta, counts, n_rows):
    r = EXPERT_ROWS
    e0 = meta[:, META_E0].astype(jnp.int32)
    e1 = meta[:, META_E1].astype(jnp.int32)
    cnt = counts[0, ROUTER_EXPERT_LANE:ROUTER_EXPERT_LANE + N_EXPERTS].astype(jnp.int32)
    padded = ((cnt + r - 1) // r) * r
    pad_end = jnp.cumsum(padded)
    pad_start = pad_end - padded
    experts = jnp.arange(N_EXPERTS, dtype=jnp.int32)[None, :]
    start_of = lambda e: jnp.sum(jnp.where(e[:, None] == experts, pad_start[None, :], 0), axis=1)
    d0 = start_of(e0) + meta[:, META_R0].astype(jnp.int32)
    d1 = start_of(e1) + meta[:, META_R1].astype(jnp.int32)
    n_used = (pad_end[-1:] // r).astype(jnp.int32)
    blk_row = jnp.minimum(jnp.arange(n_rows // r, dtype=jnp.int32), n_used - 1) * r
    blk_e = jnp.sum((pad_end[None, :] <= blk_row[:, None]).astype(jnp.int32), axis=1)
    zrow = jnp.concatenate([jnp.where(padded > 0, pad_end - r, -1).astype(jnp.int32), n_used])
    return d0, d1, blk_e, n_used, zrow


def kernel(x, c, ada_w, ada_b, norm1_g, norm2_g, w_in, pool_w, pool_scale, q_norm_g, k_norm_g, rel_bias,
           w_out, router_group_w, router_group_b, router_expert_w, router_expert_b,
           moe_w_gate, moe_w_up, moe_w_down):
    b, s, d = x.shape
    depth = ada_w.shape[0]
    n = b * s
    assert d == D_MODEL and b <= 8 and s % SEQ_TILE == 0 and s % ATT_TILE == 0 and n % ROW_TILE == 0
    p_rows = n * 2 + N_EXPERTS * EXPERT_ROWS
    tile = (TOKEN_SLABS, LANES)

    c_pad = jnp.zeros((8, d), F32).at[:b].set(c)
    mod_all = _ada_call(c_pad, ada_w, ada_b)
    bias = _group_bias(rel_bias)
    head_block = jnp.arange(ATT_WIDTH) // HEAD_DIM
    bd = (head_block[:, None] == head_block[None, :]).astype(BF16)
    sm_scale = HEAD_DIM ** -0.5 * LOG2_E

    moe = None
    for l in range(depth):
        mod = mod_all[l, :b].reshape(b, 6, d)
        qg = (jnp.tile(q_norm_g[l], N_HEADS) * sm_scale).reshape(1, ATT_WIDTH)
        kg = jnp.tile(k_norm_g[l], N_HEADS).reshape(1, ATT_WIDTH)
        outs = _in_call(x, mod, norm1_g[l].reshape(1, d), w_in, pool_w, l,
                        pool_scale[l].reshape(1, POOL_WIDTH), qg, kg, bd, moe)
        if moe is not None:
            x, outs = outs[0], outs[1:]
        pool, q, k, v = outs
        att = _att_call(q, k, v, bias)

        wr = jnp.zeros((d, LANES), F32)
        wr = wr.at[:, ROUTER_GROUP_LANE:ROUTER_GROUP_LANE + N_GROUPS].set(router_group_w[l])
        wr = wr.at[:, ROUTER_EXPERT_LANE:ROUTER_EXPERT_LANE + N_EXPERTS].set(router_expert_w[l])
        wr_hi = wr.astype(BF16)
        wr_lo = (wr - wr_hi.astype(F32)).astype(BF16)
        br = jnp.zeros((1, LANES), F32)
        br = br.at[0, ROUTER_GROUP_LANE:ROUTER_GROUP_LANE + N_GROUPS].set(router_group_b[l])
        br = br.at[0, ROUTER_EXPERT_LANE:ROUTER_EXPERT_LANE + N_EXPERTS].set(router_expert_b[l])
        x1, h2t, meta, counts = _out_call(x, pool, att, mod, norm2_g[l].reshape(1, d), w_out, l,
                                          jnp.concatenate([wr_hi, wr_lo], axis=1), br)

        d0, d1, blk_e, n_used, zrow = _dispatch_plan(meta.reshape(n, LANES), counts, p_rows)
        xs = _disp_call(h2t.reshape((n,) + tile), d0, d1, zrow, p_rows)
        ys = _exp_call(blk_e, n_used, xs.reshape(p_rows * TOKEN_SLABS, LANES),
                       moe_w_gate, moe_w_up, moe_w_down, l)
        x = x1
        moe = (meta, mod, ys.reshape((p_rows,) + tile), d0, d1)
    return _comb_call(x, *moe)
```

```python
import jax
import jax.numpy as jnp
from jax import lax
from jax.experimental import pallas as pl
from jax.experimental.pallas import tpu as pltpu

F32 = jnp.float32
BF16 = jnp.bfloat16

D_MODEL = 1024
CHUNK = 64
POOL_WIDTH = 512
POOL_WINDOWS = (2, 4, 8, 16)
POOL_GW = 128
POOL_HALO = 16
ATT_WIDTH = 512
N_HEADS = 8
HEAD_DIM = 64
LEFT = 512
MAX_REL = 128
IN_WIDTH = 2048
N_GROUPS = 4
EXPERTS_PER_GROUP = 8
N_EXPERTS = 32
D_EXPERT = 512
EPS = 1e-6
NEG_INF = -1e30

LANES = 128
TOKEN_SLABS = D_MODEL // LANES
SEQ_TILE = 512
ATT_TILE = 1024
ATT_GROUP = 4 * CHUNK
ATT_BAND = LEFT + ATT_GROUP
ATT_VARIANTS = LEFT // ATT_GROUP + 1
LOG2_E = 1.4426950408889634
EXPERT_ROWS = 512
ROW_TILE = 1024
ROW_UNROLL = 8
VMEM_LIMIT = 56 * 1024 * 1024

META_E0, META_E1, META_W0, META_W1, META_R0, META_R1 = 0, 1, 2, 3, 4, 5
ROUTER_GROUP_LANE = 0
ROUTER_EXPERT_LANE = N_GROUPS


def _params(*sem):
    return pltpu.CompilerParams(dimension_semantics=sem, vmem_limit_bytes=VMEM_LIMIT)


def _first_step():
    return jnp.logical_and(pl.program_id(0) == 0, pl.program_id(1) == 0)


def _load_token_tiles(ref, rows):
    return jnp.concatenate([ref[pl.ds(s, rows, stride=TOKEN_SLABS), :] for s in range(TOKEN_SLABS)], axis=1)


def _store_token_tiles(ref, val):
    rows = val.shape[0]
    for s in range(TOKEN_SLABS):
        ref[pl.ds(s, rows, stride=TOKEN_SLABS), :] = val[:, s * LANES:(s + 1) * LANES]


def _ada_kernel(c_ref, w_ref, b_ref, o_ref):
    ca = jax.nn.silu(c_ref[...])
    o_ref[0] = jnp.dot(ca, w_ref[0], precision=lax.Precision.HIGHEST,
                       preferred_element_type=F32) + b_ref[0]


def _ada_call(c_pad, ada_w, ada_b):
    depth = ada_w.shape[0]
    tn = 1024
    return pl.pallas_call(
        _ada_kernel,
        grid=(depth, 6 * D_MODEL // tn),
        in_specs=[
            pl.BlockSpec((8, D_MODEL), lambda l, n: (0, 0)),
            pl.BlockSpec((1, D_MODEL, tn), lambda l, n: (l, 0, n)),
            pl.BlockSpec((1, 1, tn), lambda l, n: (l, 0, n)),
        ],
        out_specs=pl.BlockSpec((1, 8, tn), lambda l, n: (l, 0, n)),
        out_shape=jax.ShapeDtypeStruct((depth, 8, 6 * D_MODEL), F32),
        compiler_params=_params("arbitrary", "arbitrary"),
        name="ada_mod",
    )(c_pad, ada_w, ada_b.reshape(depth, 1, 6 * D_MODEL))


def _in_kernel(x_ref, *refs):
    _in_body(x_ref[0], *refs)


def _in_comb_kernel(d0c_ref, d1c_ref, d0n_ref, d1n_ref, x1_ref, meta_ref, modp_ref, ys_ref,
                    mod_ref, g_ref, w_ref, pw_ref, ps_ref, qg_ref, kg_ref, bd_ref,
                    xn_ref, pool_ref, q_ref, k_ref, v_ref, ext_ref, wb_ref, ybuf_ref, sems):
    x = _moe_residual(d0c_ref, d1c_ref, d0n_ref, d1n_ref, x1_ref, meta_ref, modp_ref, ys_ref, ybuf_ref, sems)
    xn_ref[0] = x
    _in_body(x, mod_ref, g_ref, w_ref, pw_ref, ps_ref, qg_ref, kg_ref, bd_ref,
             pool_ref, q_ref, k_ref, v_ref, ext_ref, wb_ref)


def _in_body(x, mod_ref, g_ref, w_ref, pw_ref, ps_ref, qg_ref, kg_ref, bd_ref,
             pool_ref, q_ref, k_ref, v_ref, ext_ref, wb_ref):
    j = pl.program_id(1)
    t = x.shape[0]

    @pl.when(_first_step())
    def _():
        wb_ref[...] = w_ref[0].astype(BF16)

    ms = jnp.mean(x * x, axis=-1, keepdims=True)
    y = x * lax.rsqrt(ms + EPS) * g_ref[...]
    h = y * (1.0 + mod_ref[0, 1:2, :]) + mod_ref[0, 0:1, :]
    z = jnp.dot(h.astype(BF16), wb_ref[...], preferred_element_type=F32)

    bd = bd_ref[...]
    q = z[:, POOL_WIDTH:POOL_WIDTH + ATT_WIDTH]
    k = z[:, POOL_WIDTH + ATT_WIDTH:POOL_WIDTH + 2 * ATT_WIDTH]
    qss = jnp.dot((q * q).astype(BF16), bd, preferred_element_type=F32)
    kss = jnp.dot((k * k).astype(BF16), bd, preferred_element_type=F32)
    q_ref[0] = (q * lax.rsqrt(qss * (1.0 / HEAD_DIM) + EPS) * qg_ref[...]).astype(BF16)
    k_ref[0] = (k * lax.rsqrt(kss * (1.0 / HEAD_DIM) + EPS) * kg_ref[...]).astype(BF16)
    v_ref[0] = z[:, POOL_WIDTH + 2 * ATT_WIDTH:].astype(BF16)

    @pl.when(j == 0)
    def _():
        ext_ref[0:POOL_HALO, :] = jnp.zeros((POOL_HALO, POOL_WIDTH), F32)

    ext_ref[POOL_HALO:, :] = z[:, :POOL_WIDTH]
    pos = j * t + lax.broadcasted_iota(jnp.int32, (t, 1), 0)
    outs = []
    for gi, win in enumerate(POOL_WINDOWS):
        a = ext_ref[:, gi * POOL_GW:(gi + 1) * POOL_GW]
        s = a
        shift = 1
        while shift < win:
            s = s + pltpu.roll(s, shift, 0)
            shift *= 2
        cnt = jnp.minimum(pos + 1, win).astype(F32)
        pooled = s[POOL_HALO:, :] / cnt - a[POOL_HALO:, :]
        outs.append(jnp.dot(pooled.astype(BF16), pw_ref[0, gi].astype(BF16), preferred_element_type=F32))
    pool_ref[0] = (jnp.concatenate(outs, axis=1) * ps_ref[...]).astype(BF16)
    ext_ref[0:POOL_HALO, :] = ext_ref[t:t + POOL_HALO, :]


def _in_call(x, mod, g, w_in, pool_w, layer, pool_scale, qg, kg, bd, moe=None):
    b, s, d = x.shape
    t = SEQ_TILE
    steps = s // t
    tok = lambda w: pl.BlockSpec((1, t, w), lambda bi, j: (bi, j, 0))
    const2 = lambda shape: pl.BlockSpec(shape, lambda bi, j: (0, 0))
    mod_spec = pl.BlockSpec((1, 6, d), lambda bi, j: (bi, 0, 0))
    out = jax.ShapeDtypeStruct((b, s, ATT_WIDTH), BF16)
    in_specs = [
        mod_spec,
        const2((1, d)),
        pl.BlockSpec((1, d, IN_WIDTH), lambda bi, j: (layer, 0, 0)),
        pl.BlockSpec((1, len(POOL_WINDOWS), POOL_GW, POOL_GW), lambda bi, j: (layer, 0, 0, 0)),
        const2((1, POOL_WIDTH)),
        const2((1, ATT_WIDTH)),
        const2((1, ATT_WIDTH)),
        const2((ATT_WIDTH, ATT_WIDTH)),
    ]
    args = (mod, g, w_in, pool_w, pool_scale, qg, kg, bd)
    out_specs = [tok(POOL_WIDTH), tok(ATT_WIDTH), tok(ATT_WIDTH), tok(ATT_WIDTH)]
    out_shape = [jax.ShapeDtypeStruct((b, s, POOL_WIDTH), BF16), out, out, out]
    scratch = [pltpu.VMEM((t + POOL_HALO, POOL_WIDTH), F32), pltpu.VMEM((d, IN_WIDTH), BF16)]
    if moe is None:
        return pl.pallas_call(
            _in_kernel, grid=(b, steps), in_specs=[tok(d)] + in_specs, out_specs=out_specs, out_shape=out_shape,
            scratch_shapes=scratch, compiler_params=_params("arbitrary", "arbitrary"), name="in_proj",
        )(x, *args)
    meta, mod_prev, ys, d0, d1 = moe
    idx_specs, idx_args = _row_index_specs(d0, d1, b, steps, t)
    return pl.pallas_call(
        _in_comb_kernel, grid=(b, steps),
        in_specs=idx_specs + [tok(d), tok(LANES), mod_spec, pl.BlockSpec(memory_space=pl.ANY)] + in_specs,
        out_specs=[tok(d)] + out_specs,
        out_shape=[jax.ShapeDtypeStruct((b, s, d), F32)] + out_shape,
        scratch_shapes=scratch + _gather_scratch(t),
        compiler_params=_params("arbitrary", "arbitrary"), name="combine_in_proj",
    )(*idx_args, x, meta, mod_prev, ys, *args)


def _att_kernel(q_ref, kp_ref, kc_ref, vp_ref, vc_ref, bias_ref, o_ref):
    i = pl.program_id(2)
    tq = q_ref.shape[1]
    lane = lax.broadcasted_iota(jnp.int32, (1, LANES), 1)
    low_half = lane < HEAD_DIM
    for g in range(tq // ATT_GROUP):
        r0 = g * ATT_GROUP
        qg = q_ref[0, r0:r0 + ATT_GROUP, :]
        if r0 < LEFT:
            kb = jnp.concatenate([kp_ref[0, r0:LEFT, :], kc_ref[0, 0:r0 + ATT_GROUP, :]], axis=0)
            vb = jnp.concatenate([vp_ref[0, r0:LEFT, :], vc_ref[0, 0:r0 + ATT_GROUP, :]], axis=0)
            variant = jnp.where(i == 0, g + 1, 0)
        else:
            kb = kc_ref[0, r0 - LEFT:r0 + ATT_GROUP, :]
            vb = vc_ref[0, r0 - LEFT:r0 + ATT_GROUP, :]
            variant = 0
        zero = jnp.zeros_like(qg)
        qs = jnp.concatenate([jnp.where(low_half, qg, zero), jnp.where(low_half, zero, qg)], axis=0)
        s = lax.dot_general(qs, kb, (((1,), (1,)), ((), ())), preferred_element_type=F32)
        s = s + bias_ref[variant]
        m = jnp.max(s, axis=-1, keepdims=True)
        e = jnp.exp2(s - m)
        l = jnp.sum(e, axis=-1, keepdims=True)
        pv = jnp.dot(e.astype(BF16), vb, preferred_element_type=F32) / l
        o_ref[0, r0:r0 + ATT_GROUP, :] = jnp.where(low_half, pv[:ATT_GROUP], pv[ATT_GROUP:]).astype(BF16)


def _att_call(q, k, v, bias):
    b, s, _ = q.shape
    tq = ATT_TILE
    lb = tq // LEFT
    cur = pl.BlockSpec((1, tq, LANES), lambda bi, hp, i: (bi, i, hp))
    prev = pl.BlockSpec((1, LEFT, LANES), lambda bi, hp, i: (bi, jnp.maximum(i * lb - 1, 0), hp))
    return pl.pallas_call(
        _att_kernel,
        grid=(b, N_HEADS // 2, s // tq),
        in_specs=[cur, prev, cur, prev, cur,
                  pl.BlockSpec((ATT_VARIANTS, 2 * ATT_GROUP, ATT_BAND), lambda bi, hp, i: (0, hp, 0))],
        out_specs=cur,
        out_shape=jax.ShapeDtypeStruct((b, s, ATT_WIDTH), BF16),
        compiler_params=_params("arbitrary", "arbitrary", "arbitrary"),
        name="chunk_attn",
    )(q, k, k, v, v, bias)


def _group_bias(rel_bias):
    qi = jnp.arange(ATT_GROUP)[:, None]
    kj = jnp.arange(ATT_BAND)[None, :]
    first = (qi // CHUNK) * CHUNK
    visible = (kj >= first) & (kj < first + LEFT + CHUNK)
    lo = ATT_GROUP - 1
    assert lo >= MAX_REL and ATT_BAND - 1 >= MAX_REL
    tab = rel_bias.astype(F32)
    by_rel = jnp.concatenate([jnp.repeat(tab[:, :1], lo - MAX_REL, axis=1), tab,
                              jnp.repeat(tab[:, -1:], ATT_BAND - 1 - MAX_REL, axis=1)], axis=1)
    rev = by_rel[:, ::-1] * LOG2_E
    length = lo + ATT_BAND
    period = jnp.concatenate([rev, jnp.zeros((N_HEADS, 1), F32)], axis=1)
    flat = jnp.tile(period, (1, ATT_GROUP))[:, :ATT_GROUP * length]
    bias = flat.reshape(N_HEADS, ATT_GROUP, length)[:, :, lo:lo + ATT_BAND]
    tables = []
    for variant in range(ATT_VARIANTS):
        ok = visible if variant == 0 else visible & (kj >= LEFT - (variant - 1) * ATT_GROUP)
        tables.append(jnp.where(ok[None], bias, NEG_INF))
    return jnp.stack(tables, axis=0).reshape(ATT_VARIANTS, N_HEADS * ATT_GROUP, ATT_BAND)


def _out_kernel(x_ref, pool_ref, att_ref, mod_ref, g_ref, wo_ref, wr_ref, br_ref,
                x1_ref, h2_ref, meta_ref, metat_ref, cnt_ref, carry_ref, wob_ref):
    t = x_ref.shape[1]

    @pl.when(_first_step())
    def _():
        carry_ref[...] = jnp.zeros_like(carry_ref)
        wob_ref[...] = wo_ref[0].astype(BF16)

    mix = (jnp.dot(pool_ref[0], wob_ref[:POOL_WIDTH, :], preferred_element_type=F32)
           + jnp.dot(att_ref[0], wob_ref[POOL_WIDTH:, :], preferred_element_type=F32))
    x1 = x_ref[0] + mod_ref[0, 2:3, :] * mix
    x1_ref[0] = x1
    ms = jnp.mean(x1 * x1, axis=-1, keepdims=True)
    y = x1 * lax.rsqrt(ms + EPS) * g_ref[...]
    h2 = y * (1.0 + mod_ref[0, 4:5, :]) + mod_ref[0, 3:4, :]

    _store_token_tiles(h2_ref, h2)
    hb = h2.astype(BF16)

    h_lo = (h2 - hb.astype(F32)).astype(BF16)
    r = (jnp.dot(hb, wr_ref[...], preferred_element_type=F32)
         + jnp.dot(h_lo, wr_ref[...], preferred_element_type=F32))
    logits = r[:, :LANES] + r[:, LANES:] + br_ref[...]

    lane = lax.broadcasted_iota(jnp.int32, (t, LANES), 1).astype(F32)
    ninf = jnp.float32(-jnp.inf)
    gmask = lane < N_GROUPS
    gl = jnp.where(gmask, logits, ninf)
    gmax = jnp.max(gl, axis=-1, keepdims=True)
    gsum = jnp.sum(jnp.where(gmask, jnp.exp(logits - gmax), 0.0), axis=-1, keepdims=True)
    g_p = 1.0 / gsum
    g_idx = jnp.min(jnp.where(gl == gmax, lane, float(LANES)), axis=-1, keepdims=True)
    e_lo = ROUTER_EXPERT_LANE + EXPERTS_PER_GROUP * g_idx
    el = jnp.where((lane >= e_lo) & (lane < e_lo + EXPERTS_PER_GROUP), logits, ninf)
    m1 = jnp.max(el, axis=-1, keepdims=True)
    i1 = jnp.min(jnp.where(el == m1, lane, float(LANES)), axis=-1, keepdims=True)
    el2 = jnp.where(lane == i1, ninf, el)
    m2 = jnp.max(el2, axis=-1, keepdims=True)
    i2 = jnp.min(jnp.where(el2 == m2, lane, float(LANES)), axis=-1, keepdims=True)
    e2 = jnp.exp(m2 - m1)
    w1 = g_p / (1.0 + e2)
    w2 = g_p * e2 / (1.0 + e2)

    hot = ((lane == i1) | (lane == i2))
    hot_f = hot.astype(F32)
    row = lax.broadcasted_iota(jnp.int32, (t, t), 0)
    colt = lax.broadcasted_iota(jnp.int32, (t, t), 1)
    lower = (colt < row).astype(BF16)
    before = jnp.dot(lower, hot.astype(BF16), preferred_element_type=F32) + carry_ref[0:1, :]
    r1 = jnp.sum(jnp.where(lane == i1, before, 0.0), axis=-1, keepdims=True)
    r2 = jnp.sum(jnp.where(lane == i2, before, 0.0), axis=-1, keepdims=True)
    carry_ref[0:1, :] = carry_ref[0:1, :] + jnp.sum(hot_f, axis=0, keepdims=True)
    cnt_ref[...] = jnp.broadcast_to(carry_ref[0:1, :], cnt_ref.shape)

    meta = jnp.where(lane == META_E0, i1 - ROUTER_EXPERT_LANE, 0.0)
    meta = jnp.where(lane == META_E1, i2 - ROUTER_EXPERT_LANE, meta)
    meta = jnp.where(lane == META_W0, w1, meta)
    meta = jnp.where(lane == META_W1, w2, meta)
    meta = jnp.where(lane == META_R0, r1, meta)
    meta = jnp.where(lane == META_R1, r2, meta)
    meta_ref[0] = meta
    metat_ref[...] = jnp.transpose(meta)[:8, :]


def _out_call(x, pool, att, mod, g, w_out, layer, wr, br):
    b, s, d = x.shape
    t = SEQ_TILE
    steps = s // t
    tok = lambda w: pl.BlockSpec((1, t, w), lambda bi, j: (bi, j, 0))
    const2 = lambda shape: pl.BlockSpec(shape, lambda bi, j: (0, 0))
    return pl.pallas_call(
        _out_kernel,
        grid=(b, steps),
        in_specs=[tok(d), tok(POOL_WIDTH), tok(ATT_WIDTH),
                  pl.BlockSpec((1, 6, d), lambda bi, j: (bi, 0, 0)),
                  const2((1, d)), pl.BlockSpec((1, d, d), lambda bi, j: (layer, 0, 0)),
                  const2((d, 2 * LANES)), const2((1, LANES))],
        out_specs=[tok(d), pl.BlockSpec((t * TOKEN_SLABS, LANES), lambda bi, j: (bi * steps + j, 0)),
                   tok(LANES), pl.BlockSpec((8, t), lambda bi, j: (0, bi * steps + j)), const2((8, LANES))],
        out_shape=[jax.ShapeDtypeStruct((b, s, d), F32),
                   jax.ShapeDtypeStruct((b * s * TOKEN_SLABS, LANES), F32),
                   jax.ShapeDtypeStruct((b, s, LANES), F32),
                   jax.ShapeDtypeStruct((8, b * s), F32),
                   jax.ShapeDtypeStruct((8, LANES), F32)],
        scratch_shapes=[pltpu.VMEM((8, LANES), F32), pltpu.VMEM((d, d), BF16)],
        compiler_params=_params("arbitrary", "arbitrary"),
        name="out_proj_router",
    )(x, pool, att, mod, g, w_out, wr, br)


def _disp_kernel(zrow_ref, d0_ref, d1_ref, h_ref, xs_ref, zbuf_ref, sems):
    rows = h_ref.shape[0]

    @pl.when(pl.program_id(0) == 0)
    def _():
        zbuf_ref[...] = jnp.zeros_like(zbuf_ref)

        def clear(row):
            return pltpu.make_async_copy(zbuf_ref, xs_ref.at[pl.ds(row, EXPERT_ROWS)], sems.at[2])

        def each_clear(act):
            for e in range(N_EXPERTS):
                @pl.when(zrow_ref[e] >= 0)
                def _():
                    act(clear(zrow_ref[e]))

            def tail(blk, carry):
                act(clear(pl.multiple_of(blk * EXPERT_ROWS, EXPERT_ROWS)))
                return carry
            lax.fori_loop(zrow_ref[N_EXPERTS], xs_ref.shape[0] // EXPERT_ROWS, tail, 0)

        each_clear(lambda cp: cp.start())
        each_clear(lambda cp: cp.wait())

    def issue(blk, carry):
        for u in range(ROW_UNROLL):
            r = blk * ROW_UNROLL + u
            pltpu.make_async_copy(h_ref.at[r], xs_ref.at[d0_ref[0, 0, r]], sems.at[0]).start(priority=0)
            pltpu.make_async_copy(h_ref.at[r], xs_ref.at[d1_ref[0, 0, r]], sems.at[1]).start(priority=1)
        return carry
    lax.fori_loop(0, rows // ROW_UNROLL, issue, 0)
    pltpu.make_async_copy(h_ref, xs_ref.at[pl.ds(0, rows)], sems.at[0]).wait()
    pltpu.make_async_copy(h_ref, xs_ref.at[pl.ds(0, rows)], sems.at[1]).wait()


def _disp_call(h2t, d0, d1, zrow, p_rows):
    n = h2t.shape[0]
    t = ROW_TILE
    idx = pl.BlockSpec((1, 1, t), lambda i, z: (i, 0, 0), memory_space=pltpu.SMEM)
    tile = (TOKEN_SLABS, LANES)
    grid_spec = pltpu.PrefetchScalarGridSpec(
        num_scalar_prefetch=1,
        grid=(n // t,),
        in_specs=[idx, idx, pl.BlockSpec((t,) + tile, lambda i, z: (i, 0, 0))],
        out_specs=pl.BlockSpec(memory_space=pl.ANY),
        scratch_shapes=[pltpu.VMEM((EXPERT_ROWS,) + tile, F32), pltpu.SemaphoreType.DMA((3,))],
    )
    return pl.pallas_call(
        _disp_kernel,
        grid_spec=grid_spec,
        out_shape=jax.ShapeDtypeStruct((p_rows,) + tile, F32),
        compiler_params=_params("arbitrary"),
        name="dispatch_rows",
    )(zrow, d0.reshape(n // t, 1, t), d1.reshape(n // t, 1, t), h2t)


def _exp_kernel(be_ref, nu_ref, xs_ref, wg_ref, wu_ref, wd_ref, y_ref, wgb_ref, wub_ref, wdb_ref):
    i = pl.program_id(0)
    rows = xs_ref.shape[0] // TOKEN_SLABS
    used = i < nu_ref[0]
    new_expert = jnp.logical_or(i == 0, be_ref[i] != be_ref[jnp.maximum(i - 1, 0)])

    @pl.when(jnp.logical_and(used, new_expert))
    def _():
        wgb_ref[...] = wg_ref[0, 0].astype(BF16)
        wub_ref[...] = wu_ref[0, 0].astype(BF16)
        wdb_ref[...] = wd_ref[0, 0].astype(BF16)

    @pl.when(used)
    def _():
        xb = _load_token_tiles(xs_ref, rows).astype(BF16)
        gate = jnp.dot(xb, wgb_ref[...], preferred_element_type=F32)
        up = jnp.dot(xb, wub_ref[...], preferred_element_type=F32)
        mid = (jax.nn.silu(gate) * up).astype(BF16)
        _store_token_tiles(y_ref, jnp.dot(mid, wdb_ref[...], preferred_element_type=F32))

    @pl.when(jnp.logical_not(used))
    def _():
        y_ref[...] = jnp.zeros_like(y_ref)


def _exp_call(blk_e, n_used, xs, wg, wu, wd, layer):
    p = xs.shape[0] // TOKEN_SLABS
    r = EXPERT_ROWS
    blk = (r * TOKEN_SLABS, LANES)
    wspec = lambda shape: pl.BlockSpec((1, 1) + shape, lambda i, be, nu: (layer, be[i], 0, 0))
    grid_spec = pltpu.PrefetchScalarGridSpec(
        num_scalar_prefetch=2,
        grid=(p // r,),
        in_specs=[pl.BlockSpec(blk, lambda i, be, nu: (jnp.minimum(i, nu[0] - 1), 0)),
                  wspec((D_MODEL, D_EXPERT)), wspec((D_MODEL, D_EXPERT)), wspec((D_EXPERT, D_MODEL))],
        out_specs=pl.BlockSpec(blk, lambda i, be, nu: (i, 0)),
        scratch_shapes=[pltpu.VMEM((D_MODEL, D_EXPERT), BF16), pltpu.VMEM((D_MODEL, D_EXPERT), BF16),
                        pltpu.VMEM((D_EXPERT, D_MODEL), BF16)],
    )
    return pl.pallas_call(
        _exp_kernel,
        grid_spec=grid_spec,
        out_shape=jax.ShapeDtypeStruct(xs.shape, F32),
        compiler_params=_params("arbitrary"),
        name="experts",
    )(blk_e, n_used, xs, wg, wu, wd)


def _gather_scratch(t):
    return [pltpu.VMEM((2, 2, t * TOKEN_SLABS, LANES), F32), pltpu.SemaphoreType.DMA((2, 2))]


def _moe_residual(d0c_ref, d1c_ref, d0n_ref, d1n_ref, x_ref, meta_ref, mod_ref, ys_ref, ybuf_ref, sems):
    rows = x_ref.shape[1]
    n = pl.program_id(0) * pl.num_programs(1) + pl.program_id(1)
    last = pl.num_programs(0) * pl.num_programs(1) - 1

    def copies(i0_ref, i1_ref, slot, r, vmem_rows):
        return (pltpu.make_async_copy(ys_ref.at[i0_ref[0, 0, r]], ybuf_ref.at[slot, 0, vmem_rows], sems.at[slot, 0]),
                pltpu.make_async_copy(ys_ref.at[i1_ref[0, 0, r]], ybuf_ref.at[slot, 1, vmem_rows], sems.at[slot, 1]))

    def wait(slot):
        for k in range(2):
            buf = ybuf_ref.at[slot, k]
            pltpu.make_async_copy(buf, buf, sems.at[slot, k]).wait()

    @pl.when(n == 0)
    def _():
        def issue(r, carry):
            c0, c1 = copies(d0c_ref, d1c_ref, 0, r, pl.ds(pl.multiple_of(r * TOKEN_SLABS, TOKEN_SLABS), TOKEN_SLABS))
            c0.start(priority=0)
            c1.start(priority=1)
            return carry
        lax.fori_loop(0, rows, issue, 0)

    nxt = (n + 1) % 2
    for r in range(rows):
        c0, c1 = copies(d0n_ref, d1n_ref, nxt, r, pl.ds(r * TOKEN_SLABS, TOKEN_SLABS))
        c0.start(priority=0)
        c1.start(priority=1)

    slot = n % 2
    wait(slot)

    @pl.when(n == last)
    def _():
        wait(nxt)

    w0 = meta_ref[0, :, META_W0:META_W0 + 1]
    w1 = meta_ref[0, :, META_W1:META_W1 + 1]
    moe = _load_token_tiles(ybuf_ref.at[slot, 0], rows) * w0 + _load_token_tiles(ybuf_ref.at[slot, 1], rows) * w1
    return x_ref[0] + mod_ref[0, 5:6, :] * moe


def _row_index_specs(d0, d1, b, steps, t):
    last = b * steps - 1
    cur = pl.BlockSpec((1, 1, t), lambda bi, j: (bi * steps + j, 0, 0), memory_space=pltpu.SMEM)
    nxt = pl.BlockSpec((1, 1, t), lambda bi, j: (jnp.minimum(bi * steps + j + 1, last), 0, 0),
                       memory_space=pltpu.SMEM)
    d0 = d0.reshape(b * steps, 1, t)
    d1 = d1.reshape(b * steps, 1, t)
    return [cur, cur, nxt, nxt], (d0, d1, d0, d1)


def _comb_kernel(d0c_ref, d1c_ref, d0n_ref, d1n_ref, x_ref, meta_ref, mod_ref, ys_ref, o_ref, ybuf_ref, sems):
    o_ref[0] = _moe_residual(d0c_ref, d1c_ref, d0n_ref, d1n_ref, x_ref, meta_ref, mod_ref, ys_ref,
                             ybuf_ref, sems)


def _comb_call(x1, meta, mod, ys, d0, d1):
    b, s, d = x1.shape
    t = SEQ_TILE
    steps = s // t
    idx_specs, idx_args = _row_index_specs(d0, d1, b, steps, t)
    tok = lambda w: pl.BlockSpec((1, t, w), lambda bi, j: (bi, j, 0))
    return pl.pallas_call(
        _comb_kernel,
        grid=(b, steps),
        in_specs=idx_specs + [tok(d), tok(LANES),
                              pl.BlockSpec((1, 6, d), lambda bi, j: (bi, 0, 0)),
                              pl.BlockSpec(memory_space=pl.ANY)],
        out_specs=tok(d),
        out_shape=jax.ShapeDtypeStruct((b, s, d), F32),
        scratch_shapes=_gather_scratch(t),
        compiler_params=_params("arbitrary", "arbitrary"),
        name="combine_rows",
    )(*idx_args, x1, meta, mod, ys)


def _dispatch_plan(meta_t, counts, n_rows):
    r = EXPERT_ROWS
    e0 = meta_t[META_E0].astype(jnp.int32)
    e1 = meta_t[META_E1].astype(jnp.int32)
    cnt = counts[0, ROUTER_EXPERT_LANE:ROUTER_EXPERT_LANE + N_EXPERTS].astype(jnp.int32)
    padded = ((cnt + r - 1) // r) * r
    pad_end = jnp.cumsum(padded)
    pad_start = pad_end - padded
    experts = jnp.arange(N_EXPERTS, dtype=jnp.int32)[None, :]
    start_of = lambda e: jnp.sum(jnp.where(e[:, None] == experts, pad_start[None, :], 0), axis=1)
    d0 = start_of(e0) + meta_t[META_R0].astype(jnp.int32)
    d1 = start_of(e1) + meta_t[META_R1].astype(jnp.int32)
    n_used = (pad_end[-1:] // r).astype(jnp.int32)
    blk_row = jnp.minimum(jnp.arange(n_rows // r, dtype=jnp.int32), n_used - 1) * r
    blk_e = jnp.sum((pad_end[None, :] <= blk_row[:, None]).astype(jnp.int32), axis=1)
    zrow = jnp.concatenate([jnp.where(padded > 0, pad_end - r, -1).astype(jnp.int32), n_used])
    return d0, d1, blk_e, n_used, zrow


def kernel(x, c, ada_w, ada_b, norm1_g, norm2_g, w_in, pool_w, pool_scale, q_norm_g, k_norm_g, rel_bias,
           w_out, router_group_w, router_group_b, router_expert_w, router_expert_b,
           moe_w_gate, moe_w_up, moe_w_down):
    b, s, d = x.shape
    depth = ada_w.shape[0]
    n = b * s
    assert d == D_MODEL and b <= 8 and s % SEQ_TILE == 0 and s % ATT_TILE == 0 and n % ROW_TILE == 0
    p_rows = n * 2 + N_EXPERTS * EXPERT_ROWS
    tile = (TOKEN_SLABS, LANES)

    c_pad = jnp.zeros((8, d), F32).at[:b].set(c)
    mod_all = _ada_call(c_pad, ada_w, ada_b)
    bias = _group_bias(rel_bias)
    head_block = jnp.arange(ATT_WIDTH) // HEAD_DIM
    bd = (head_block[:, None] == head_block[None, :]).astype(BF16)
    sm_scale = HEAD_DIM ** -0.5 * LOG2_E

    moe = None
    for l in range(depth):
        mod = mod_all[l, :b].reshape(b, 6, d)
        qg = (jnp.tile(q_norm_g[l], N_HEADS) * sm_scale).reshape(1, ATT_WIDTH)
        kg = jnp.tile(k_norm_g[l], N_HEADS).reshape(1, ATT_WIDTH)
        outs = _in_call(x, mod, norm1_g[l].reshape(1, d), w_in, pool_w, l,
                        pool_scale[l].reshape(1, POOL_WIDTH), qg, kg, bd, moe)
        if moe is not None:
            x, outs = outs[0], outs[1:]
        pool, q, k, v = outs
        att = _att_call(q, k, v, bias)

        wr = jnp.zeros((d, LANES), F32)
        wr = wr.at[:, ROUTER_GROUP_LANE:ROUTER_GROUP_LANE + N_GROUPS].set(router_group_w[l])
        wr = wr.at[:, ROUTER_EXPERT_LANE:ROUTER_EXPERT_LANE + N_EXPERTS].set(router_expert_w[l])
        wr_hi = wr.astype(BF16)
        wr_lo = (wr - wr_hi.astype(F32)).astype(BF16)
        br = jnp.zeros((1, LANES), F32)
        br = br.at[0, ROUTER_GROUP_LANE:ROUTER_GROUP_LANE + N_GROUPS].set(router_group_b[l])
        br = br.at[0, ROUTER_EXPERT_LANE:ROUTER_EXPERT_LANE + N_EXPERTS].set(router_expert_b[l])
        x1, h2t, meta, meta_t, counts = _out_call(x, pool, att, mod, norm2_g[l].reshape(1, d), w_out, l,
                                                  jnp.concatenate([wr_hi, wr_lo], axis=1), br)

        d0, d1, blk_e, n_used, zrow = _dispatch_plan(meta_t, counts, p_rows)
        xs = _disp_call(h2t.reshape((n,) + tile), d0, d1, zrow, p_rows)
        ys = _exp_call(blk_e, n_used, xs.reshape(p_rows * TOKEN_SLABS, LANES),
                       moe_w_gate, moe_w_up, moe_w_down, l)
        x = x1
        moe = (meta, mod, ys.reshape((p_rows,) + tile), d0, d1)
    return _comb_call(x, *moe)
```

```python
import jax
import jax.numpy as jnp
from jax import lax
from jax.experimental import pallas as pl
from jax.experimental.pallas import tpu as pltpu

F32 = jnp.float32
BF16 = jnp.bfloat16

D_MODEL = 1024
CHUNK = 64
POOL_WIDTH = 512
POOL_WINDOWS = (2, 4, 8, 16)
POOL_GW = 128
POOL_HALO = 16
ATT_WIDTH = 512
N_HEADS = 8
HEAD_DIM = 64
LEFT = 512
MAX_REL = 128
IN_WIDTH = 2048
N_GROUPS = 4
EXPERTS_PER_GROUP = 8
N_EXPERTS = 32
D_EXPERT = 512
EPS = 1e-6
NEG_INF = -1e30

LANES = 128
TOKEN_SLABS = D_MODEL // LANES
SEQ_TILE = 512
ATT_TILE = 1024
ATT_GROUP = 4 * CHUNK
ATT_BAND = LEFT + ATT_GROUP
ATT_VARIANTS = LEFT // ATT_GROUP + 1
LOG2_E = 1.4426950408889634
EXPERT_ROWS = 512
RUN_CHUNK = 16
MAX_CHUNKS = 2 * SEQ_TILE // RUN_CHUNK + N_EXPERTS
STAGE_ROWS = MAX_CHUNKS * RUN_CHUNK
assert MAX_CHUNKS < LANES
VMEM_LIMIT = 56 * 1024 * 1024

META_E0, META_E1, META_W0, META_W1, META_R0, META_R1 = 0, 1, 2, 3, 4, 5
ROUTER_GROUP_LANE = 0
ROUTER_EXPERT_LANE = N_GROUPS


def _params(*sem):
    return pltpu.CompilerParams(dimension_semantics=sem, vmem_limit_bytes=VMEM_LIMIT)


def _first_step():
    return jnp.logical_and(pl.program_id(0) == 0, pl.program_id(1) == 0)


def _load_token_tiles(ref, rows):
    return jnp.concatenate([ref[pl.ds(s, rows, stride=TOKEN_SLABS), :] for s in range(TOKEN_SLABS)], axis=1)


def _store_token_tiles(ref, val):
    rows = val.shape[0]
    for s in range(TOKEN_SLABS):
        ref[pl.ds(s, rows, stride=TOKEN_SLABS), :] = val[:, s * LANES:(s + 1) * LANES]


def _ada_kernel(c_ref, w_ref, b_ref, o_ref):
    ca = jax.nn.silu(c_ref[...])
    o_ref[0] = jnp.dot(ca, w_ref[0], precision=lax.Precision.HIGHEST,
                       preferred_element_type=F32) + b_ref[0]


def _ada_call(c_pad, ada_w, ada_b):
    depth = ada_w.shape[0]
    tn = 1024
    return pl.pallas_call(
        _ada_kernel,
        grid=(depth, 6 * D_MODEL // tn),
        in_specs=[
            pl.BlockSpec((8, D_MODEL), lambda l, n: (0, 0)),
            pl.BlockSpec((1, D_MODEL, tn), lambda l, n: (l, 0, n)),
            pl.BlockSpec((1, 1, tn), lambda l, n: (l, 0, n)),
        ],
        out_specs=pl.BlockSpec((1, 8, tn), lambda l, n: (l, 0, n)),
        out_shape=jax.ShapeDtypeStruct((depth, 8, 6 * D_MODEL), F32),
        compiler_params=_params("arbitrary", "arbitrary"),
        name="ada_mod",
    )(c_pad, ada_w, ada_b.reshape(depth, 1, 6 * D_MODEL))


def _in_kernel(x_ref, *refs):
    _in_body(x_ref[0], *refs)


def _in_comb_kernel(tabc_ref, tabn_ref, p0_ref, p1_ref, w0_ref, w1_ref, x1_ref, modp_ref, ys_ref,
                    mod_ref, g_ref, w_ref, pw_ref, ps_ref, qg_ref, kg_ref, bd_ref,
                    xn_ref, pool_ref, q_ref, k_ref, v_ref, ext_ref, wb_ref, stage_ref, moe_ref, sems):
    x = _moe_residual(tabc_ref, tabn_ref, p0_ref, p1_ref, w0_ref, w1_ref, x1_ref, modp_ref, ys_ref,
                      stage_ref, moe_ref, sems)
    xn_ref[0] = x
    _in_body(x, mod_ref, g_ref, w_ref, pw_ref, ps_ref, qg_ref, kg_ref, bd_ref,
             pool_ref, q_ref, k_ref, v_ref, ext_ref, wb_ref)


def _in_body(x, mod_ref, g_ref, w_ref, pw_ref, ps_ref, qg_ref, kg_ref, bd_ref,
             pool_ref, q_ref, k_ref, v_ref, ext_ref, wb_ref):
    j = pl.program_id(1)
    t = x.shape[0]

    @pl.when(_first_step())
    def _():
        wb_ref[...] = w_ref[0].astype(BF16)

    ms = jnp.mean(x * x, axis=-1, keepdims=True)
    y = x * lax.rsqrt(ms + EPS) * g_ref[...]
    h = y * (1.0 + mod_ref[0, 1:2, :]) + mod_ref[0, 0:1, :]
    z = jnp.dot(h.astype(BF16), wb_ref[...], preferred_element_type=F32)

    bd = bd_ref[...]
    q = z[:, POOL_WIDTH:POOL_WIDTH + ATT_WIDTH]
    k = z[:, POOL_WIDTH + ATT_WIDTH:POOL_WIDTH + 2 * ATT_WIDTH]
    qss = jnp.dot((q * q).astype(BF16), bd, preferred_element_type=F32)
    kss = jnp.dot((k * k).astype(BF16), bd, preferred_element_type=F32)
    q_ref[0] = (q * lax.rsqrt(qss * (1.0 / HEAD_DIM) + EPS) * qg_ref[...]).astype(BF16)
    k_ref[0] = (k * lax.rsqrt(kss * (1.0 / HEAD_DIM) + EPS) * kg_ref[...]).astype(BF16)
    v_ref[0] = z[:, POOL_WIDTH + 2 * ATT_WIDTH:].astype(BF16)

    @pl.when(j == 0)
    def _():
        ext_ref[0:POOL_HALO, :] = jnp.zeros((POOL_HALO, POOL_WIDTH), F32)

    ext_ref[POOL_HALO:, :] = z[:, :POOL_WIDTH]
    pos = j * t + lax.broadcasted_iota(jnp.int32, (t, 1), 0)
    outs = []
    for gi, win in enumerate(POOL_WINDOWS):
        a = ext_ref[:, gi * POOL_GW:(gi + 1) * POOL_GW]
        s = a
        shift = 1
        while shift < win:
            s = s + pltpu.roll(s, shift, 0)
            shift *= 2
        cnt = jnp.minimum(pos + 1, win).astype(F32)
        pooled = s[POOL_HALO:, :] / cnt - a[POOL_HALO:, :]
        outs.append(jnp.dot(pooled.astype(BF16), pw_ref[0, gi].astype(BF16), preferred_element_type=F32))
    pool_ref[0] = (jnp.concatenate(outs, axis=1) * ps_ref[...]).astype(BF16)
    ext_ref[0:POOL_HALO, :] = ext_ref[t:t + POOL_HALO, :]


def _in_call(x, mod, g, w_in, pool_w, layer, pool_scale, qg, kg, bd, moe=None):
    b, s, d = x.shape
    t = SEQ_TILE
    steps = s // t
    tok = lambda w: pl.BlockSpec((1, t, w), lambda bi, j: (bi, j, 0))
    const2 = lambda shape: pl.BlockSpec(shape, lambda bi, j: (0, 0))
    mod_spec = pl.BlockSpec((1, 6, d), lambda bi, j: (bi, 0, 0))
    out = jax.ShapeDtypeStruct((b, s, ATT_WIDTH), BF16)
    in_specs = [
        mod_spec,
        const2((1, d)),
        pl.BlockSpec((1, d, IN_WIDTH), lambda bi, j: (layer, 0, 0), pipeline_mode=pl.Buffered(1)),
        pl.BlockSpec((1, len(POOL_WINDOWS), POOL_GW, POOL_GW), lambda bi, j: (layer, 0, 0, 0)),
        const2((1, POOL_WIDTH)),
        const2((1, ATT_WIDTH)),
        const2((1, ATT_WIDTH)),
        const2((ATT_WIDTH, ATT_WIDTH)),
    ]
    args = (mod, g, w_in, pool_w, pool_scale, qg, kg, bd)
    out_specs = [tok(POOL_WIDTH), tok(ATT_WIDTH), tok(ATT_WIDTH), tok(ATT_WIDTH)]
    out_shape = [jax.ShapeDtypeStruct((b, s, POOL_WIDTH), BF16), out, out, out]
    scratch = [pltpu.VMEM((t + POOL_HALO, POOL_WIDTH), F32), pltpu.VMEM((d, IN_WIDTH), BF16)]
    if moe is None:
        return pl.pallas_call(
            _in_kernel, grid=(b, steps), in_specs=[tok(d)] + in_specs, out_specs=out_specs, out_shape=out_shape,
            scratch_shapes=scratch, compiler_params=_params("arbitrary", "arbitrary"), name="in_proj",
        )(x, *args)
    plan, mod_prev, ys = moe
    plan_specs, plan_args = _combine_plan_specs(plan, b, steps)
    return pl.pallas_call(
        _in_comb_kernel, grid=(b, steps),
        in_specs=plan_specs + [tok(d), mod_spec, pl.BlockSpec(memory_space=pl.ANY)] + in_specs,
        out_specs=[tok(d)] + out_specs,
        out_shape=[jax.ShapeDtypeStruct((b, s, d), F32)] + out_shape,
        scratch_shapes=scratch + _gather_scratch(),
        compiler_params=_params("arbitrary", "arbitrary"), name="combine_in_proj",
    )(*plan_args, x, mod_prev, ys, *args)


def _att_kernel(q_ref, kp_ref, kc_ref, vp_ref, vc_ref, bias_ref, o_ref):
    i = pl.program_id(2)
    tq = q_ref.shape[1]
    lane = lax.broadcasted_iota(jnp.int32, (1, LANES), 1)
    low_half = lane < HEAD_DIM
    for g in range(tq // ATT_GROUP):
        r0 = g * ATT_GROUP
        qg = q_ref[0, r0:r0 + ATT_GROUP, :]
        if r0 < LEFT:
            kb = jnp.concatenate([kp_ref[0, r0:LEFT, :], kc_ref[0, 0:r0 + ATT_GROUP, :]], axis=0)
            vb = jnp.concatenate([vp_ref[0, r0:LEFT, :], vc_ref[0, 0:r0 + ATT_GROUP, :]], axis=0)
            variant = jnp.where(i == 0, g + 1, 0)
        else:
            kb = kc_ref[0, r0 - LEFT:r0 + ATT_GROUP, :]
            vb = vc_ref[0, r0 - LEFT:r0 + ATT_GROUP, :]
            variant = 0
        zero = jnp.zeros_like(qg)
        qs = jnp.concatenate([jnp.where(low_half, qg, zero), jnp.where(low_half, zero, qg)], axis=0)
        s = lax.dot_general(qs, kb, (((1,), (1,)), ((), ())), preferred_element_type=F32)
        s = s + bias_ref[variant]
        m = jnp.max(s, axis=-1, keepdims=True)
        e = jnp.exp2(s - m)
        l = jnp.sum(e, axis=-1, keepdims=True)
        pv = jnp.dot(e.astype(BF16), vb, preferred_element_type=F32) / l
        o_ref[0, r0:r0 + ATT_GROUP, :] = jnp.where(low_half, pv[:ATT_GROUP], pv[ATT_GROUP:]).astype(BF16)


def _att_call(q, k, v, bias):
    b, s, _ = q.shape
    tq = ATT_TILE
    lb = tq // LEFT
    cur = pl.BlockSpec((1, tq, LANES), lambda bi, hp, i: (bi, i, hp))
    prev = pl.BlockSpec((1, LEFT, LANES), lambda bi, hp, i: (bi, jnp.maximum(i * lb - 1, 0), hp))
    return pl.pallas_call(
        _att_kernel,
        grid=(b, N_HEADS // 2, s // tq),
        in_specs=[cur, prev, cur, prev, cur,
                  pl.BlockSpec((ATT_VARIANTS, 2 * ATT_GROUP, ATT_BAND), lambda bi, hp, i: (0, hp, 0))],
        out_specs=cur,
        out_shape=jax.ShapeDtypeStruct((b, s, ATT_WIDTH), BF16),
        compiler_params=_params("arbitrary", "arbitrary", "arbitrary"),
        name="chunk_attn",
    )(q, k, k, v, v, bias)


def _group_bias(rel_bias):
    qi = jnp.arange(ATT_GROUP)[:, None]
    kj = jnp.arange(ATT_BAND)[None, :]
    first = (qi // CHUNK) * CHUNK
    visible = (kj >= first) & (kj < first + LEFT + CHUNK)
    lo = ATT_GROUP - 1
    assert lo >= MAX_REL and ATT_BAND - 1 >= MAX_REL
    tab = rel_bias.astype(F32)
    by_rel = jnp.concatenate([jnp.repeat(tab[:, :1], lo - MAX_REL, axis=1), tab,
                              jnp.repeat(tab[:, -1:], ATT_BAND - 1 - MAX_REL, axis=1)], axis=1)
    rev = by_rel[:, ::-1] * LOG2_E
    length = lo + ATT_BAND
    period = jnp.concatenate([rev, jnp.zeros((N_HEADS, 1), F32)], axis=1)
    flat = jnp.tile(period, (1, ATT_GROUP))[:, :ATT_GROUP * length]
    bias = flat.reshape(N_HEADS, ATT_GROUP, length)[:, :, lo:lo + ATT_BAND]
    tables = []
    for variant in range(ATT_VARIANTS):
        ok = visible if variant == 0 else visible & (kj >= LEFT - (variant - 1) * ATT_GROUP)
        tables.append(jnp.where(ok[None], bias, NEG_INF))
    return jnp.stack(tables, axis=0).reshape(ATT_VARIANTS, N_HEADS * ATT_GROUP, ATT_BAND)


def _out_kernel(x_ref, pool_ref, att_ref, mod_ref, g_ref, wo_ref, wr_ref, br_ref,
                x1_ref, h2_ref, metat_ref, start_ref, cnt_ref, carry_ref, wob_ref):
    t = x_ref.shape[1]

    @pl.when(_first_step())
    def _():
        carry_ref[...] = jnp.zeros_like(carry_ref)
        wob_ref[...] = wo_ref[0].astype(BF16)

    start_ref[...] = carry_ref[...]

    mix = (jnp.dot(pool_ref[0], wob_ref[:POOL_WIDTH, :], preferred_element_type=F32)
           + jnp.dot(att_ref[0], wob_ref[POOL_WIDTH:, :], preferred_element_type=F32))
    x1 = x_ref[0] + mod_ref[0, 2:3, :] * mix
    x1_ref[0] = x1
    ms = jnp.mean(x1 * x1, axis=-1, keepdims=True)
    y = x1 * lax.rsqrt(ms + EPS) * g_ref[...]
    h2 = y * (1.0 + mod_ref[0, 4:5, :]) + mod_ref[0, 3:4, :]

    _store_token_tiles(h2_ref, h2)
    hb = h2.astype(BF16)

    h_lo = (h2 - hb.astype(F32)).astype(BF16)
    r = (jnp.dot(hb, wr_ref[...], preferred_element_type=F32)
         + jnp.dot(h_lo, wr_ref[...], preferred_element_type=F32))
    logits = r[:, :LANES] + r[:, LANES:] + br_ref[...]

    lane = lax.broadcasted_iota(jnp.int32, (t, LANES), 1).astype(F32)
    ninf = jnp.float32(-jnp.inf)
    gmask = lane < N_GROUPS
    gl = jnp.where(gmask, logits, ninf)
    gmax = jnp.max(gl, axis=-1, keepdims=True)
    gsum = jnp.sum(jnp.where(gmask, jnp.exp(logits - gmax), 0.0), axis=-1, keepdims=True)
    g_p = 1.0 / gsum
    g_idx = jnp.min(jnp.where(gl == gmax, lane, float(LANES)), axis=-1, keepdims=True)
    e_lo = ROUTER_EXPERT_LANE + EXPERTS_PER_GROUP * g_idx
    el = jnp.where((lane >= e_lo) & (lane < e_lo + EXPERTS_PER_GROUP), logits, ninf)
    m1 = jnp.max(el, axis=-1, keepdims=True)
    i1 = jnp.min(jnp.where(el == m1, lane, float(LANES)), axis=-1, keepdims=True)
    el2 = jnp.where(lane == i1, ninf, el)
    m2 = jnp.max(el2, axis=-1, keepdims=True)
    i2 = jnp.min(jnp.where(el2 == m2, lane, float(LANES)), axis=-1, keepdims=True)
    e2 = jnp.exp(m2 - m1)
    w1 = g_p / (1.0 + e2)
    w2 = g_p * e2 / (1.0 + e2)

    hot = ((lane == i1) | (lane == i2))
    hot_f = hot.astype(F32)
    row = lax.broadcasted_iota(jnp.int32, (t, t), 0)
    colt = lax.broadcasted_iota(jnp.int32, (t, t), 1)
    lower = (colt < row).astype(BF16)
    before = jnp.dot(lower, hot.astype(BF16), preferred_element_type=F32) + carry_ref[0:1, :]
    r1 = jnp.sum(jnp.where(lane == i1, before, 0.0), axis=-1, keepdims=True)
    r2 = jnp.sum(jnp.where(lane == i2, before, 0.0), axis=-1, keepdims=True)
    carry_ref[0:1, :] = carry_ref[0:1, :] + jnp.sum(hot_f, axis=0, keepdims=True)
    cnt_ref[...] = jnp.broadcast_to(carry_ref[0:1, :], cnt_ref.shape)

    meta = jnp.where(lane == META_E0, i1 - ROUTER_EXPERT_LANE, 0.0)
    meta = jnp.where(lane == META_E1, i2 - ROUTER_EXPERT_LANE, meta)
    meta = jnp.where(lane == META_W0, w1, meta)
    meta = jnp.where(lane == META_W1, w2, meta)
    meta = jnp.where(lane == META_R0, r1, meta)
    meta = jnp.where(lane == META_R1, r2, meta)
    metat_ref[...] = jnp.transpose(meta)[:8, :]


def _out_call(x, pool, att, mod, g, w_out, layer, wr, br):
    b, s, d = x.shape
    t = SEQ_TILE
    steps = s // t
    tok = lambda w: pl.BlockSpec((1, t, w), lambda bi, j: (bi, j, 0))
    const2 = lambda shape: pl.BlockSpec(shape, lambda bi, j: (0, 0))
    return pl.pallas_call(
        _out_kernel,
        grid=(b, steps),
        in_specs=[tok(d), tok(POOL_WIDTH), tok(ATT_WIDTH),
                  pl.BlockSpec((1, 6, d), lambda bi, j: (bi, 0, 0)),
                  const2((1, d)),
                  pl.BlockSpec((1, d, d), lambda bi, j: (layer, 0, 0), pipeline_mode=pl.Buffered(1)),
                  const2((d, 2 * LANES)), const2((1, LANES))],
        out_specs=[tok(d), pl.BlockSpec((t * TOKEN_SLABS, LANES), lambda bi, j: (bi * steps + j, 0)),
                   pl.BlockSpec((8, t), lambda bi, j: (0, bi * steps + j)),
                   pl.BlockSpec((8, LANES), lambda bi, j: (bi * steps + j, 0)), const2((8, LANES))],
        out_shape=[jax.ShapeDtypeStruct((b, s, d), F32),
                   jax.ShapeDtypeStruct((b * s * TOKEN_SLABS, LANES), F32),
                   jax.ShapeDtypeStruct((8, b * s), F32),
                   jax.ShapeDtypeStruct((b * steps * 8, LANES), F32),
                   jax.ShapeDtypeStruct((8, LANES), F32)],
        scratch_shapes=[pltpu.VMEM((8, LANES), F32), pltpu.VMEM((d, d), BF16)],
        compiler_params=_params("arbitrary", "arbitrary"),
        name="out_proj_router",
    )(x, pool, att, mod, g, w_out, wr, br)


def _tile_rows(first_tile, n_tiles):
    return pl.ds(pl.multiple_of(first_tile * TOKEN_SLABS, TOKEN_SLABS), n_tiles * TOKEN_SLABS)


def _for_each_run_chunk(tab_ref, fn):
    def piece(k, carry):
        fn(tab_ref[0, 0, k], k * RUN_CHUNK)
        return carry
    lax.fori_loop(0, tab_ref[0, 0, MAX_CHUNKS], piece, 0)


def _disp_kernel(zrow_ref, tabc_ref, tabp_ref, p0_ref, p1_ref, h_ref, xs_ref, stage_ref, zbuf_ref, sems):
    j = pl.program_id(0)
    rows = h_ref.shape[0] // TOKEN_SLABS
    slot = j % 2

    @pl.when(j == 0)
    def _():
        stage_ref[...] = jnp.zeros_like(stage_ref)
        zbuf_ref[...] = jnp.zeros_like(zbuf_ref)

        def clear(blk_tile):
            return pltpu.make_async_copy(zbuf_ref, xs_ref.at[_tile_rows(blk_tile, EXPERT_ROWS)], sems.at[1])

        def each_clear(act):
            for e in range(2 * N_EXPERTS):
                @pl.when(zrow_ref[e] >= 0)
                def _():
                    act(clear(zrow_ref[e]))

            def tail(blk, carry):
                act(clear(blk * EXPERT_ROWS))
                return carry
            lax.fori_loop(zrow_ref[2 * N_EXPERTS], xs_ref.shape[0] // (EXPERT_ROWS * TOKEN_SLABS), tail, 0)

        each_clear(lambda cp: cp.start())
        each_clear(lambda cp: cp.wait())

    for r in range(rows):
        tile = h_ref[pl.ds(r * TOKEN_SLABS, TOKEN_SLABS), :]
        stage_ref[slot, _tile_rows(p0_ref[0, 0, r], 1), :] = tile
        stage_ref[slot, _tile_rows(p1_ref[0, 0, r], 1), :] = tile

    def copy(hbm_tile, stage_tile, s):
        return pltpu.make_async_copy(stage_ref.at[s, _tile_rows(stage_tile, RUN_CHUNK)],
                                     xs_ref.at[_tile_rows(hbm_tile, RUN_CHUNK)], sems.at[0])

    @pl.when(j > 0)
    def _():
        _for_each_run_chunk(tabp_ref, lambda h, s: copy(h, s, 1 - slot).wait())

    _for_each_run_chunk(tabc_ref, lambda h, s: copy(h, s, slot).start())

    @pl.when(j == pl.num_programs(0) - 1)
    def _():
        _for_each_run_chunk(tabc_ref, lambda h, s: copy(h, s, slot).wait())


def _disp_call(h2t, plan, zrow, p_rows):
    t = SEQ_TILE
    n_tiles = h2t.shape[0] // (t * TOKEN_SLABS)
    smem = lambda w, imap: pl.BlockSpec((1, 1, w), imap, memory_space=pltpu.SMEM)
    cur = lambda i, z: (i, 0, 0)
    grid_spec = pltpu.PrefetchScalarGridSpec(
        num_scalar_prefetch=1,
        grid=(n_tiles,),
        in_specs=[smem(LANES, cur), smem(LANES, lambda i, z: (jnp.maximum(i - 1, 0), 0, 0)),
                  smem(t, cur), smem(t, cur),
                  pl.BlockSpec((t * TOKEN_SLABS, LANES), lambda i, z: (i, 0))],
        out_specs=pl.BlockSpec(memory_space=pl.ANY),
        scratch_shapes=[pltpu.VMEM((2, STAGE_ROWS * TOKEN_SLABS, LANES), F32),
                        pltpu.VMEM((EXPERT_ROWS * TOKEN_SLABS, LANES), F32),
                        pltpu.SemaphoreType.DMA((2,))],
    )
    return pl.pallas_call(
        _disp_kernel,
        grid_spec=grid_spec,
        out_shape=jax.ShapeDtypeStruct((p_rows * TOKEN_SLABS, LANES), F32),
        compiler_params=_params("arbitrary"),
        name="dispatch_rows",
    )(zrow, plan["table"], plan["table"], plan["pos0"], plan["pos1"], h2t)


def _exp_kernel(be_ref, nu_ref, xs_ref, wg_ref, wu_ref, wd_ref, y_ref, wgb_ref, wub_ref, wdb_ref):
    i = pl.program_id(0)
    rows = xs_ref.shape[0] // TOKEN_SLABS
    used = i < nu_ref[0]
    new_expert = jnp.logical_or(i == 0, be_ref[i] != be_ref[jnp.maximum(i - 1, 0)])

    @pl.when(jnp.logical_and(used, new_expert))
    def _():
        wgb_ref[...] = wg_ref[0, 0].astype(BF16)
        wub_ref[...] = wu_ref[0, 0].astype(BF16)
        wdb_ref[...] = wd_ref[0, 0].astype(BF16)

    @pl.when(used)
    def _():
        xb = _load_token_tiles(xs_ref, rows).astype(BF16)
        gate = jnp.dot(xb, wgb_ref[...], preferred_element_type=F32)
        up = jnp.dot(xb, wub_ref[...], preferred_element_type=F32)
        mid = (jax.nn.silu(gate) * up).astype(BF16)
        _store_token_tiles(y_ref, jnp.dot(mid, wdb_ref[...], preferred_element_type=F32))

    @pl.when(jnp.logical_not(used))
    def _():
        y_ref[...] = jnp.zeros_like(y_ref)


def _exp_call(blk_e, n_used, xs, wg, wu, wd, layer):
    p = xs.shape[0] // TOKEN_SLABS
    r = EXPERT_ROWS
    blk = (r * TOKEN_SLABS, LANES)
    wspec = lambda shape: pl.BlockSpec((1, 1) + shape, lambda i, be, nu: (layer, be[i], 0, 0))
    grid_spec = pltpu.PrefetchScalarGridSpec(
        num_scalar_prefetch=2,
        grid=(p // r,),
        in_specs=[pl.BlockSpec(blk, lambda i, be, nu: (jnp.minimum(i, nu[0] - 1), 0)),
                  wspec((D_MODEL, D_EXPERT)), wspec((D_MODEL, D_EXPERT)), wspec((D_EXPERT, D_MODEL))],
        out_specs=pl.BlockSpec(blk, lambda i, be, nu: (i, 0)),
        scratch_shapes=[pltpu.VMEM((D_MODEL, D_EXPERT), BF16), pltpu.VMEM((D_MODEL, D_EXPERT), BF16),
                        pltpu.VMEM((D_EXPERT, D_MODEL), BF16)],
    )
    return pl.pallas_call(
        _exp_kernel,
        grid_spec=grid_spec,
        out_shape=jax.ShapeDtypeStruct(xs.shape, F32),
        compiler_params=_params("arbitrary"),
        name="experts",
    )(blk_e, n_used, xs, wg, wu, wd)


def _gather_scratch():
    return [pltpu.VMEM((2, STAGE_ROWS * TOKEN_SLABS, LANES), F32),
            pltpu.VMEM((SEQ_TILE * TOKEN_SLABS, LANES), F32), pltpu.SemaphoreType.DMA((2,))]


def _moe_residual(tabc_ref, tabn_ref, p0_ref, p1_ref, w0_ref, w1_ref, x_ref, mod_ref, ys_ref,
                  stage_ref, moe_ref, sems):
    rows = x_ref.shape[1]
    n = pl.program_id(0) * pl.num_programs(1) + pl.program_id(1)
    last = pl.num_programs(0) * pl.num_programs(1) - 1
    slot = n % 2

    def copy(hbm_tile, stage_tile, s):
        return pltpu.make_async_copy(ys_ref.at[_tile_rows(hbm_tile, RUN_CHUNK)],
                                     stage_ref.at[s, _tile_rows(stage_tile, RUN_CHUNK)], sems.at[s])

    @pl.when(n == 0)
    def _():
        _for_each_run_chunk(tabc_ref, lambda h, s: copy(h, s, 0).start())

    @pl.when(n < last)
    def _():
        _for_each_run_chunk(tabn_ref, lambda h, s: copy(h, s, 1 - slot).start())

    _for_each_run_chunk(tabc_ref, lambda h, s: copy(h, s, slot).wait())

    for r in range(rows):
        y0 = stage_ref[slot, _tile_rows(p0_ref[0, 0, r], 1), :]
        y1 = stage_ref[slot, _tile_rows(p1_ref[0, 0, r], 1), :]
        moe_ref[pl.ds(r * TOKEN_SLABS, TOKEN_SLABS), :] = y0 * w0_ref[0, 0, r] + y1 * w1_ref[0, 0, r]
    return x_ref[0] + mod_ref[0, 5:6, :] * _load_token_tiles(moe_ref, rows)


def _combine_plan_specs(plan, b, steps):
    last = b * steps - 1
    cur = lambda bi, j: (bi * steps + j, 0, 0)
    nxt = lambda bi, j: (jnp.minimum(bi * steps + j + 1, last), 0, 0)
    smem = lambda w, imap: pl.BlockSpec((1, 1, w), imap, memory_space=pltpu.SMEM)
    t = SEQ_TILE
    specs = [smem(LANES, cur), smem(LANES, nxt), smem(t, cur), smem(t, cur), smem(t, cur), smem(t, cur)]
    return specs, (plan["table"], plan["table"], plan["pos0"], plan["pos1"], plan["w0"], plan["w1"])


def _comb_kernel(tabc_ref, tabn_ref, p0_ref, p1_ref, w0_ref, w1_ref, x_ref, mod_ref, ys_ref, o_ref,
                 stage_ref, moe_ref, sems):
    o_ref[0] = _moe_residual(tabc_ref, tabn_ref, p0_ref, p1_ref, w0_ref, w1_ref, x_ref, mod_ref, ys_ref,
                             stage_ref, moe_ref, sems)


def _comb_call(x1, plan, mod, ys):
    b, s, d = x1.shape
    t = SEQ_TILE
    steps = s // t
    plan_specs, plan_args = _combine_plan_specs(plan, b, steps)
    tok = lambda w: pl.BlockSpec((1, t, w), lambda bi, j: (bi, j, 0))
    return pl.pallas_call(
        _comb_kernel,
        grid=(b, steps),
        in_specs=plan_specs + [tok(d), pl.BlockSpec((1, 6, d), lambda bi, j: (bi, 0, 0)),
                               pl.BlockSpec(memory_space=pl.ANY)],
        out_specs=tok(d),
        out_shape=jax.ShapeDtypeStruct((b, s, d), F32),
        scratch_shapes=_gather_scratch(),
        compiler_params=_params("arbitrary", "arbitrary"),
        name="combine_rows",
    )(*plan_args, x1, mod, ys)


def _dispatch_plan(meta_t, starts, counts, n_rows):
    r = EXPERT_ROWS
    t = SEQ_TILE
    lanes = slice(ROUTER_EXPERT_LANE, ROUTER_EXPERT_LANE + N_EXPERTS)
    cnt = counts[0, lanes].astype(jnp.int32)
    start = starts[::8, lanes].astype(jnp.int32)
    in_tile = jnp.concatenate([start[1:], cnt[None, :]], axis=0) - start
    chunks = (in_tile + RUN_CHUNK - 1) // RUN_CHUNK
    span = chunks * RUN_CHUNK
    offset = jnp.cumsum(span, axis=1) - span
    padded = ((cnt + RUN_CHUNK + r - 1) // r) * r
    pad_end = jnp.cumsum(padded)
    pad_start = pad_end - padded
    n_tiles = start.shape[0]
    piece_end = jnp.cumsum(chunks, axis=1)
    k = jnp.arange(MAX_CHUNKS, dtype=jnp.int32)[None, :, None]
    owner = (piece_end[:, None, :] > k) & (piece_end[:, None, :] - chunks[:, None, :] <= k)
    first = (pad_start[None, :] + start - offset)[:, None, :] + k * RUN_CHUNK
    piece_row = jnp.sum(jnp.where(owner, first, 0), axis=2)
    table = jnp.concatenate([piece_row, jnp.broadcast_to(piece_end[:, -1:], (n_tiles, LANES - MAX_CHUNKS))],
                            axis=1).astype(jnp.int32)

    experts = jnp.arange(N_EXPERTS, dtype=jnp.int32)[None, :]
    base = jnp.repeat(offset - start, t, axis=0)

    def position(e, rank):
        return jnp.sum(jnp.where(e[:, None] == experts, base, 0), axis=1) + rank

    as_int = lambda row: meta_t[row].astype(jnp.int32)
    per_tile = lambda v: v.reshape(n_tiles, 1, t)
    plan = {"table": table.reshape(n_tiles, 1, LANES),
            "pos0": per_tile(position(as_int(META_E0), as_int(META_R0))),
            "pos1": per_tile(position(as_int(META_E1), as_int(META_R1))),
            "w0": per_tile(meta_t[META_W0]), "w1": per_tile(meta_t[META_W1])}

    n_used = (pad_end[-1:] // r).astype(jnp.int32)
    blk_row = jnp.minimum(jnp.arange(n_rows // r, dtype=jnp.int32), n_used - 1) * r
    blk_e = jnp.sum((pad_end[None, :] <= blk_row[:, None]).astype(jnp.int32), axis=1)
    pad_first = pad_start + (cnt // r) * r
    pad_last = pad_end - r
    zrow = jnp.concatenate([pad_first, jnp.where(pad_last > pad_first, pad_last, -1), n_used]).astype(jnp.int32)
    return plan, blk_e, n_used, zrow


def kernel(x, c, ada_w, ada_b, norm1_g, norm2_g, w_in, pool_w, pool_scale, q_norm_g, k_norm_g, rel_bias,
           w_out, router_group_w, router_group_b, router_expert_w, router_expert_b,
           moe_w_gate, moe_w_up, moe_w_down):
    b, s, d = x.shape
    depth = ada_w.shape[0]
    n = b * s
    assert d == D_MODEL and b <= 8 and s % SEQ_TILE == 0 and s % ATT_TILE == 0
    spare_blocks = -(-N_EXPERTS * RUN_CHUNK // EXPERT_ROWS)
    p_rows = n * 2 + (N_EXPERTS + spare_blocks) * EXPERT_ROWS

    c_pad = jnp.zeros((8, d), F32).at[:b].set(c)
    mod_all = _ada_call(c_pad, ada_w, ada_b)
    bias = _group_bias(rel_bias)
    head_block = jnp.arange(ATT_WIDTH) // HEAD_DIM
    bd = (head_block[:, None] == head_block[None, :]).astype(BF16)
    sm_scale = HEAD_DIM ** -0.5 * LOG2_E

    moe = None
    for l in range(depth):
        mod = mod_all[l, :b].reshape(b, 6, d)
        qg = (jnp.tile(q_norm_g[l], N_HEADS) * sm_scale).reshape(1, ATT_WIDTH)
        kg = jnp.tile(k_norm_g[l], N_HEADS).reshape(1, ATT_WIDTH)
        outs = _in_call(x, mod, norm1_g[l].reshape(1, d), w_in, pool_w, l,
                        pool_scale[l].reshape(1, POOL_WIDTH), qg, kg, bd, moe)
        if moe is not None:
            x, outs = outs[0], outs[1:]
        pool, q, k, v = outs
        att = _att_call(q, k, v, bias)

        wr = jnp.zeros((d, LANES), F32)
        wr = wr.at[:, ROUTER_GROUP_LANE:ROUTER_GROUP_LANE + N_GROUPS].set(router_group_w[l])
        wr = wr.at[:, ROUTER_EXPERT_LANE:ROUTER_EXPERT_LANE + N_EXPERTS].set(router_expert_w[l])
        wr_hi = wr.astype(BF16)
        wr_lo = (wr - wr_hi.astype(F32)).astype(BF16)
        br = jnp.zeros((1, LANES), F32)
        br = br.at[0, ROUTER_GROUP_LANE:ROUTER_GROUP_LANE + N_GROUPS].set(router_group_b[l])
        br = br.at[0, ROUTER_EXPERT_LANE:ROUTER_EXPERT_LANE + N_EXPERTS].set(router_expert_b[l])
        x1, h2t, meta_t, starts, counts = _out_call(x, pool, att, mod, norm2_g[l].reshape(1, d), w_out, l,
                                                    jnp.concatenate([wr_hi, wr_lo], axis=1), br)

        plan, blk_e, n_used, zrow = _dispatch_plan(meta_t, starts, counts, p_rows)
        xs = _disp_call(h2t, plan, zrow, p_rows)
        ys = _exp_call(blk_e, n_used, xs, moe_w_gate, moe_w_up, moe_w_down, l)
        x = x1
        moe = (plan, mod, ys)
    return _comb_call(x, moe[0], moe[1], moe[2])
```

```python
import jax
import jax.numpy as jnp
from jax import lax
from jax.experimental import pallas as pl
from jax.experimental.pallas import tpu as pltpu

F32 = jnp.float32
BF16 = jnp.bfloat16

D_MODEL = 1024
CHUNK = 64
POOL_WIDTH = 512
POOL_WINDOWS = (2, 4, 8, 16)
POOL_GW = 128
POOL_HALO = 16
ATT_WIDTH = 512
N_HEADS = 8
HEAD_DIM = 64
LEFT = 512
MAX_REL = 128
IN_WIDTH = 2048
N_GROUPS = 4
EXPERTS_PER_GROUP = 8
N_EXPERTS = 32
D_EXPERT = 512
EPS = 1e-6
NEG_INF = -1e30

LANES = 128
TOKEN_SLABS = D_MODEL // LANES
SEQ_TILE = 512
ATT_TILE = 1024
ATT_GROUP = 4 * CHUNK
ATT_BAND = LEFT + ATT_GROUP
ATT_VARIANTS = LEFT // ATT_GROUP + 1
LOG2_E = 1.4426950408889634
EXPERT_ROWS = 512
EXPERT_COLS = 256
RUN_CHUNK = 16
MAX_CHUNKS = 2 * SEQ_TILE // RUN_CHUNK + N_EXPERTS
STAGE_ROWS = MAX_CHUNKS * RUN_CHUNK
assert MAX_CHUNKS < LANES
VMEM_LIMIT = 56 * 1024 * 1024

META_E0, META_E1, META_W0, META_W1, META_R0, META_R1 = 0, 1, 2, 3, 4, 5
ROUTER_EXPERT_LANE = 0
ROUTER_GROUP_LANE = N_EXPERTS


def _params(*sem):
    return pltpu.CompilerParams(dimension_semantics=sem, vmem_limit_bytes=VMEM_LIMIT)


def _first_step():
    return jnp.logical_and(pl.program_id(0) == 0, pl.program_id(1) == 0)


def _load_token_tiles(ref, rows):
    return jnp.concatenate([ref[pl.ds(s, rows, stride=TOKEN_SLABS), :] for s in range(TOKEN_SLABS)], axis=1)


def _store_token_tiles(ref, val, first_token=0):
    rows = val.shape[0]
    for s in range(TOKEN_SLABS):
        ref[pl.ds(first_token * TOKEN_SLABS + s, rows, stride=TOKEN_SLABS), :] = val[:, s * LANES:(s + 1) * LANES]


def _ada_kernel(c_ref, w_ref, b_ref, o_ref):
    ca = jax.nn.silu(c_ref[...])
    o_ref[0] = jnp.dot(ca, w_ref[0], precision=lax.Precision.HIGHEST,
                       preferred_element_type=F32) + b_ref[0]


def _ada_call(c_pad, ada_w, ada_b):
    depth = ada_w.shape[0]
    tn = 1024
    return pl.pallas_call(
        _ada_kernel,
        grid=(depth, 6 * D_MODEL // tn),
        in_specs=[
            pl.BlockSpec((8, D_MODEL), lambda l, n: (0, 0)),
            pl.BlockSpec((1, D_MODEL, tn), lambda l, n: (l, 0, n)),
            pl.BlockSpec((1, 1, tn), lambda l, n: (l, 0, n)),
        ],
        out_specs=pl.BlockSpec((1, 8, tn), lambda l, n: (l, 0, n)),
        out_shape=jax.ShapeDtypeStruct((depth, 8, 6 * D_MODEL), F32),
        compiler_params=_params("arbitrary", "arbitrary"),
        name="ada_mod",
    )(c_pad, ada_w, ada_b.reshape(depth, 1, 6 * D_MODEL))


def _in_kernel(x_ref, *refs):
    _in_body(x_ref[0], *refs)


def _in_comb_kernel(tabc_ref, tabn_ref, p0_ref, p1_ref, w0_ref, w1_ref, x1_ref, modp_ref, ys_ref,
                    mod_ref, g_ref, w_ref, pw_ref, ps_ref, qg_ref, kg_ref, bd_ref,
                    xn_ref, pool_ref, q_ref, k_ref, v_ref, ext_ref, wb_ref, stage_ref, moe_ref, sems):
    x = _moe_residual(tabc_ref, tabn_ref, p0_ref, p1_ref, w0_ref, w1_ref, x1_ref, modp_ref, ys_ref,
                      stage_ref, moe_ref, sems)
    xn_ref[0] = x
    _in_body(x, mod_ref, g_ref, w_ref, pw_ref, ps_ref, qg_ref, kg_ref, bd_ref,
             pool_ref, q_ref, k_ref, v_ref, ext_ref, wb_ref)


def _in_body(x, mod_ref, g_ref, w_ref, pw_ref, ps_ref, qg_ref, kg_ref, bd_ref,
             pool_ref, q_ref, k_ref, v_ref, ext_ref, wb_ref):
    j = pl.program_id(1)
    t = x.shape[0]

    @pl.when(_first_step())
    def _():
        wb_ref[...] = w_ref[0].astype(BF16)

    ms = jnp.mean(x * x, axis=-1, keepdims=True)
    y = x * lax.rsqrt(ms + EPS) * g_ref[...]
    h = y * (1.0 + mod_ref[0, 1:2, :]) + mod_ref[0, 0:1, :]
    z = jnp.dot(h.astype(BF16), wb_ref[...], preferred_element_type=F32)

    bd = bd_ref[...]
    q = z[:, POOL_WIDTH:POOL_WIDTH + ATT_WIDTH]
    k = z[:, POOL_WIDTH + ATT_WIDTH:POOL_WIDTH + 2 * ATT_WIDTH]
    qss = jnp.dot((q * q).astype(BF16), bd, preferred_element_type=F32)
    kss = jnp.dot((k * k).astype(BF16), bd, preferred_element_type=F32)
    q_ref[0] = (q * lax.rsqrt(qss * (1.0 / HEAD_DIM) + EPS) * qg_ref[...]).astype(BF16)
    k_ref[0] = (k * lax.rsqrt(kss * (1.0 / HEAD_DIM) + EPS) * kg_ref[...]).astype(BF16)
    v_ref[0] = z[:, POOL_WIDTH + 2 * ATT_WIDTH:].astype(BF16)

    @pl.when(j == 0)
    def _():
        ext_ref[0:POOL_HALO, :] = jnp.zeros((POOL_HALO, POOL_WIDTH), F32)

    ext_ref[POOL_HALO:, :] = z[:, :POOL_WIDTH]
    pos = j * t + lax.broadcasted_iota(jnp.int32, (t, 1), 0)
    outs = []
    for gi, win in enumerate(POOL_WINDOWS):
        a = ext_ref[:, gi * POOL_GW:(gi + 1) * POOL_GW]
        s = a
        shift = 1
        while shift < win:
            s = s + pltpu.roll(s, shift, 0)
            shift *= 2
        cnt = jnp.minimum(pos + 1, win).astype(F32)
        pooled = s[POOL_HALO:, :] / cnt - a[POOL_HALO:, :]
        outs.append(jnp.dot(pooled.astype(BF16), pw_ref[0, gi].astype(BF16), preferred_element_type=F32))
    pool_ref[0] = (jnp.concatenate(outs, axis=1) * ps_ref[...]).astype(BF16)
    ext_ref[0:POOL_HALO, :] = ext_ref[t:t + POOL_HALO, :]


def _in_call(x, mod, g, w_in, pool_w, layer, pool_scale, qg, kg, bd, moe=None):
    b, s, d = x.shape
    t = SEQ_TILE
    steps = s // t
    tok = lambda w: pl.BlockSpec((1, t, w), lambda bi, j: (bi, j, 0))
    const2 = lambda shape: pl.BlockSpec(shape, lambda bi, j: (0, 0))
    mod_spec = pl.BlockSpec((1, 6, d), lambda bi, j: (bi, 0, 0))
    out = jax.ShapeDtypeStruct((b, s, ATT_WIDTH), BF16)
    in_specs = [
        mod_spec,
        const2((1, d)),
        pl.BlockSpec((1, d, IN_WIDTH), lambda bi, j: (layer, 0, 0), pipeline_mode=pl.Buffered(1)),
        pl.BlockSpec((1, len(POOL_WINDOWS), POOL_GW, POOL_GW), lambda bi, j: (layer, 0, 0, 0)),
        const2((1, POOL_WIDTH)),
        const2((1, ATT_WIDTH)),
        const2((1, ATT_WIDTH)),
        const2((ATT_WIDTH, ATT_WIDTH)),
    ]
    args = (mod, g, w_in, pool_w, pool_scale, qg, kg, bd)
    out_specs = [tok(POOL_WIDTH), tok(ATT_WIDTH), tok(ATT_WIDTH), tok(ATT_WIDTH)]
    out_shape = [jax.ShapeDtypeStruct((b, s, POOL_WIDTH), BF16), out, out, out]
    scratch = [pltpu.VMEM((t + POOL_HALO, POOL_WIDTH), F32), pltpu.VMEM((d, IN_WIDTH), BF16)]
    if moe is None:
        return pl.pallas_call(
            _in_kernel, grid=(b, steps), in_specs=[tok(d)] + in_specs, out_specs=out_specs, out_shape=out_shape,
            scratch_shapes=scratch, compiler_params=_params("arbitrary", "arbitrary"), name="in_proj",
        )(x, *args)
    plan, mod_prev, ys = moe
    plan_specs, plan_args = _combine_plan_specs(plan, b, steps)
    return pl.pallas_call(
        _in_comb_kernel, grid=(b, steps),
        in_specs=plan_specs + [tok(d), mod_spec, pl.BlockSpec(memory_space=pl.ANY)] + in_specs,
        out_specs=[tok(d)] + out_specs,
        out_shape=[jax.ShapeDtypeStruct((b, s, d), F32)] + out_shape,
        scratch_shapes=scratch + _gather_scratch(),
        compiler_params=_params("arbitrary", "arbitrary"), name="combine_in_proj",
    )(*plan_args, x, mod_prev, ys, *args)


def _att_kernel(q_ref, kp_ref, kc_ref, vp_ref, vc_ref, bias_ref, o_ref):
    i = pl.program_id(2)
    tq = q_ref.shape[1]
    lane = lax.broadcasted_iota(jnp.int32, (1, LANES), 1)
    low_half = lane < HEAD_DIM
    for g in range(tq // ATT_GROUP):
        r0 = g * ATT_GROUP
        qg = q_ref[0, r0:r0 + ATT_GROUP, :]
        if r0 < LEFT:
            kb = jnp.concatenate([kp_ref[0, r0:LEFT, :], kc_ref[0, 0:r0 + ATT_GROUP, :]], axis=0)
            vb = jnp.concatenate([vp_ref[0, r0:LEFT, :], vc_ref[0, 0:r0 + ATT_GROUP, :]], axis=0)
            variant = jnp.where(i == 0, g + 1, 0)
        else:
            kb = kc_ref[0, r0 - LEFT:r0 + ATT_GROUP, :]
            vb = vc_ref[0, r0 - LEFT:r0 + ATT_GROUP, :]
            variant = 0
        zero = jnp.zeros_like(qg)
        qs = jnp.concatenate([jnp.where(low_half, qg, zero), jnp.where(low_half, zero, qg)], axis=0)
        s = lax.dot_general(qs, kb, (((1,), (1,)), ((), ())), preferred_element_type=F32)
        s = s + bias_ref[variant]
        m = jnp.max(s, axis=-1, keepdims=True)
        e = jnp.exp2(s - m)
        l = jnp.sum(e, axis=-1, keepdims=True)
        pv = jnp.dot(e.astype(BF16), vb, preferred_element_type=F32) / l
        o_ref[0, r0:r0 + ATT_GROUP, :] = jnp.where(low_half, pv[:ATT_GROUP], pv[ATT_GROUP:]).astype(BF16)


def _att_call(q, k, v, bias):
    b, s, _ = q.shape
    tq = ATT_TILE
    lb = tq // LEFT
    cur = pl.BlockSpec((1, tq, LANES), lambda bi, hp, i: (bi, i, hp))
    prev = pl.BlockSpec((1, LEFT, LANES), lambda bi, hp, i: (bi, jnp.maximum(i * lb - 1, 0), hp))
    return pl.pallas_call(
        _att_kernel,
        grid=(b, N_HEADS // 2, s // tq),
        in_specs=[cur, prev, cur, prev, cur,
                  pl.BlockSpec((ATT_VARIANTS, 2 * ATT_GROUP, ATT_BAND), lambda bi, hp, i: (0, hp, 0))],
        out_specs=cur,
        out_shape=jax.ShapeDtypeStruct((b, s, ATT_WIDTH), BF16),
        compiler_params=_params("arbitrary", "arbitrary", "arbitrary"),
        name="chunk_attn",
    )(q, k, k, v, v, bias)


def _group_bias(rel_bias):
    qi = jnp.arange(ATT_GROUP)[:, None]
    kj = jnp.arange(ATT_BAND)[None, :]
    first = (qi // CHUNK) * CHUNK
    visible = (kj >= first) & (kj < first + LEFT + CHUNK)
    lo = ATT_GROUP - 1
    assert lo >= MAX_REL and ATT_BAND - 1 >= MAX_REL
    tab = rel_bias.astype(F32)
    by_rel = jnp.concatenate([jnp.repeat(tab[:, :1], lo - MAX_REL, axis=1), tab,
                              jnp.repeat(tab[:, -1:], ATT_BAND - 1 - MAX_REL, axis=1)], axis=1)
    rev = by_rel[:, ::-1] * LOG2_E
    length = lo + ATT_BAND
    period = jnp.concatenate([rev, jnp.zeros((N_HEADS, 1), F32)], axis=1)
    flat = jnp.tile(period, (1, ATT_GROUP))[:, :ATT_GROUP * length]
    bias = flat.reshape(N_HEADS, ATT_GROUP, length)[:, :, lo:lo + ATT_BAND]
    tables = []
    for variant in range(ATT_VARIANTS):
        ok = visible if variant == 0 else visible & (kj >= LEFT - (variant - 1) * ATT_GROUP)
        tables.append(jnp.where(ok[None], bias, NEG_INF))
    return jnp.stack(tables, axis=0).reshape(ATT_VARIANTS, N_HEADS * ATT_GROUP, ATT_BAND)


def _out_kernel(x_ref, pool_ref, att_ref, mod_ref, g_ref, wo_ref, wr_ref, br_ref,
                x1_ref, h2_ref, metat_ref, start_ref, cnt_ref, carry_ref, wob_ref):
    t = x_ref.shape[1]

    @pl.when(_first_step())
    def _():
        carry_ref[...] = jnp.zeros_like(carry_ref)
        wob_ref[...] = wo_ref[0].astype(BF16)

    start_ref[...] = carry_ref[...]

    mix = (jnp.dot(pool_ref[0], wob_ref[:POOL_WIDTH, :], preferred_element_type=F32)
           + jnp.dot(att_ref[0], wob_ref[POOL_WIDTH:, :], preferred_element_type=F32))
    x1 = x_ref[0] + mod_ref[0, 2:3, :] * mix
    x1_ref[0] = x1
    ms = jnp.mean(x1 * x1, axis=-1, keepdims=True)
    y = x1 * lax.rsqrt(ms + EPS) * g_ref[...]
    h2 = y * (1.0 + mod_ref[0, 4:5, :]) + mod_ref[0, 3:4, :]

    _store_token_tiles(h2_ref, h2)
    hb = h2.astype(BF16)

    h_lo = (h2 - hb.astype(F32)).astype(BF16)
    r = (jnp.dot(hb, wr_ref[...], preferred_element_type=F32)
         + jnp.dot(h_lo, wr_ref[...], preferred_element_type=F32))
    logits = r[:, :LANES] + r[:, LANES:] + br_ref[...]

    lt = jnp.transpose(logits)
    sub = lax.broadcasted_iota(jnp.int32, (EXPERTS_PER_GROUP, t), 0).astype(F32)
    ninf = jnp.float32(-jnp.inf)
    first_of = lambda hit: jnp.min(jnp.where(hit, sub, float(EXPERTS_PER_GROUP)), axis=0, keepdims=True)

    grp = lt[ROUTER_GROUP_LANE:ROUTER_GROUP_LANE + EXPERTS_PER_GROUP, :]
    gvalid = sub < N_GROUPS
    gl = jnp.where(gvalid, grp, ninf)
    gmax = jnp.max(gl, axis=0, keepdims=True)
    gsum = jnp.sum(jnp.where(gvalid, jnp.exp(grp - gmax), 0.0), axis=0, keepdims=True)
    g_p = 1.0 / gsum
    g_idx = first_of(gl == gmax)

    m1 = m2 = i1 = i2 = jnp.zeros((1, t), F32)
    for g in range(N_GROUPS):
        eg = lt[ROUTER_EXPERT_LANE + g * EXPERTS_PER_GROUP:ROUTER_EXPERT_LANE + (g + 1) * EXPERTS_PER_GROUP, :]
        a1 = jnp.max(eg, axis=0, keepdims=True)
        j1 = first_of(eg == a1)
        eg2 = jnp.where(sub == j1, ninf, eg)
        a2 = jnp.max(eg2, axis=0, keepdims=True)
        j2 = first_of(eg2 == a2)
        pick = g_idx == g
        m1 = jnp.where(pick, a1, m1)
        m2 = jnp.where(pick, a2, m2)
        i1 = jnp.where(pick, j1 + g * EXPERTS_PER_GROUP, i1)
        i2 = jnp.where(pick, j2 + g * EXPERTS_PER_GROUP, i2)
    e2 = jnp.exp(m2 - m1)
    w1 = g_p / (1.0 + e2)
    w2 = g_p * e2 / (1.0 + e2)

    expert = lax.broadcasted_iota(jnp.int32, (N_EXPERTS, t), 0).astype(F32)
    hot = (expert == i1) | (expert == i2)
    earlier = (lax.broadcasted_iota(jnp.int32, (t, t), 0)
               < lax.broadcasted_iota(jnp.int32, (t, t), 1)).astype(BF16)
    before = jnp.dot(hot.astype(BF16), earlier, preferred_element_type=F32) + carry_ref[:, 0:1]
    r1 = jnp.sum(jnp.where(expert == i1, before, 0.0), axis=0, keepdims=True)
    r2 = jnp.sum(jnp.where(expert == i2, before, 0.0), axis=0, keepdims=True)
    carry_ref[...] = carry_ref[...] + jnp.sum(hot.astype(F32), axis=1, keepdims=True)
    cnt_ref[...] = carry_ref[...]

    fields = {META_E0: i1, META_E1: i2, META_W0: w1, META_W1: w2, META_R0: r1, META_R1: r2}
    metat_ref[...] = jnp.concatenate([fields.get(k, jnp.zeros((1, t), F32)) for k in range(8)], axis=0)


def _out_call(x, pool, att, mod, g, w_out, layer, wr, br):
    b, s, d = x.shape
    t = SEQ_TILE
    steps = s // t
    tok = lambda w: pl.BlockSpec((1, t, w), lambda bi, j: (bi, j, 0))
    const2 = lambda shape: pl.BlockSpec(shape, lambda bi, j: (0, 0))
    return pl.pallas_call(
        _out_kernel,
        grid=(b, steps),
        in_specs=[tok(d), tok(POOL_WIDTH), tok(ATT_WIDTH),
                  pl.BlockSpec((1, 6, d), lambda bi, j: (bi, 0, 0)),
                  const2((1, d)),
                  pl.BlockSpec((1, d, d), lambda bi, j: (layer, 0, 0), pipeline_mode=pl.Buffered(1)),
                  const2((d, 2 * LANES)), const2((1, LANES))],
        out_specs=[tok(d), pl.BlockSpec((t * TOKEN_SLABS, LANES), lambda bi, j: (bi * steps + j, 0)),
                   pl.BlockSpec((8, t), lambda bi, j: (0, bi * steps + j)),
                   pl.BlockSpec((N_EXPERTS, LANES), lambda bi, j: (bi * steps + j, 0)),
                   const2((N_EXPERTS, LANES))],
        out_shape=[jax.ShapeDtypeStruct((b, s, d), F32),
                   jax.ShapeDtypeStruct((b * s * TOKEN_SLABS, LANES), F32),
                   jax.ShapeDtypeStruct((8, b * s), F32),
                   jax.ShapeDtypeStruct((b * steps * N_EXPERTS, LANES), F32),
                   jax.ShapeDtypeStruct((N_EXPERTS, LANES), F32)],
        scratch_shapes=[pltpu.VMEM((N_EXPERTS, LANES), F32), pltpu.VMEM((d, d), BF16)],
        compiler_params=_params("arbitrary", "arbitrary"),
        name="out_proj_router",
    )(x, pool, att, mod, g, w_out, wr, br)


def _tile_rows(first_tile, n_tiles):
    return pl.ds(pl.multiple_of(first_tile * TOKEN_SLABS, TOKEN_SLABS), n_tiles * TOKEN_SLABS)


def _for_each_run_chunk(tab_ref, fn):
    def piece(k, carry):
        fn(tab_ref[0, 0, k], k * RUN_CHUNK)
        return carry
    lax.fori_loop(0, tab_ref[0, 0, MAX_CHUNKS], piece, 0)


def _disp_kernel(zrow_ref, tabc_ref, tabp_ref, p0_ref, p1_ref, h_ref, xs_ref, stage_ref, zbuf_ref, sems):
    j = pl.program_id(0)
    rows = h_ref.shape[0] // TOKEN_SLABS
    slot = j % 2

    @pl.when(j == 0)
    def _():
        stage_ref[...] = jnp.zeros_like(stage_ref)
        zbuf_ref[...] = jnp.zeros_like(zbuf_ref)

        def clear(blk_tile):
            return pltpu.make_async_copy(zbuf_ref, xs_ref.at[_tile_rows(blk_tile, EXPERT_ROWS)], sems.at[1])

        def each_clear(act):
            for e in range(2 * N_EXPERTS):
                @pl.when(zrow_ref[e] >= 0)
                def _():
                    act(clear(zrow_ref[e]))

            def tail(blk, carry):
                act(clear(blk * EXPERT_ROWS))
                return carry
            lax.fori_loop(zrow_ref[2 * N_EXPERTS], xs_ref.shape[0] // (EXPERT_ROWS * TOKEN_SLABS), tail, 0)

        each_clear(lambda cp: cp.start())
        each_clear(lambda cp: cp.wait())

    for r in range(rows):
        tile = h_ref[pl.ds(r * TOKEN_SLABS, TOKEN_SLABS), :]
        stage_ref[slot, _tile_rows(p0_ref[0, 0, r], 1), :] = tile
        stage_ref[slot, _tile_rows(p1_ref[0, 0, r], 1), :] = tile

    def copy(hbm_tile, stage_tile, s):
        return pltpu.make_async_copy(stage_ref.at[s, _tile_rows(stage_tile, RUN_CHUNK)],
                                     xs_ref.at[_tile_rows(hbm_tile, RUN_CHUNK)], sems.at[0])

    @pl.when(j > 0)
    def _():
        _for_each_run_chunk(tabp_ref, lambda h, s: copy(h, s, 1 - slot).wait())

    _for_each_run_chunk(tabc_ref, lambda h, s: copy(h, s, slot).start())

    @pl.when(j == pl.num_programs(0) - 1)
    def _():
        _for_each_run_chunk(tabc_ref, lambda h, s: copy(h, s, slot).wait())


def _disp_call(h2t, plan, zrow, p_rows):
    t = SEQ_TILE
    n_tiles = h2t.shape[0] // (t * TOKEN_SLABS)
    smem = lambda w, imap: pl.BlockSpec((1, 1, w), imap, memory_space=pltpu.SMEM)
    cur = lambda i, z: (i, 0, 0)
    grid_spec = pltpu.PrefetchScalarGridSpec(
        num_scalar_prefetch=1,
        grid=(n_tiles,),
        in_specs=[smem(LANES, cur), smem(LANES, lambda i, z: (jnp.maximum(i - 1, 0), 0, 0)),
                  smem(t, cur), smem(t, cur),
                  pl.BlockSpec((t * TOKEN_SLABS, LANES), lambda i, z: (i, 0))],
        out_specs=pl.BlockSpec(memory_space=pl.ANY),
        scratch_shapes=[pltpu.VMEM((2, STAGE_ROWS * TOKEN_SLABS, LANES), F32),
                        pltpu.VMEM((EXPERT_ROWS * TOKEN_SLABS, LANES), F32),
                        pltpu.SemaphoreType.DMA((2,))],
    )
    return pl.pallas_call(
        _disp_kernel,
        grid_spec=grid_spec,
        out_shape=jax.ShapeDtypeStruct((p_rows * TOKEN_SLABS, LANES), F32),
        compiler_params=_params("arbitrary"),
        name="dispatch_rows",
    )(zrow, plan["table"], plan["table"], plan["pos0"], plan["pos1"], h2t)


def _exp_kernel(be_ref, nu_ref, xs_ref, wg_ref, wu_ref, wd_ref, y_ref, wgb_ref, wub_ref, wdb_ref):
    i = pl.program_id(0)
    rows = xs_ref.shape[0] // TOKEN_SLABS
    used = i < nu_ref[0]
    new_expert = jnp.logical_or(i == 0, be_ref[i] != be_ref[jnp.maximum(i - 1, 0)])

    @pl.when(jnp.logical_and(used, new_expert))
    def _():
        wgb_ref[...] = wg_ref[0, 0].astype(BF16)
        wub_ref[...] = wu_ref[0, 0].astype(BF16)
        wdb_ref[...] = wd_ref[0, 0].astype(BF16)

    @pl.when(used)
    def _():
        xb = _load_token_tiles(xs_ref, rows).astype(BF16)
        mids = []
        for c0 in range(0, D_EXPERT, EXPERT_COLS):
            gate = jnp.dot(xb, wgb_ref[:, c0:c0 + EXPERT_COLS], preferred_element_type=F32)
            up = jnp.dot(xb, wub_ref[:, c0:c0 + EXPERT_COLS], preferred_element_type=F32)
            mids.append((jax.nn.silu(gate) * up).astype(BF16))
        mid = jnp.concatenate(mids, axis=1)
        _store_token_tiles(y_ref, jnp.dot(mid, wdb_ref[...], preferred_element_type=F32))

    @pl.when(jnp.logical_not(used))
    def _():
        y_ref[...] = jnp.zeros_like(y_ref)


def _exp_call(blk_e, n_used, xs, wg, wu, wd, layer):
    p = xs.shape[0] // TOKEN_SLABS
    r = EXPERT_ROWS
    blk = (r * TOKEN_SLABS, LANES)
    wspec = lambda shape: pl.BlockSpec((1, 1) + shape, lambda i, be, nu: (layer, be[i], 0, 0))
    grid_spec = pltpu.PrefetchScalarGridSpec(
        num_scalar_prefetch=2,
        grid=(p // r,),
        in_specs=[pl.BlockSpec(blk, lambda i, be, nu: (jnp.minimum(i, nu[0] - 1), 0)),
                  wspec((D_MODEL, D_EXPERT)), wspec((D_MODEL, D_EXPERT)), wspec((D_EXPERT, D_MODEL))],
        out_specs=pl.BlockSpec(blk, lambda i, be, nu: (i, 0)),
        scratch_shapes=[pltpu.VMEM((D_MODEL, D_EXPERT), BF16), pltpu.VMEM((D_MODEL, D_EXPERT), BF16),
                        pltpu.VMEM((D_EXPERT, D_MODEL), BF16)],
    )
    return pl.pallas_call(
        _exp_kernel,
        grid_spec=grid_spec,
        out_shape=jax.ShapeDtypeStruct(xs.shape, F32),
        compiler_params=_params("arbitrary"),
        name="experts",
    )(blk_e, n_used, xs, wg, wu, wd)


def _gather_scratch():
    return [pltpu.VMEM((2, STAGE_ROWS * TOKEN_SLABS, LANES), F32),
            pltpu.VMEM((SEQ_TILE * TOKEN_SLABS, LANES), F32), pltpu.SemaphoreType.DMA((2,))]


def _moe_residual(tabc_ref, tabn_ref, p0_ref, p1_ref, w0_ref, w1_ref, x_ref, mod_ref, ys_ref,
                  stage_ref, moe_ref, sems):
    rows = x_ref.shape[1]
    n = pl.program_id(0) * pl.num_programs(1) + pl.program_id(1)
    last = pl.num_programs(0) * pl.num_programs(1) - 1
    slot = n % 2

    def copy(hbm_tile, stage_tile, s):
        return pltpu.make_async_copy(ys_ref.at[_tile_rows(hbm_tile, RUN_CHUNK)],
                                     stage_ref.at[s, _tile_rows(stage_tile, RUN_CHUNK)], sems.at[s])

    @pl.when(n == 0)
    def _():
        _for_each_run_chunk(tabc_ref, lambda h, s: copy(h, s, 0).start())

    @pl.when(n < last)
    def _():
        _for_each_run_chunk(tabn_ref, lambda h, s: copy(h, s, 1 - slot).start())

    _for_each_run_chunk(tabc_ref, lambda h, s: copy(h, s, slot).wait())

    for r in range(rows):
        y0 = stage_ref[slot, _tile_rows(p0_ref[0, 0, r], 1), :]
        y1 = stage_ref[slot, _tile_rows(p1_ref[0, 0, r], 1), :]
        moe_ref[pl.ds(r * TOKEN_SLABS, TOKEN_SLABS), :] = y0 * w0_ref[0, 0, r] + y1 * w1_ref[0, 0, r]
    return x_ref[0] + mod_ref[0, 5:6, :] * _load_token_tiles(moe_ref, rows)


def _combine_plan_specs(plan, b, steps):
    last = b * steps - 1
    cur = lambda bi, j: (bi * steps + j, 0, 0)
    nxt = lambda bi, j: (jnp.minimum(bi * steps + j + 1, last), 0, 0)
    smem = lambda w, imap: pl.BlockSpec((1, 1, w), imap, memory_space=pltpu.SMEM)
    t = SEQ_TILE
    specs = [smem(LANES, cur), smem(LANES, nxt), smem(t, cur), smem(t, cur), smem(t, cur), smem(t, cur)]
    return specs, (plan["table"], plan["table"], plan["pos0"], plan["pos1"], plan["w0"], plan["w1"])


def _comb_kernel(tabc_ref, tabn_ref, p0_ref, p1_ref, w0_ref, w1_ref, x_ref, mod_ref, ys_ref, o_ref,
                 stage_ref, moe_ref, sems):
    o_ref[0] = _moe_residual(tabc_ref, tabn_ref, p0_ref, p1_ref, w0_ref, w1_ref, x_ref, mod_ref, ys_ref,
                             stage_ref, moe_ref, sems)


def _comb_call(x1, plan, mod, ys):
    b, s, d = x1.shape
    t = SEQ_TILE
    steps = s // t
    plan_specs, plan_args = _combine_plan_specs(plan, b, steps)
    tok = lambda w: pl.BlockSpec((1, t, w), lambda bi, j: (bi, j, 0))
    return pl.pallas_call(
        _comb_kernel,
        grid=(b, steps),
        in_specs=plan_specs + [tok(d), pl.BlockSpec((1, 6, d), lambda bi, j: (bi, 0, 0)),
                               pl.BlockSpec(memory_space=pl.ANY)],
        out_specs=tok(d),
        out_shape=jax.ShapeDtypeStruct((b, s, d), F32),
        scratch_shapes=_gather_scratch(),
        compiler_params=_params("arbitrary", "arbitrary"),
        name="combine_rows",
    )(*plan_args, x1, mod, ys)


def _dispatch_plan(meta_t, starts, counts, n_rows):
    r = EXPERT_ROWS
    t = SEQ_TILE
    cnt = counts[:, 0].astype(jnp.int32)
    start = starts[:, 0].reshape(-1, N_EXPERTS).astype(jnp.int32)
    in_tile = jnp.concatenate([start[1:], cnt[None, :]], axis=0) - start
    chunks = (in_tile + RUN_CHUNK - 1) // RUN_CHUNK
    span = chunks * RUN_CHUNK
    offset = jnp.cumsum(span, axis=1) - span
    padded = ((cnt + RUN_CHUNK + r - 1) // r) * r
    pad_end = jnp.cumsum(padded)
    pad_start = pad_end - padded
    n_tiles = start.shape[0]
    piece_end = jnp.cumsum(chunks, axis=1)
    k = jnp.arange(MAX_CHUNKS, dtype=jnp.int32)[None, :, None]
    owner = (piece_end[:, None, :] > k) & (piece_end[:, None, :] - chunks[:, None, :] <= k)
    first = (pad_start[None, :] + start - offset)[:, None, :] + k * RUN_CHUNK
    piece_row = jnp.sum(jnp.where(owner, first, 0), axis=2)
    table = jnp.concatenate([piece_row, jnp.broadcast_to(piece_end[:, -1:], (n_tiles, LANES - MAX_CHUNKS))],
                            axis=1).astype(jnp.int32)

    experts = jnp.arange(N_EXPERTS, dtype=jnp.int32)[None, :]
    base = jnp.repeat(offset - start, t, axis=0)

    def position(e, rank):
        return jnp.sum(jnp.where(e[:, None] == experts, base, 0), axis=1) + rank

    as_int = lambda row: meta_t[row].astype(jnp.int32)
    per_tile = lambda v: v.reshape(n_tiles, 1, t)
    plan = {"table": table.reshape(n_tiles, 1, LANES),
            "pos0": per_tile(position(as_int(META_E0), as_int(META_R0))),
            "pos1": per_tile(position(as_int(META_E1), as_int(META_R1))),
            "w0": per_tile(meta_t[META_W0]), "w1": per_tile(meta_t[META_W1])}

    n_used = (pad_end[-1:] // r).astype(jnp.int32)
    blk_row = jnp.minimum(jnp.arange(n_rows // r, dtype=jnp.int32), n_used - 1) * r
    blk_e = jnp.sum((pad_end[None, :] <= blk_row[:, None]).astype(jnp.int32), axis=1)
    pad_first = pad_start + (cnt // r) * r
    pad_last = pad_end - r
    zrow = jnp.concatenate([pad_first, jnp.where(pad_last > pad_first, pad_last, -1), n_used]).astype(jnp.int32)
    return plan, blk_e, n_used, zrow


def kernel(x, c, ada_w, ada_b, norm1_g, norm2_g, w_in, pool_w, pool_scale, q_norm_g, k_norm_g, rel_bias,
           w_out, router_group_w, router_group_b, router_expert_w, router_expert_b,
           moe_w_gate, moe_w_up, moe_w_down):
    b, s, d = x.shape
    depth = ada_w.shape[0]
    n = b * s
    assert d == D_MODEL and b <= 8 and s % SEQ_TILE == 0 and s % ATT_TILE == 0
    spare_blocks = -(-N_EXPERTS * RUN_CHUNK // EXPERT_ROWS)
    p_rows = n * 2 + (N_EXPERTS + spare_blocks) * EXPERT_ROWS

    c_pad = jnp.zeros((8, d), F32).at[:b].set(c)
    mod_all = _ada_call(c_pad, ada_w, ada_b)
    bias = _group_bias(rel_bias)
    head_block = jnp.arange(ATT_WIDTH) // HEAD_DIM
    bd = (head_block[:, None] == head_block[None, :]).astype(BF16)
    sm_scale = HEAD_DIM ** -0.5 * LOG2_E

    moe = None
    for l in range(depth):
        mod = mod_all[l, :b].reshape(b, 6, d)
        qg = (jnp.tile(q_norm_g[l], N_HEADS) * sm_scale).reshape(1, ATT_WIDTH)
        kg = jnp.tile(k_norm_g[l], N_HEADS).reshape(1, ATT_WIDTH)
        outs = _in_call(x, mod, norm1_g[l].reshape(1, d), w_in, pool_w, l,
                        pool_scale[l].reshape(1, POOL_WIDTH), qg, kg, bd, moe)
        if moe is not None:
            x, outs = outs[0], outs[1:]
        pool, q, k, v = outs
        att = _att_call(q, k, v, bias)

        wr = jnp.zeros((d, LANES), F32)
        wr = wr.at[:, ROUTER_GROUP_LANE:ROUTER_GROUP_LANE + N_GROUPS].set(router_group_w[l])
        wr = wr.at[:, ROUTER_EXPERT_LANE:ROUTER_EXPERT_LANE + N_EXPERTS].set(router_expert_w[l])
        wr_hi = wr.astype(BF16)
        wr_lo = (wr - wr_hi.astype(F32)).astype(BF16)
        br = jnp.zeros((1, LANES), F32)
        br = br.at[0, ROUTER_GROUP_LANE:ROUTER_GROUP_LANE + N_GROUPS].set(router_group_b[l])
        br = br.at[0, ROUTER_EXPERT_LANE:ROUTER_EXPERT_LANE + N_EXPERTS].set(router_expert_b[l])
        x1, h2t, meta_t, starts, counts = _out_call(x, pool, att, mod, norm2_g[l].reshape(1, d), w_out, l,
                                                    jnp.concatenate([wr_hi, wr_lo], axis=1), br)

        plan, blk_e, n_used, zrow = _dispatch_plan(meta_t, starts, counts, p_rows)
        xs = _disp_call(h2t, plan, zrow, p_rows)
        ys = _exp_call(blk_e, n_used, xs, moe_w_gate, moe_w_up, moe_w_down, l)
        x = x1
        moe = (plan, mod, ys)
    return _comb_call(x, moe[0], moe[1], moe[2])
```

```python
import functools

import jax
import jax.numpy as jnp
from jax import lax
from jax.experimental import pallas as pl
from jax.experimental.pallas import tpu as pltpu

F32 = jnp.float32
BF16 = jnp.bfloat16

D_MODEL = 1024
CHUNK = 64
POOL_WIDTH = 512
POOL_WINDOWS = (2, 4, 8, 16)
POOL_GW = 128
POOL_HALO = 16
ATT_WIDTH = 512
N_HEADS = 8
HEAD_DIM = 64
LEFT = 512
MAX_REL = 128
IN_WIDTH = 2048
N_GROUPS = 4
EXPERTS_PER_GROUP = 8
N_EXPERTS = 32
D_EXPERT = 512
EPS = 1e-6
NEG_INF = -1e30

LANES = 128
TOKEN_SLABS = D_MODEL // LANES
SEQ_TILE = 512
ATT_TILE = 2048
ATT_GROUP = 4 * CHUNK
ATT_BAND = LEFT + ATT_GROUP
ATT_VARIANTS = LEFT // ATT_GROUP + 1
LOG2_E = 1.4426950408889634
EXPERT_ROWS = 512
EXPERT_COLS = 256
RUN_CHUNK = 16
MAX_CHUNKS = 2 * SEQ_TILE // RUN_CHUNK + N_EXPERTS
STAGE_ROWS = MAX_CHUNKS * RUN_CHUNK
assert MAX_CHUNKS < LANES
VMEM_LIMIT = 56 * 1024 * 1024

META_E0, META_E1, META_W0, META_W1, META_R0, META_R1 = 0, 1, 2, 3, 4, 5
ROUTER_EXPERT_LANE = 0
ROUTER_GROUP_LANE = N_EXPERTS


def _params(*sem):
    return pltpu.CompilerParams(dimension_semantics=sem, vmem_limit_bytes=VMEM_LIMIT)


def _first_step():
    return jnp.logical_and(pl.program_id(0) == 0, pl.program_id(1) == 0)


def _load_token_tiles(ref, rows):
    return jnp.concatenate([ref[pl.ds(s, rows, stride=TOKEN_SLABS), :] for s in range(TOKEN_SLABS)], axis=1)


def _store_token_tiles(ref, val, first_token=0):
    rows = val.shape[0]
    for s in range(TOKEN_SLABS):
        ref[pl.ds(first_token * TOKEN_SLABS + s, rows, stride=TOKEN_SLABS), :] = val[:, s * LANES:(s + 1) * LANES]


def _ada_kernel(c_ref, w_ref, b_ref, o_ref):
    ca = jax.nn.silu(c_ref[...])
    o_ref[0] = jnp.dot(ca, w_ref[0], precision=lax.Precision.HIGHEST,
                       preferred_element_type=F32) + b_ref[0]


def _ada_call(c_pad, ada_w, ada_b):
    depth = ada_w.shape[0]
    tn = 1024
    return pl.pallas_call(
        _ada_kernel,
        grid=(depth, 6 * D_MODEL // tn),
        in_specs=[
            pl.BlockSpec((8, D_MODEL), lambda l, n: (0, 0)),
            pl.BlockSpec((1, D_MODEL, tn), lambda l, n: (l, 0, n)),
            pl.BlockSpec((1, 1, tn), lambda l, n: (l, 0, n)),
        ],
        out_specs=pl.BlockSpec((1, 8, tn), lambda l, n: (l, 0, n)),
        out_shape=jax.ShapeDtypeStruct((depth, 8, 6 * D_MODEL), F32),
        compiler_params=_params("arbitrary", "arbitrary"),
        name="ada_mod",
    )(c_pad, ada_w, ada_b.reshape(depth, 1, 6 * D_MODEL))


def _in_kernel(x_ref, *refs):
    _in_body(x_ref[0], *refs)


def _in_comb_kernel(tabc_ref, tabn_ref, p0_ref, p1_ref, w0_ref, w1_ref, x1_ref, modp_ref, ys_ref,
                    mod_ref, g_ref, w_ref, pw_ref, ps_ref, qg_ref, kg_ref, bd_ref,
                    xn_ref, pool_ref, q_ref, k_ref, v_ref, ext_ref, wb_ref, stage_ref, moe_ref, sems):
    x = _moe_residual(tabc_ref, tabn_ref, p0_ref, p1_ref, w0_ref, w1_ref, x1_ref, modp_ref, ys_ref,
                      stage_ref, moe_ref, sems)
    xn_ref[0] = x
    _in_body(x, mod_ref, g_ref, w_ref, pw_ref, ps_ref, qg_ref, kg_ref, bd_ref,
             pool_ref, q_ref, k_ref, v_ref, ext_ref, wb_ref)


def _in_body(x, mod_ref, g_ref, w_ref, pw_ref, ps_ref, qg_ref, kg_ref, bd_ref,
             pool_ref, q_ref, k_ref, v_ref, ext_ref, wb_ref):
    j = pl.program_id(1)
    t = x.shape[0]

    @pl.when(_first_step())
    def _():
        wb_ref[...] = w_ref[0].astype(BF16)

    ms = jnp.mean(x * x, axis=-1, keepdims=True)
    y = x * lax.rsqrt(ms + EPS) * g_ref[...]
    h = y * (1.0 + mod_ref[0, 1:2, :]) + mod_ref[0, 0:1, :]
    z = jnp.dot(h.astype(BF16), wb_ref[...], preferred_element_type=F32)

    bd = bd_ref[...]
    q = z[:, POOL_WIDTH:POOL_WIDTH + ATT_WIDTH]
    k = z[:, POOL_WIDTH + ATT_WIDTH:POOL_WIDTH + 2 * ATT_WIDTH]
    qss = jnp.dot((q * q).astype(BF16), bd, preferred_element_type=F32)
    kss = jnp.dot((k * k).astype(BF16), bd, preferred_element_type=F32)
    q_ref[0] = (q * lax.rsqrt(qss * (1.0 / HEAD_DIM) + EPS) * qg_ref[...]).astype(BF16)
    k_ref[0] = (k * lax.rsqrt(kss * (1.0 / HEAD_DIM) + EPS) * kg_ref[...]).astype(BF16)
    v_ref[0] = z[:, POOL_WIDTH + 2 * ATT_WIDTH:].astype(BF16)

    @pl.when(j == 0)
    def _():
        ext_ref[0:POOL_HALO, :] = jnp.zeros((POOL_HALO, POOL_WIDTH), F32)

    ext_ref[POOL_HALO:, :] = z[:, :POOL_WIDTH]
    pos = j * t + lax.broadcasted_iota(jnp.int32, (t, 1), 0)
    outs = []
    for gi, win in enumerate(POOL_WINDOWS):
        a = ext_ref[:, gi * POOL_GW:(gi + 1) * POOL_GW]
        s = a
        shift = 1
        while shift < win:
            s = s + pltpu.roll(s, shift, 0)
            shift *= 2
        cnt = jnp.minimum(pos + 1, win).astype(F32)
        pooled = s[POOL_HALO:, :] / cnt - a[POOL_HALO:, :]
        outs.append(jnp.dot(pooled.astype(BF16), pw_ref[0, gi].astype(BF16), preferred_element_type=F32))
    pool_ref[0] = (jnp.concatenate(outs, axis=1) * ps_ref[...]).astype(BF16)
    ext_ref[0:POOL_HALO, :] = ext_ref[t:t + POOL_HALO, :]


def _in_call(x, mod, g, w_in, pool_w, layer, pool_scale, qg, kg, bd, moe=None):
    b, s, d = x.shape
    t = SEQ_TILE
    steps = s // t
    tok = lambda w: pl.BlockSpec((1, t, w), lambda bi, j: (bi, j, 0))
    const2 = lambda shape: pl.BlockSpec(shape, lambda bi, j: (0, 0))
    mod_spec = pl.BlockSpec((1, 6, d), lambda bi, j: (bi, 0, 0))
    out = jax.ShapeDtypeStruct((b, s, ATT_WIDTH), BF16)
    in_specs = [
        mod_spec,
        const2((1, d)),
        pl.BlockSpec((1, d, IN_WIDTH), lambda bi, j: (layer, 0, 0), pipeline_mode=pl.Buffered(1)),
        pl.BlockSpec((1, len(POOL_WINDOWS), POOL_GW, POOL_GW), lambda bi, j: (layer, 0, 0, 0)),
        const2((1, POOL_WIDTH)),
        const2((1, ATT_WIDTH)),
        const2((1, ATT_WIDTH)),
        const2((ATT_WIDTH, ATT_WIDTH)),
    ]
    args = (mod, g, w_in, pool_w, pool_scale, qg, kg, bd)
    out_specs = [tok(POOL_WIDTH), tok(ATT_WIDTH), tok(ATT_WIDTH), tok(ATT_WIDTH)]
    out_shape = [jax.ShapeDtypeStruct((b, s, POOL_WIDTH), BF16), out, out, out]
    scratch = [pltpu.VMEM((t + POOL_HALO, POOL_WIDTH), F32), pltpu.VMEM((d, IN_WIDTH), BF16)]
    if moe is None:
        return pl.pallas_call(
            _in_kernel, grid=(b, steps), in_specs=[tok(d)] + in_specs, out_specs=out_specs, out_shape=out_shape,
            scratch_shapes=scratch, compiler_params=_params("arbitrary", "arbitrary"), name="in_proj",
        )(x, *args)
    plan, mod_prev, ys = moe
    plan_specs, plan_args = _combine_plan_specs(plan, b, steps)
    return pl.pallas_call(
        _in_comb_kernel, grid=(b, steps),
        in_specs=plan_specs + [tok(d), mod_spec, pl.BlockSpec(memory_space=pl.ANY)] + in_specs,
        out_specs=[tok(d)] + out_specs,
        out_shape=[jax.ShapeDtypeStruct((b, s, d), F32)] + out_shape,
        scratch_shapes=scratch + _gather_scratch(),
        compiler_params=_params("arbitrary", "arbitrary"), name="combine_in_proj",
    )(*plan_args, x, mod_prev, ys, *args)


def _att_kernel(q_ref, kp_ref, kc_ref, vp_ref, vc_ref, bias_ref, o_ref):
    i = pl.program_id(2)
    tq = q_ref.shape[1]
    lane = lax.broadcasted_iota(jnp.int32, (1, LANES), 1)
    low_half = lane < HEAD_DIM
    for g in range(tq // ATT_GROUP):
        r0 = g * ATT_GROUP
        qg = q_ref[0, r0:r0 + ATT_GROUP, :]
        if r0 < LEFT:
            kb = jnp.concatenate([kp_ref[0, r0:LEFT, :], kc_ref[0, 0:r0 + ATT_GROUP, :]], axis=0)
            vb = jnp.concatenate([vp_ref[0, r0:LEFT, :], vc_ref[0, 0:r0 + ATT_GROUP, :]], axis=0)
            variant = jnp.where(i == 0, g + 1, 0)
        else:
            kb = kc_ref[0, r0 - LEFT:r0 + ATT_GROUP, :]
            vb = vc_ref[0, r0 - LEFT:r0 + ATT_GROUP, :]
            variant = 0
        zero = jnp.zeros_like(qg)
        qs = jnp.concatenate([jnp.where(low_half, qg, zero), jnp.where(low_half, zero, qg)], axis=0)
        s = lax.dot_general(qs, kb, (((1,), (1,)), ((), ())), preferred_element_type=F32)
        s = s + bias_ref[variant]
        m = jnp.max(s, axis=-1, keepdims=True)
        e = jnp.exp2(s - m)
        l = jnp.sum(e, axis=-1, keepdims=True)
        pv = jnp.dot(e.astype(BF16), vb, preferred_element_type=F32) / l
        o_ref[0, r0:r0 + ATT_GROUP, :] = jnp.where(low_half, pv[:ATT_GROUP], pv[ATT_GROUP:]).astype(BF16)


def _att_call(q, k, v, bias):
    b, s, _ = q.shape
    tq = ATT_TILE
    lb = tq // LEFT
    cur = pl.BlockSpec((1, tq, LANES), lambda bi, hp, i: (bi, i, hp))
    prev = pl.BlockSpec((1, LEFT, LANES), lambda bi, hp, i: (bi, jnp.maximum(i * lb - 1, 0), hp))
    return pl.pallas_call(
        _att_kernel,
        grid=(b, N_HEADS // 2, s // tq),
        in_specs=[cur, prev, cur, prev, cur,
                  pl.BlockSpec((ATT_VARIANTS, 2 * ATT_GROUP, ATT_BAND), lambda bi, hp, i: (0, hp, 0))],
        out_specs=cur,
        out_shape=jax.ShapeDtypeStruct((b, s, ATT_WIDTH), BF16),
        compiler_params=_params("arbitrary", "arbitrary", "arbitrary"),
        name="chunk_attn",
    )(q, k, k, v, v, bias)


def _group_bias(rel_bias):
    qi = jnp.arange(ATT_GROUP)[:, None]
    kj = jnp.arange(ATT_BAND)[None, :]
    first = (qi // CHUNK) * CHUNK
    visible = (kj >= first) & (kj < first + LEFT + CHUNK)
    lo = ATT_GROUP - 1
    assert lo >= MAX_REL and ATT_BAND - 1 >= MAX_REL
    tab = rel_bias.astype(F32)
    by_rel = jnp.concatenate([jnp.repeat(tab[:, :1], lo - MAX_REL, axis=1), tab,
                              jnp.repeat(tab[:, -1:], ATT_BAND - 1 - MAX_REL, axis=1)], axis=1)
    rev = by_rel[:, ::-1] * LOG2_E
    length = lo + ATT_BAND
    period = jnp.concatenate([rev, jnp.zeros((N_HEADS, 1), F32)], axis=1)
    flat = jnp.tile(period, (1, ATT_GROUP))[:, :ATT_GROUP * length]
    bias = flat.reshape(N_HEADS, ATT_GROUP, length)[:, :, lo:lo + ATT_BAND]
    tables = []
    for variant in range(ATT_VARIANTS):
        ok = visible if variant == 0 else visible & (kj >= LEFT - (variant - 1) * ATT_GROUP)
        tables.append(jnp.where(ok[None], bias, NEG_INF))
    return jnp.stack(tables, axis=0).reshape(ATT_VARIANTS, N_HEADS * ATT_GROUP, ATT_BAND)


def _out_kernel(x_ref, pool_ref, att_ref, mod_ref, g_ref, wo_ref, wr_ref, br_ref,
                x1_ref, h2_ref, metat_ref, start_ref, cnt_ref, carry_ref, wob_ref):
    t = x_ref.shape[1]

    @pl.when(_first_step())
    def _():
        carry_ref[...] = jnp.zeros_like(carry_ref)
        wob_ref[...] = wo_ref[0].astype(BF16)

    start_ref[...] = carry_ref[...]

    mix = (jnp.dot(pool_ref[0], wob_ref[:POOL_WIDTH, :], preferred_element_type=F32)
           + jnp.dot(att_ref[0], wob_ref[POOL_WIDTH:, :], preferred_element_type=F32))
    x1 = x_ref[0] + mod_ref[0, 2:3, :] * mix
    x1_ref[0] = x1
    ms = jnp.mean(x1 * x1, axis=-1, keepdims=True)
    y = x1 * lax.rsqrt(ms + EPS) * g_ref[...]
    h2 = y * (1.0 + mod_ref[0, 4:5, :]) + mod_ref[0, 3:4, :]

    _store_token_tiles(h2_ref, h2)
    hb = h2.astype(BF16)

    h_lo = (h2 - hb.astype(F32)).astype(BF16)
    r = (jnp.dot(hb, wr_ref[...], preferred_element_type=F32)
         + jnp.dot(h_lo, wr_ref[...], preferred_element_type=F32))
    logits = r[:, :LANES] + r[:, LANES:] + br_ref[...]

    lt = jnp.transpose(logits)
    sub = lax.broadcasted_iota(jnp.int32, (EXPERTS_PER_GROUP, t), 0).astype(F32)
    ninf = jnp.float32(-jnp.inf)
    first_of = lambda hit: jnp.min(jnp.where(hit, sub, float(EXPERTS_PER_GROUP)), axis=0, keepdims=True)

    grp = lt[ROUTER_GROUP_LANE:ROUTER_GROUP_LANE + EXPERTS_PER_GROUP, :]
    gvalid = sub < N_GROUPS
    gl = jnp.where(gvalid, grp, ninf)
    gmax = jnp.max(gl, axis=0, keepdims=True)
    gsum = jnp.sum(jnp.where(gvalid, jnp.exp(grp - gmax), 0.0), axis=0, keepdims=True)
    g_p = 1.0 / gsum
    g_idx = first_of(gl == gmax)

    m1 = m2 = i1 = i2 = jnp.zeros((1, t), F32)
    for g in range(N_GROUPS):
        eg = lt[ROUTER_EXPERT_LANE + g * EXPERTS_PER_GROUP:ROUTER_EXPERT_LANE + (g + 1) * EXPERTS_PER_GROUP, :]
        a1 = jnp.max(eg, axis=0, keepdims=True)
        j1 = first_of(eg == a1)
        eg2 = jnp.where(sub == j1, ninf, eg)
        a2 = jnp.max(eg2, axis=0, keepdims=True)
        j2 = first_of(eg2 == a2)
        pick = g_idx == g
        m1 = jnp.where(pick, a1, m1)
        m2 = jnp.where(pick, a2, m2)
        i1 = jnp.where(pick, j1 + g * EXPERTS_PER_GROUP, i1)
        i2 = jnp.where(pick, j2 + g * EXPERTS_PER_GROUP, i2)
    e2 = jnp.exp(m2 - m1)
    w1 = g_p / (1.0 + e2)
    w2 = g_p * e2 / (1.0 + e2)

    expert = lax.broadcasted_iota(jnp.int32, (N_EXPERTS, t), 0).astype(F32)
    hot = (expert == i1) | (expert == i2)
    earlier = (lax.broadcasted_iota(jnp.int32, (t, t), 0)
               < lax.broadcasted_iota(jnp.int32, (t, t), 1)).astype(BF16)
    before = jnp.dot(hot.astype(BF16), earlier, preferred_element_type=F32) + carry_ref[:, 0:1]
    r1 = jnp.sum(jnp.where(expert == i1, before, 0.0), axis=0, keepdims=True)
    r2 = jnp.sum(jnp.where(expert == i2, before, 0.0), axis=0, keepdims=True)
    carry_ref[...] = carry_ref[...] + jnp.sum(hot.astype(F32), axis=1, keepdims=True)
    cnt_ref[...] = carry_ref[...]

    fields = {META_E0: i1, META_E1: i2, META_W0: w1, META_W1: w2, META_R0: r1, META_R1: r2}
    metat_ref[...] = jnp.concatenate([fields.get(k, jnp.zeros((1, t), F32)) for k in range(8)], axis=0)


def _out_call(x, pool, att, mod, g, w_out, layer, wr, br):
    b, s, d = x.shape
    t = SEQ_TILE
    steps = s // t
    tok = lambda w: pl.BlockSpec((1, t, w), lambda bi, j: (bi, j, 0))
    const2 = lambda shape: pl.BlockSpec(shape, lambda bi, j: (0, 0))
    return pl.pallas_call(
        _out_kernel,
        grid=(b, steps),
        in_specs=[tok(d), tok(POOL_WIDTH), tok(ATT_WIDTH),
                  pl.BlockSpec((1, 6, d), lambda bi, j: (bi, 0, 0)),
                  const2((1, d)),
                  pl.BlockSpec((1, d, d), lambda bi, j: (layer, 0, 0), pipeline_mode=pl.Buffered(1)),
                  const2((d, 2 * LANES)), const2((1, LANES))],
        out_specs=[tok(d), pl.BlockSpec((t * TOKEN_SLABS, LANES), lambda bi, j: (bi * steps + j, 0)),
                   pl.BlockSpec((8, t), lambda bi, j: (0, bi * steps + j)),
                   pl.BlockSpec((N_EXPERTS, LANES), lambda bi, j: (bi * steps + j, 0)),
                   const2((N_EXPERTS, LANES))],
        out_shape=[jax.ShapeDtypeStruct((b, s, d), F32),
                   jax.ShapeDtypeStruct((b * s * TOKEN_SLABS, LANES), F32),
                   jax.ShapeDtypeStruct((8, b * s), F32),
                   jax.ShapeDtypeStruct((b * steps * N_EXPERTS, LANES), F32),
                   jax.ShapeDtypeStruct((N_EXPERTS, LANES), F32)],
        scratch_shapes=[pltpu.VMEM((N_EXPERTS, LANES), F32), pltpu.VMEM((d, d), BF16)],
        compiler_params=_params("arbitrary", "arbitrary"),
        name="out_proj_router",
    )(x, pool, att, mod, g, w_out, wr, br)


def _tile_rows(first_tile, n_tiles):
    return pl.ds(pl.multiple_of(first_tile * TOKEN_SLABS, TOKEN_SLABS), n_tiles * TOKEN_SLABS)


def _for_each_run_chunk(tab_ref, fn):
    def piece(k, carry):
        fn(tab_ref[0, 0, k], k * RUN_CHUNK)
        return carry
    lax.fori_loop(0, tab_ref[0, 0, MAX_CHUNKS], piece, 0)


def _disp_kernel(zrow_ref, tabc_ref, tabp_ref, p0_ref, p1_ref, h_ref, xs_ref, stage_ref, zbuf_ref, sems):
    j = pl.program_id(0)
    rows = h_ref.shape[0] // TOKEN_SLABS
    slot = j % 2

    @pl.when(j == 0)
    def _():
        stage_ref[...] = jnp.zeros_like(stage_ref)
        zbuf_ref[...] = jnp.zeros_like(zbuf_ref)

        def clear(blk_tile):
            return pltpu.make_async_copy(zbuf_ref, xs_ref.at[_tile_rows(blk_tile, EXPERT_ROWS)], sems.at[1])

        def each_clear(act):
            for e in range(2 * N_EXPERTS):
                @pl.when(zrow_ref[e] >= 0)
                def _():
                    act(clear(zrow_ref[e]))

            def tail(blk, carry):
                act(clear(blk * EXPERT_ROWS))
                return carry
            lax.fori_loop(zrow_ref[2 * N_EXPERTS], xs_ref.shape[0] // (EXPERT_ROWS * TOKEN_SLABS), tail, 0)

        each_clear(lambda cp: cp.start())
        each_clear(lambda cp: cp.wait())

    for r in range(rows):
        tile = h_ref[pl.ds(r * TOKEN_SLABS, TOKEN_SLABS), :]
        stage_ref[slot, _tile_rows(p0_ref[0, 0, r], 1), :] = tile
        stage_ref[slot, _tile_rows(p1_ref[0, 0, r], 1), :] = tile

    def copy(hbm_tile, stage_tile, s):
        return pltpu.make_async_copy(stage_ref.at[s, _tile_rows(stage_tile, RUN_CHUNK)],
                                     xs_ref.at[_tile_rows(hbm_tile, RUN_CHUNK)], sems.at[0])

    @pl.when(j > 0)
    def _():
        _for_each_run_chunk(tabp_ref, lambda h, s: copy(h, s, 1 - slot).wait())

    _for_each_run_chunk(tabc_ref, lambda h, s: copy(h, s, slot).start())

    @pl.when(j == pl.num_programs(0) - 1)
    def _():
        _for_each_run_chunk(tabc_ref, lambda h, s: copy(h, s, slot).wait())


def _disp_call(h2t, plan, zrow, p_rows):
    t = SEQ_TILE
    n_tiles = h2t.shape[0] // (t * TOKEN_SLABS)
    smem = lambda w, imap: pl.BlockSpec((1, 1, w), imap, memory_space=pltpu.SMEM)
    cur = lambda i, z: (i, 0, 0)
    grid_spec = pltpu.PrefetchScalarGridSpec(
        num_scalar_prefetch=1,
        grid=(n_tiles,),
        in_specs=[smem(LANES, cur), smem(LANES, lambda i, z: (jnp.maximum(i - 1, 0), 0, 0)),
                  smem(t, cur), smem(t, cur),
                  pl.BlockSpec((t * TOKEN_SLABS, LANES), lambda i, z: (i, 0))],
        out_specs=pl.BlockSpec(memory_space=pl.ANY),
        scratch_shapes=[pltpu.VMEM((2, STAGE_ROWS * TOKEN_SLABS, LANES), F32),
                        pltpu.VMEM((EXPERT_ROWS * TOKEN_SLABS, LANES), F32),
                        pltpu.SemaphoreType.DMA((2,))],
    )
    return pl.pallas_call(
        _disp_kernel,
        grid_spec=grid_spec,
        out_shape=jax.ShapeDtypeStruct((p_rows * TOKEN_SLABS, LANES), F32),
        compiler_params=_params("arbitrary"),
        name="dispatch_rows",
    )(zrow, plan["table"], plan["table"], plan["pos0"], plan["pos1"], h2t)


def _exp_kernel(layer, be_ref, nu_ref, xs_ref, wg_ref, wu_ref, wd_ref, y_ref,
                wgf_ref, wuf_ref, wdf_ref, wgb_ref, wub_ref, wdb_ref, sems):
    i = pl.program_id(0)
    rows = xs_ref.shape[0] // TOKEN_SLABS
    used = i < nu_ref[0]
    e = be_ref[i]
    new_expert = jnp.logical_or(i == 0, e != be_ref[jnp.maximum(i - 1, 0)])

    def fetch(expert, slot):
        return [pltpu.make_async_copy(w_ref.at[layer, expert], f_ref.at[slot], sems.at[slot])
                for w_ref, f_ref in ((wg_ref, wgf_ref), (wu_ref, wuf_ref), (wd_ref, wdf_ref))]

    @pl.when(jnp.logical_and(used, new_expert))
    def _():
        slot = e % 2

        @pl.when(i == 0)
        def _():
            for cp in fetch(e, slot):
                cp.start()

        for cp in fetch(e, slot):
            cp.wait()

        @pl.when(e + 1 < N_EXPERTS)
        def _():
            for cp in fetch(e + 1, 1 - slot):
                cp.start()

        wgb_ref[...] = wgf_ref[slot].astype(BF16)
        wub_ref[...] = wuf_ref[slot].astype(BF16)
        wdb_ref[...] = wdf_ref[slot].astype(BF16)

    @pl.when(used)
    def _():
        xb = _load_token_tiles(xs_ref, rows).astype(BF16)
        mids = []
        for c0 in range(0, D_EXPERT, EXPERT_COLS):
            gate = jnp.dot(xb, wgb_ref[:, c0:c0 + EXPERT_COLS], preferred_element_type=F32)
            up = jnp.dot(xb, wub_ref[:, c0:c0 + EXPERT_COLS], preferred_element_type=F32)
            mids.append((jax.nn.silu(gate) * up).astype(BF16))
        mid = jnp.concatenate(mids, axis=1)
        _store_token_tiles(y_ref, jnp.dot(mid, wdb_ref[...], preferred_element_type=F32))

    @pl.when(jnp.logical_not(used))
    def _():
        y_ref[...] = jnp.zeros_like(y_ref)


def _exp_call(blk_e, n_used, xs, wg, wu, wd, layer):
    p = xs.shape[0] // TOKEN_SLABS
    r = EXPERT_ROWS
    blk = (r * TOKEN_SLABS, LANES)
    hbm = pl.BlockSpec(memory_space=pl.ANY)
    up_shape, down_shape = (D_MODEL, D_EXPERT), (D_EXPERT, D_MODEL)
    grid_spec = pltpu.PrefetchScalarGridSpec(
        num_scalar_prefetch=2,
        grid=(p // r,),
        in_specs=[pl.BlockSpec(blk, lambda i, be, nu: (jnp.minimum(i, nu[0] - 1), 0)), hbm, hbm, hbm],
        out_specs=pl.BlockSpec(blk, lambda i, be, nu: (i, 0)),
        scratch_shapes=[pltpu.VMEM((2,) + up_shape, F32), pltpu.VMEM((2,) + up_shape, F32),
                        pltpu.VMEM((2,) + down_shape, F32),
                        pltpu.VMEM(up_shape, BF16), pltpu.VMEM(up_shape, BF16), pltpu.VMEM(down_shape, BF16),
                        pltpu.SemaphoreType.DMA((2,))],
    )
    return pl.pallas_call(
        functools.partial(_exp_kernel, layer),
        grid_spec=grid_spec,
        out_shape=jax.ShapeDtypeStruct(xs.shape, F32),
        compiler_params=_params("arbitrary"),
        name="experts",
    )(blk_e, n_used, xs, wg, wu, wd)


def _gather_scratch():
    return [pltpu.VMEM((2, STAGE_ROWS * TOKEN_SLABS, LANES), F32),
            pltpu.VMEM((SEQ_TILE * TOKEN_SLABS, LANES), F32), pltpu.SemaphoreType.DMA((2,))]


def _moe_residual(tabc_ref, tabn_ref, p0_ref, p1_ref, w0_ref, w1_ref, x_ref, mod_ref, ys_ref,
                  stage_ref, moe_ref, sems):
    rows = x_ref.shape[1]
    n = pl.program_id(0) * pl.num_programs(1) + pl.program_id(1)
    last = pl.num_programs(0) * pl.num_programs(1) - 1
    slot = n % 2

    def copy(hbm_tile, stage_tile, s):
        return pltpu.make_async_copy(ys_ref.at[_tile_rows(hbm_tile, RUN_CHUNK)],
                                     stage_ref.at[s, _tile_rows(stage_tile, RUN_CHUNK)], sems.at[s])

    @pl.when(n == 0)
    def _():
        _for_each_run_chunk(tabc_ref, lambda h, s: copy(h, s, 0).start())

    @pl.when(n < last)
    def _():
        _for_each_run_chunk(tabn_ref, lambda h, s: copy(h, s, 1 - slot).start())

    _for_each_run_chunk(tabc_ref, lambda h, s: copy(h, s, slot).wait())

    for r in range(rows):
        y0 = stage_ref[slot, _tile_rows(p0_ref[0, 0, r], 1), :]
        y1 = stage_ref[slot, _tile_rows(p1_ref[0, 0, r], 1), :]
        moe_ref[pl.ds(r * TOKEN_SLABS, TOKEN_SLABS), :] = y0 * w0_ref[0, 0, r] + y1 * w1_ref[0, 0, r]
    return x_ref[0] + mod_ref[0, 5:6, :] * _load_token_tiles(moe_ref, rows)


def _combine_plan_specs(plan, b, steps):
    last = b * steps - 1
    cur = lambda bi, j: (bi * steps + j, 0, 0)
    nxt = lambda bi, j: (jnp.minimum(bi * steps + j + 1, last), 0, 0)
    smem = lambda w, imap: pl.BlockSpec((1, 1, w), imap, memory_space=pltpu.SMEM)
    t = SEQ_TILE
    specs = [smem(LANES, cur), smem(LANES, nxt), smem(t, cur), smem(t, cur), smem(t, cur), smem(t, cur)]
    return specs, (plan["table"], plan["table"], plan["pos0"], plan["pos1"], plan["w0"], plan["w1"])


def _comb_kernel(tabc_ref, tabn_ref, p0_ref, p1_ref, w0_ref, w1_ref, x_ref, mod_ref, ys_ref, o_ref,
                 stage_ref, moe_ref, sems):
    o_ref[0] = _moe_residual(tabc_ref, tabn_ref, p0_ref, p1_ref, w0_ref, w1_ref, x_ref, mod_ref, ys_ref,
                             stage_ref, moe_ref, sems)


def _comb_call(x1, plan, mod, ys):
    b, s, d = x1.shape
    t = SEQ_TILE
    steps = s // t
    plan_specs, plan_args = _combine_plan_specs(plan, b, steps)
    tok = lambda w: pl.BlockSpec((1, t, w), lambda bi, j: (bi, j, 0))
    return pl.pallas_call(
        _comb_kernel,
        grid=(b, steps),
        in_specs=plan_specs + [tok(d), pl.BlockSpec((1, 6, d), lambda bi, j: (bi, 0, 0)),
                               pl.BlockSpec(memory_space=pl.ANY)],
        out_specs=tok(d),
        out_shape=jax.ShapeDtypeStruct((b, s, d), F32),
        scratch_shapes=_gather_scratch(),
        compiler_params=_params("arbitrary", "arbitrary"),
        name="combine_rows",
    )(*plan_args, x1, mod, ys)


def _dispatch_plan(meta_t, starts, counts, n_rows):
    r = EXPERT_ROWS
    t = SEQ_TILE
    cnt = counts[:, 0].astype(jnp.int32)
    start = starts[:, 0].reshape(-1, N_EXPERTS).astype(jnp.int32)
    in_tile = jnp.concatenate([start[1:], cnt[None, :]], axis=0) - start
    chunks = (in_tile + RUN_CHUNK - 1) // RUN_CHUNK
    span = chunks * RUN_CHUNK
    offset = jnp.cumsum(span, axis=1) - span
    padded = ((cnt + RUN_CHUNK + r - 1) // r) * r
    pad_end = jnp.cumsum(padded)
    pad_start = pad_end - padded
    n_tiles = start.shape[0]
    piece_end = jnp.cumsum(chunks, axis=1)
    k = jnp.arange(MAX_CHUNKS, dtype=jnp.int32)[None, :, None]
    owner = (piece_end[:, None, :] > k) & (piece_end[:, None, :] - chunks[:, None, :] <= k)
    first = (pad_start[None, :] + start - offset)[:, None, :] + k * RUN_CHUNK
    piece_row = jnp.sum(jnp.where(owner, first, 0), axis=2)
    table = jnp.concatenate([piece_row, jnp.broadcast_to(piece_end[:, -1:], (n_tiles, LANES - MAX_CHUNKS))],
                            axis=1).astype(jnp.int32)

    experts = jnp.arange(N_EXPERTS, dtype=jnp.int32)[None, :]
    base = jnp.repeat(offset - start, t, axis=0)

    def position(e, rank):
        return jnp.sum(jnp.where(e[:, None] == experts, base, 0), axis=1) + rank

    as_int = lambda row: meta_t[row].astype(jnp.int32)
    per_tile = lambda v: v.reshape(n_tiles, 1, t)
    plan = {"table": table.reshape(n_tiles, 1, LANES),
            "pos0": per_tile(position(as_int(META_E0), as_int(META_R0))),
            "pos1": per_tile(position(as_int(META_E1), as_int(META_R1))),
            "w0": per_tile(meta_t[META_W0]), "w1": per_tile(meta_t[META_W1])}

    n_used = (pad_end[-1:] // r).astype(jnp.int32)
    blk_row = jnp.minimum(jnp.arange(n_rows // r, dtype=jnp.int32), n_used - 1) * r
    blk_e = jnp.sum((pad_end[None, :] <= blk_row[:, None]).astype(jnp.int32), axis=1)
    pad_first = pad_start + (cnt // r) * r
    pad_last = pad_end - r
    zrow = jnp.concatenate([pad_first, jnp.where(pad_last > pad_first, pad_last, -1), n_used]).astype(jnp.int32)
    return plan, blk_e, n_used, zrow


def kernel(x, c, ada_w, ada_b, norm1_g, norm2_g, w_in, pool_w, pool_scale, q_norm_g, k_norm_g, rel_bias,
           w_out, router_group_w, router_group_b, router_expert_w, router_expert_b,
           moe_w_gate, moe_w_up, moe_w_down):
    b, s, d = x.shape
    depth = ada_w.shape[0]
    n = b * s
    assert d == D_MODEL and b <= 8 and s % SEQ_TILE == 0 and s % ATT_TILE == 0
    spare_blocks = -(-N_EXPERTS * RUN_CHUNK // EXPERT_ROWS)
    p_rows = n * 2 + (N_EXPERTS + spare_blocks) * EXPERT_ROWS

    c_pad = jnp.zeros((8, d), F32).at[:b].set(c)
    mod_all = _ada_call(c_pad, ada_w, ada_b)
    bias = _group_bias(rel_bias)
    head_block = jnp.arange(ATT_WIDTH) // HEAD_DIM
    bd = (head_block[:, None] == head_block[None, :]).astype(BF16)
    sm_scale = HEAD_DIM ** -0.5 * LOG2_E

    moe = None
    for l in range(depth):
        mod = mod_all[l, :b].reshape(b, 6, d)
        qg = (jnp.tile(q_norm_g[l], N_HEADS) * sm_scale).reshape(1, ATT_WIDTH)
        kg = jnp.tile(k_norm_g[l], N_HEADS).reshape(1, ATT_WIDTH)
        outs = _in_call(x, mod, norm1_g[l].reshape(1, d), w_in, pool_w, l,
                        pool_scale[l].reshape(1, POOL_WIDTH), qg, kg, bd, moe)
        if moe is not None:
            x, outs = outs[0], outs[1:]
        pool, q, k, v = outs
        att = _att_call(q, k, v, bias)

        wr = jnp.zeros((d, LANES), F32)
        wr = wr.at[:, ROUTER_GROUP_LANE:ROUTER_GROUP_LANE + N_GROUPS].set(router_group_w[l])
        wr = wr.at[:, ROUTER_EXPERT_LANE:ROUTER_EXPERT_LANE + N_EXPERTS].set(router_expert_w[l])
        wr_hi = wr.astype(BF16)
        wr_lo = (wr - wr_hi.astype(F32)).astype(BF16)
        br = jnp.zeros((1, LANES), F32)
        br = br.at[0, ROUTER_GROUP_LANE:ROUTER_GROUP_LANE + N_GROUPS].set(router_group_b[l])
        br = br.at[0, ROUTER_EXPERT_LANE:ROUTER_EXPERT_LANE + N_EXPERTS].set(router_expert_b[l])
        x1, h2t, meta_t, starts, counts = _out_call(x, pool, att, mod, norm2_g[l].reshape(1, d), w_out, l,
                                                    jnp.concatenate([wr_hi, wr_lo], axis=1), br)

        plan, blk_e, n_used, zrow = _dispatch_plan(meta_t, starts, counts, p_rows)
        xs = _disp_call(h2t, plan, zrow, p_rows)
        ys = _exp_call(blk_e, n_used, xs, moe_w_gate, moe_w_up, moe_w_down, l)
        x = x1
        moe = (plan, mod, ys)
    return _comb_call(x, moe[0], moe[1], moe[2])
```

```python
import functools

import jax
import jax.numpy as jnp
from jax import lax
from jax.experimental import pallas as pl
from jax.experimental.pallas import tpu as pltpu

F32 = jnp.float32
BF16 = jnp.bfloat16

D_MODEL = 1024
CHUNK = 64
POOL_WIDTH = 512
POOL_WINDOWS = (2, 4, 8, 16)
POOL_GW = 128
POOL_HALO = 16
ATT_WIDTH = 512
N_HEADS = 8
HEAD_DIM = 64
LEFT = 512
MAX_REL = 128
IN_WIDTH = 2048
N_GROUPS = 4
EXPERTS_PER_GROUP = 8
N_EXPERTS = 32
D_EXPERT = 512
EPS = 1e-6
NEG_INF = -1e30

LANES = 128
TOKEN_SLABS = D_MODEL // LANES
SEQ_TILE = 512
ATT_TILE = 4096
ATT_GROUP = 4 * CHUNK
ATT_BAND = LEFT + ATT_GROUP
ATT_VARIANTS = LEFT // ATT_GROUP + 1
LOG2_E = 1.4426950408889634
EXPERT_ROWS = 512
EXPERT_COLS = 256
RUN_CHUNK = 16
MAX_CHUNKS = 2 * SEQ_TILE // RUN_CHUNK + N_EXPERTS
STAGE_ROWS = MAX_CHUNKS * RUN_CHUNK
assert MAX_CHUNKS < LANES
VMEM_LIMIT = 56 * 1024 * 1024

META_E0, META_E1, META_W0, META_W1, META_R0, META_R1 = 0, 1, 2, 3, 4, 5
ROUTER_EXPERT_LANE = 0
ROUTER_GROUP_LANE = N_EXPERTS


def _params(*sem):
    return pltpu.CompilerParams(dimension_semantics=sem, vmem_limit_bytes=VMEM_LIMIT)


def _first_step():
    return jnp.logical_and(pl.program_id(0) == 0, pl.program_id(1) == 0)


def _load_token_tiles(ref, rows):
    return jnp.concatenate([ref[pl.ds(s, rows, stride=TOKEN_SLABS), :] for s in range(TOKEN_SLABS)], axis=1)


def _store_token_tiles(ref, val, first_token=0):
    rows = val.shape[0]
    for s in range(TOKEN_SLABS):
        ref[pl.ds(first_token * TOKEN_SLABS + s, rows, stride=TOKEN_SLABS), :] = val[:, s * LANES:(s + 1) * LANES]


def _ada_kernel(c_ref, w_ref, b_ref, o_ref):
    ca = jax.nn.silu(c_ref[...])
    o_ref[0] = jnp.dot(ca, w_ref[0], precision=lax.Precision.HIGHEST,
                       preferred_element_type=F32) + b_ref[0]


def _ada_call(c_pad, ada_w, ada_b):
    depth = ada_w.shape[0]
    tn = 1024
    return pl.pallas_call(
        _ada_kernel,
        grid=(depth, 6 * D_MODEL // tn),
        in_specs=[
            pl.BlockSpec((8, D_MODEL), lambda l, n: (0, 0)),
            pl.BlockSpec((1, D_MODEL, tn), lambda l, n: (l, 0, n)),
            pl.BlockSpec((1, 1, tn), lambda l, n: (l, 0, n)),
        ],
        out_specs=pl.BlockSpec((1, 8, tn), lambda l, n: (l, 0, n)),
        out_shape=jax.ShapeDtypeStruct((depth, 8, 6 * D_MODEL), F32),
        compiler_params=_params("arbitrary", "arbitrary"),
        name="ada_mod",
    )(c_pad, ada_w, ada_b.reshape(depth, 1, 6 * D_MODEL))


def _in_kernel(x_ref, *refs):
    _in_body(x_ref[0], *refs)


def _in_comb_kernel(tabc_ref, tabn_ref, p0_ref, p1_ref, w0_ref, w1_ref, x1_ref, modp_ref, ys_ref,
                    mod_ref, g_ref, w_ref, pw_ref, ps_ref, qg_ref, kg_ref, bd_ref,
                    xn_ref, pool_ref, q_ref, k_ref, v_ref, ext_ref, wb_ref, stage_ref, moe_ref, sems):
    x = _moe_residual(tabc_ref, tabn_ref, p0_ref, p1_ref, w0_ref, w1_ref, x1_ref, modp_ref, ys_ref,
                      stage_ref, moe_ref, sems)
    xn_ref[0] = x
    _in_body(x, mod_ref, g_ref, w_ref, pw_ref, ps_ref, qg_ref, kg_ref, bd_ref,
             pool_ref, q_ref, k_ref, v_ref, ext_ref, wb_ref)


def _in_body(x, mod_ref, g_ref, w_ref, pw_ref, ps_ref, qg_ref, kg_ref, bd_ref,
             pool_ref, q_ref, k_ref, v_ref, ext_ref, wb_ref):
    j = pl.program_id(1)
    t = x.shape[0]

    @pl.when(_first_step())
    def _():
        wb_ref[...] = w_ref[0].astype(BF16)

    ms = jnp.mean(x * x, axis=-1, keepdims=True)
    y = x * lax.rsqrt(ms + EPS) * g_ref[...]
    h = y * (1.0 + mod_ref[0, 1:2, :]) + mod_ref[0, 0:1, :]
    z = jnp.dot(h.astype(BF16), wb_ref[...], preferred_element_type=F32)

    bd = bd_ref[...]
    q = z[:, POOL_WIDTH:POOL_WIDTH + ATT_WIDTH]
    k = z[:, POOL_WIDTH + ATT_WIDTH:POOL_WIDTH + 2 * ATT_WIDTH]
    qss = jnp.dot((q * q).astype(BF16), bd, preferred_element_type=F32)
    kss = jnp.dot((k * k).astype(BF16), bd, preferred_element_type=F32)
    q_ref[0] = (q * lax.rsqrt(qss * (1.0 / HEAD_DIM) + EPS) * qg_ref[...]).astype(BF16)
    k_ref[0] = (k * lax.rsqrt(kss * (1.0 / HEAD_DIM) + EPS) * kg_ref[...]).astype(BF16)
    v_ref[0] = z[:, POOL_WIDTH + 2 * ATT_WIDTH:].astype(BF16)

    @pl.when(j == 0)
    def _():
        ext_ref[0:POOL_HALO, :] = jnp.zeros((POOL_HALO, POOL_WIDTH), F32)

    ext_ref[POOL_HALO:, :] = z[:, :POOL_WIDTH]
    pos = j * t + lax.broadcasted_iota(jnp.int32, (t, 1), 0)
    outs = []
    for gi, win in enumerate(POOL_WINDOWS):
        a = ext_ref[:, gi * POOL_GW:(gi + 1) * POOL_GW]
        s = a
        shift = 1
        while shift < win:
            s = s + pltpu.roll(s, shift, 0)
            shift *= 2
        cnt = jnp.minimum(pos + 1, win).astype(F32)
        pooled = s[POOL_HALO:, :] / cnt - a[POOL_HALO:, :]
        outs.append(jnp.dot(pooled.astype(BF16), pw_ref[0, gi].astype(BF16), preferred_element_type=F32))
    pool_ref[0] = (jnp.concatenate(outs, axis=1) * ps_ref[...]).astype(BF16)
    ext_ref[0:POOL_HALO, :] = ext_ref[t:t + POOL_HALO, :]


def _in_call(x, mod, g, w_in, pool_w, layer, pool_scale, qg, kg, bd, moe=None):
    b, s, d = x.shape
    t = SEQ_TILE
    steps = s // t
    tok = lambda w: pl.BlockSpec((1, t, w), lambda bi, j: (bi, j, 0))
    const2 = lambda shape: pl.BlockSpec(shape, lambda bi, j: (0, 0))
    mod_spec = pl.BlockSpec((1, 6, d), lambda bi, j: (bi, 0, 0))
    out = jax.ShapeDtypeStruct((b, s, ATT_WIDTH), BF16)
    in_specs = [
        mod_spec,
        const2((1, d)),
        pl.BlockSpec((1, d, IN_WIDTH), lambda bi, j: (layer, 0, 0), pipeline_mode=pl.Buffered(1)),
        pl.BlockSpec((1, len(POOL_WINDOWS), POOL_GW, POOL_GW), lambda bi, j: (layer, 0, 0, 0)),
        const2((1, POOL_WIDTH)),
        const2((1, ATT_WIDTH)),
        const2((1, ATT_WIDTH)),
        const2((ATT_WIDTH, ATT_WIDTH)),
    ]
    args = (mod, g, w_in, pool_w, pool_scale, qg, kg, bd)
    out_specs = [tok(POOL_WIDTH), tok(ATT_WIDTH), tok(ATT_WIDTH), tok(ATT_WIDTH)]
    out_shape = [jax.ShapeDtypeStruct((b, s, POOL_WIDTH), BF16), out, out, out]
    scratch = [pltpu.VMEM((t + POOL_HALO, POOL_WIDTH), F32), pltpu.VMEM((d, IN_WIDTH), BF16)]
    if moe is None:
        return pl.pallas_call(
            _in_kernel, grid=(b, steps), in_specs=[tok(d)] + in_specs, out_specs=out_specs, out_shape=out_shape,
            scratch_shapes=scratch, compiler_params=_params("arbitrary", "arbitrary"), name="in_proj",
        )(x, *args)
    plan, mod_prev, ys = moe
    plan_specs, plan_args = _combine_plan_specs(plan, b, steps)
    return pl.pallas_call(
        _in_comb_kernel, grid=(b, steps),
        in_specs=plan_specs + [tok(d), mod_spec, pl.BlockSpec(memory_space=pl.ANY)] + in_specs,
        out_specs=[tok(d)] + out_specs,
        out_shape=[jax.ShapeDtypeStruct((b, s, d), F32)] + out_shape,
        scratch_shapes=scratch + _gather_scratch(),
        compiler_params=_params("arbitrary", "arbitrary"), name="combine_in_proj",
    )(*plan_args, x, mod_prev, ys, *args)


def _att_kernel(q_ref, kp_ref, kc_ref, vp_ref, vc_ref, bias_ref, o_ref):
    i = pl.program_id(2)
    tq = q_ref.shape[1]
    lane = lax.broadcasted_iota(jnp.int32, (1, LANES), 1)
    low_half = lane < HEAD_DIM
    for g in range(tq // ATT_GROUP):
        r0 = g * ATT_GROUP
        qg = q_ref[0, r0:r0 + ATT_GROUP, :]
        if r0 < LEFT:
            kb = jnp.concatenate([kp_ref[0, r0:LEFT, :], kc_ref[0, 0:r0 + ATT_GROUP, :]], axis=0)
            vb = jnp.concatenate([vp_ref[0, r0:LEFT, :], vc_ref[0, 0:r0 + ATT_GROUP, :]], axis=0)
            variant = jnp.where(i == 0, g + 1, 0)
        else:
            kb = kc_ref[0, r0 - LEFT:r0 + ATT_GROUP, :]
            vb = vc_ref[0, r0 - LEFT:r0 + ATT_GROUP, :]
            variant = 0
        zero = jnp.zeros_like(qg)
        qs = jnp.concatenate([jnp.where(low_half, qg, zero), jnp.where(low_half, zero, qg)], axis=0)
        s = lax.dot_general(qs, kb, (((1,), (1,)), ((), ())), preferred_element_type=F32)
        s = s + bias_ref[variant]
        m = jnp.max(s, axis=-1, keepdims=True)
        e = jnp.exp2(s - m)
        l = jnp.sum(e, axis=-1, keepdims=True)
        pv = jnp.dot(e.astype(BF16), vb, preferred_element_type=F32) / l
        o_ref[0, r0:r0 + ATT_GROUP, :] = jnp.where(low_half, pv[:ATT_GROUP], pv[ATT_GROUP:]).astype(BF16)


def _att_call(q, k, v, bias):
    b, s, _ = q.shape
    tq = min(ATT_TILE, s)
    assert s % tq == 0 and tq >= LEFT and tq % ATT_GROUP == 0
    lb = tq // LEFT
    cur = pl.BlockSpec((1, tq, LANES), lambda bi, hp, i: (bi, i, hp))
    prev = pl.BlockSpec((1, LEFT, LANES), lambda bi, hp, i: (bi, jnp.maximum(i * lb - 1, 0), hp))
    return pl.pallas_call(
        _att_kernel,
        grid=(b, N_HEADS // 2, s // tq),
        in_specs=[cur, prev, cur, prev, cur,
                  pl.BlockSpec((ATT_VARIANTS, 2 * ATT_GROUP, ATT_BAND), lambda bi, hp, i: (0, hp, 0))],
        out_specs=cur,
        out_shape=jax.ShapeDtypeStruct((b, s, ATT_WIDTH), BF16),
        compiler_params=_params("arbitrary", "arbitrary", "arbitrary"),
        name="chunk_attn",
    )(q, k, k, v, v, bias)


def _group_bias(rel_bias):
    qi = jnp.arange(ATT_GROUP)[:, None]
    kj = jnp.arange(ATT_BAND)[None, :]
    first = (qi // CHUNK) * CHUNK
    visible = (kj >= first) & (kj < first + LEFT + CHUNK)
    lo = ATT_GROUP - 1
    assert lo >= MAX_REL and ATT_BAND - 1 >= MAX_REL
    tab = rel_bias.astype(F32)
    by_rel = jnp.concatenate([jnp.repeat(tab[:, :1], lo - MAX_REL, axis=1), tab,
                              jnp.repeat(tab[:, -1:], ATT_BAND - 1 - MAX_REL, axis=1)], axis=1)
    rev = by_rel[:, ::-1] * LOG2_E
    length = lo + ATT_BAND
    period = jnp.concatenate([rev, jnp.zeros((N_HEADS, 1), F32)], axis=1)
    flat = jnp.tile(period, (1, ATT_GROUP))[:, :ATT_GROUP * length]
    bias = flat.reshape(N_HEADS, ATT_GROUP, length)[:, :, lo:lo + ATT_BAND]
    tables = []
    for variant in range(ATT_VARIANTS):
        ok = visible if variant == 0 else visible & (kj >= LEFT - (variant - 1) * ATT_GROUP)
        tables.append(jnp.where(ok[None], bias, NEG_INF))
    return jnp.stack(tables, axis=0).reshape(ATT_VARIANTS, N_HEADS * ATT_GROUP, ATT_BAND)


def _out_kernel(x_ref, pool_ref, att_ref, mod_ref, g_ref, wo_ref, wr_ref, br_ref,
                x1_ref, h2_ref, metat_ref, start_ref, cnt_ref, carry_ref, wob_ref):
    t = x_ref.shape[1]

    @pl.when(_first_step())
    def _():
        carry_ref[...] = jnp.zeros_like(carry_ref)
        wob_ref[...] = wo_ref[0].astype(BF16)

    start_ref[...] = carry_ref[...]

    mix = (jnp.dot(pool_ref[0], wob_ref[:POOL_WIDTH, :], preferred_element_type=F32)
           + jnp.dot(att_ref[0], wob_ref[POOL_WIDTH:, :], preferred_element_type=F32))
    x1 = x_ref[0] + mod_ref[0, 2:3, :] * mix
    x1_ref[0] = x1
    ms = jnp.mean(x1 * x1, axis=-1, keepdims=True)
    y = x1 * lax.rsqrt(ms + EPS) * g_ref[...]
    h2 = y * (1.0 + mod_ref[0, 4:5, :]) + mod_ref[0, 3:4, :]

    _store_token_tiles(h2_ref, h2)
    hb = h2.astype(BF16)

    h_lo = (h2 - hb.astype(F32)).astype(BF16)
    r = (jnp.dot(hb, wr_ref[...], preferred_element_type=F32)
         + jnp.dot(h_lo, wr_ref[...], preferred_element_type=F32))
    logits = r[:, :LANES] + r[:, LANES:] + br_ref[...]

    lt = jnp.transpose(logits)
    sub = lax.broadcasted_iota(jnp.int32, (EXPERTS_PER_GROUP, t), 0).astype(F32)
    ninf = jnp.float32(-jnp.inf)
    first_of = lambda hit: jnp.min(jnp.where(hit, sub, float(EXPERTS_PER_GROUP)), axis=0, keepdims=True)

    grp = lt[ROUTER_GROUP_LANE:ROUTER_GROUP_LANE + EXPERTS_PER_GROUP, :]
    gvalid = sub < N_GROUPS
    gl = jnp.where(gvalid, grp, ninf)
    gmax = jnp.max(gl, axis=0, keepdims=True)
    gsum = jnp.sum(jnp.where(gvalid, jnp.exp(grp - gmax), 0.0), axis=0, keepdims=True)
    g_p = 1.0 / gsum
    g_idx = first_of(gl == gmax)

    m1 = m2 = i1 = i2 = jnp.zeros((1, t), F32)
    for g in range(N_GROUPS):
        eg = lt[ROUTER_EXPERT_LANE + g * EXPERTS_PER_GROUP:ROUTER_EXPERT_LANE + (g + 1) * EXPERTS_PER_GROUP, :]
        a1 = jnp.max(eg, axis=0, keepdims=True)
        j1 = first_of(eg == a1)
        eg2 = jnp.where(sub == j1, ninf, eg)
        a2 = jnp.max(eg2, axis=0, keepdims=True)
        j2 = first_of(eg2 == a2)
        pick = g_idx == g
        m1 = jnp.where(pick, a1, m1)
        m2 = jnp.where(pick, a2, m2)
        i1 = jnp.where(pick, j1 + g * EXPERTS_PER_GROUP, i1)
        i2 = jnp.where(pick, j2 + g * EXPERTS_PER_GROUP, i2)
    e2 = jnp.exp(m2 - m1)
    w1 = g_p / (1.0 + e2)
    w2 = g_p * e2 / (1.0 + e2)

    expert = lax.broadcasted_iota(jnp.int32, (N_EXPERTS, t), 0).astype(F32)
    hot = (expert == i1) | (expert == i2)
    earlier = (lax.broadcasted_iota(jnp.int32, (t, t), 0)
               < lax.broadcasted_iota(jnp.int32, (t, t), 1)).astype(BF16)
    before = jnp.dot(hot.astype(BF16), earlier, preferred_element_type=F32) + carry_ref[:, 0:1]
    r1 = jnp.sum(jnp.where(expert == i1, before, 0.0), axis=0, keepdims=True)
    r2 = jnp.sum(jnp.where(expert == i2, before, 0.0), axis=0, keepdims=True)
    carry_ref[...] = carry_ref[...] + jnp.sum(hot.astype(F32), axis=1, keepdims=True)
    cnt_ref[...] = carry_ref[...]

    fields = {META_E0: i1, META_E1: i2, META_W0: w1, META_W1: w2, META_R0: r1, META_R1: r2}
    metat_ref[...] = jnp.concatenate([fields.get(k, jnp.zeros((1, t), F32)) for k in range(8)], axis=0)


def _out_call(x, pool, att, mod, g, w_out, layer, wr, br):
    b, s, d = x.shape
    t = SEQ_TILE
    steps = s // t
    tok = lambda w: pl.BlockSpec((1, t, w), lambda bi, j: (bi, j, 0))
    const2 = lambda shape: pl.BlockSpec(shape, lambda bi, j: (0, 0))
    return pl.pallas_call(
        _out_kernel,
        grid=(b, steps),
        in_specs=[tok(d), tok(POOL_WIDTH), tok(ATT_WIDTH),
                  pl.BlockSpec((1, 6, d), lambda bi, j: (bi, 0, 0)),
                  const2((1, d)),
                  pl.BlockSpec((1, d, d), lambda bi, j: (layer, 0, 0), pipeline_mode=pl.Buffered(1)),
                  const2((d, 2 * LANES)), const2((1, LANES))],
        out_specs=[tok(d), pl.BlockSpec((t * TOKEN_SLABS, LANES), lambda bi, j: (bi * steps + j, 0)),
                   pl.BlockSpec((8, t), lambda bi, j: (0, bi * steps + j)),
                   pl.BlockSpec((N_EXPERTS, LANES), lambda bi, j: (bi * steps + j, 0)),
                   const2((N_EXPERTS, LANES))],
        out_shape=[jax.ShapeDtypeStruct((b, s, d), F32),
                   jax.ShapeDtypeStruct((b * s * TOKEN_SLABS, LANES), F32),
                   jax.ShapeDtypeStruct((8, b * s), F32),
                   jax.ShapeDtypeStruct((b * steps * N_EXPERTS, LANES), F32),
                   jax.ShapeDtypeStruct((N_EXPERTS, LANES), F32)],
        scratch_shapes=[pltpu.VMEM((N_EXPERTS, LANES), F32), pltpu.VMEM((d, d), BF16)],
        compiler_params=_params("arbitrary", "arbitrary"),
        name="out_proj_router",
    )(x, pool, att, mod, g, w_out, wr, br)


def _tile_rows(first_tile, n_tiles):
    return pl.ds(pl.multiple_of(first_tile * TOKEN_SLABS, TOKEN_SLABS), n_tiles * TOKEN_SLABS)


def _for_each_run_chunk(tab_ref, fn):
    def piece(k, carry):
        fn(tab_ref[0, 0, k], k * RUN_CHUNK)
        return carry
    lax.fori_loop(0, tab_ref[0, 0, MAX_CHUNKS], piece, 0)


def _disp_kernel(zrow_ref, tabc_ref, tabp_ref, p0_ref, p1_ref, h_ref, xs_ref, stage_ref, zbuf_ref, sems):
    j = pl.program_id(0)
    rows = h_ref.shape[0] // TOKEN_SLABS
    slot = j % 2

    @pl.when(j == 0)
    def _():
        stage_ref[...] = jnp.zeros_like(stage_ref)
        zbuf_ref[...] = jnp.zeros_like(zbuf_ref)

        def clear(blk_tile):
            return pltpu.make_async_copy(zbuf_ref, xs_ref.at[_tile_rows(blk_tile, EXPERT_ROWS)], sems.at[1])

        def each_clear(act):
            for e in range(2 * N_EXPERTS):
                @pl.when(zrow_ref[e] >= 0)
                def _():
                    act(clear(zrow_ref[e]))

            def tail(blk, carry):
                act(clear(blk * EXPERT_ROWS))
                return carry
            lax.fori_loop(zrow_ref[2 * N_EXPERTS], xs_ref.shape[0] // (EXPERT_ROWS * TOKEN_SLABS), tail, 0)

        each_clear(lambda cp: cp.start())
        each_clear(lambda cp: cp.wait())

    for r in range(rows):
        tile = h_ref[pl.ds(r * TOKEN_SLABS, TOKEN_SLABS), :]
        stage_ref[slot, _tile_rows(p0_ref[0, 0, r], 1), :] = tile
        stage_ref[slot, _tile_rows(p1_ref[0, 0, r], 1), :] = tile

    def copy(hbm_tile, stage_tile, s):
        return pltpu.make_async_copy(stage_ref.at[s, _tile_rows(stage_tile, RUN_CHUNK)],
                                     xs_ref.at[_tile_rows(hbm_tile, RUN_CHUNK)], sems.at[0])

    @pl.when(j > 0)
    def _():
        _for_each_run_chunk(tabp_ref, lambda h, s: copy(h, s, 1 - slot).wait())

    _for_each_run_chunk(tabc_ref, lambda h, s: copy(h, s, slot).start())

    @pl.when(j == pl.num_programs(0) - 1)
    def _():
        _for_each_run_chunk(tabc_ref, lambda h, s: copy(h, s, slot).wait())


def _disp_call(h2t, plan, zrow, p_rows):
    t = SEQ_TILE
    n_tiles = h2t.shape[0] // (t * TOKEN_SLABS)
    smem = lambda w, imap: pl.BlockSpec((1, 1, w), imap, memory_space=pltpu.SMEM)
    cur = lambda i, z: (i, 0, 0)
    grid_spec = pltpu.PrefetchScalarGridSpec(
        num_scalar_prefetch=1,
        grid=(n_tiles,),
        in_specs=[smem(LANES, cur), smem(LANES, lambda i, z: (jnp.maximum(i - 1, 0), 0, 0)),
                  smem(t, cur), smem(t, cur),
                  pl.BlockSpec((t * TOKEN_SLABS, LANES), lambda i, z: (i, 0))],
        out_specs=pl.BlockSpec(memory_space=pl.ANY),
        scratch_shapes=[pltpu.VMEM((2, STAGE_ROWS * TOKEN_SLABS, LANES), F32),
                        pltpu.VMEM((EXPERT_ROWS * TOKEN_SLABS, LANES), F32),
                        pltpu.SemaphoreType.DMA((2,))],
    )
    return pl.pallas_call(
        _disp_kernel,
        grid_spec=grid_spec,
        out_shape=jax.ShapeDtypeStruct((p_rows * TOKEN_SLABS, LANES), F32),
        compiler_params=_params("arbitrary"),
        name="dispatch_rows",
    )(zrow, plan["table"], plan["table"], plan["pos0"], plan["pos1"], h2t)


def _exp_kernel(layer, be_ref, nu_ref, xs_ref, wg_ref, wu_ref, wd_ref, y_ref,
                wgf_ref, wuf_ref, wdf_ref, wgb_ref, wub_ref, wdb_ref, sems):
    i = pl.program_id(0)
    rows = xs_ref.shape[0] // TOKEN_SLABS
    used = i < nu_ref[0]
    e = be_ref[i]
    new_expert = jnp.logical_or(i == 0, e != be_ref[jnp.maximum(i - 1, 0)])

    def fetch(expert, slot):
        return [pltpu.make_async_copy(w_ref.at[layer, expert], f_ref.at[slot], sems.at[slot])
                for w_ref, f_ref in ((wg_ref, wgf_ref), (wu_ref, wuf_ref), (wd_ref, wdf_ref))]

    @pl.when(jnp.logical_and(used, new_expert))
    def _():
        slot = e % 2

        @pl.when(i == 0)
        def _():
            for cp in fetch(e, slot):
                cp.start()

        for cp in fetch(e, slot):
            cp.wait()

        @pl.when(e + 1 < N_EXPERTS)
        def _():
            for cp in fetch(e + 1, 1 - slot):
                cp.start()

        wgb_ref[...] = wgf_ref[slot].astype(BF16)
        wub_ref[...] = wuf_ref[slot].astype(BF16)
        wdb_ref[...] = wdf_ref[slot].astype(BF16)

    @pl.when(used)
    def _():
        xb = _load_token_tiles(xs_ref, rows).astype(BF16)
        mids = []
        for c0 in range(0, D_EXPERT, EXPERT_COLS):
            gate = jnp.dot(xb, wgb_ref[:, c0:c0 + EXPERT_COLS], preferred_element_type=F32)
            up = jnp.dot(xb, wub_ref[:, c0:c0 + EXPERT_COLS], preferred_element_type=F32)
            mids.append((jax.nn.silu(gate) * up).astype(BF16))
        mid = jnp.concatenate(mids, axis=1)
        _store_token_tiles(y_ref, jnp.dot(mid, wdb_ref[...], preferred_element_type=F32))

    @pl.when(jnp.logical_not(used))
    def _():
        y_ref[...] = jnp.zeros_like(y_ref)


def _exp_call(blk_e, n_used, xs, wg, wu, wd, layer):
    p = xs.shape[0] // TOKEN_SLABS
    r = EXPERT_ROWS
    blk = (r * TOKEN_SLABS, LANES)
    hbm = pl.BlockSpec(memory_space=pl.ANY)
    up_shape, down_shape = (D_MODEL, D_EXPERT), (D_EXPERT, D_MODEL)
    grid_spec = pltpu.PrefetchScalarGridSpec(
        num_scalar_prefetch=2,
        grid=(p // r,),
        in_specs=[pl.BlockSpec(blk, lambda i, be, nu: (jnp.minimum(i, nu[0] - 1), 0)), hbm, hbm, hbm],
        out_specs=pl.BlockSpec(blk, lambda i, be, nu: (i, 0)),
        scratch_shapes=[pltpu.VMEM((2,) + up_shape, F32), pltpu.VMEM((2,) + up_shape, F32),
                        pltpu.VMEM((2,) + down_shape, F32),
                        pltpu.VMEM(up_shape, BF16), pltpu.VMEM(up_shape, BF16), pltpu.VMEM(down_shape, BF16),
                        pltpu.SemaphoreType.DMA((2,))],
    )
    return pl.pallas_call(
        functools.partial(_exp_kernel, layer),
        grid_spec=grid_spec,
        out_shape=jax.ShapeDtypeStruct(xs.shape, F32),
        compiler_params=_params("arbitrary"),
        name="experts",
    )(blk_e, n_used, xs, wg, wu, wd)


def _gather_scratch():
    return [pltpu.VMEM((2, STAGE_ROWS * TOKEN_SLABS, LANES), F32),
            pltpu.VMEM((SEQ_TILE * TOKEN_SLABS, LANES), F32), pltpu.SemaphoreType.DMA((2,))]


def _moe_residual(tabc_ref, tabn_ref, p0_ref, p1_ref, w0_ref, w1_ref, x_ref, mod_ref, ys_ref,
                  stage_ref, moe_ref, sems):
    rows = x_ref.shape[1]
    n = pl.program_id(0) * pl.num_programs(1) + pl.program_id(1)
    last = pl.num_programs(0) * pl.num_programs(1) - 1
    slot = n % 2

    def copy(hbm_tile, stage_tile, s):
        return pltpu.make_async_copy(ys_ref.at[_tile_rows(hbm_tile, RUN_CHUNK)],
                                     stage_ref.at[s, _tile_rows(stage_tile, RUN_CHUNK)], sems.at[s])

    @pl.when(n == 0)
    def _():
        _for_each_run_chunk(tabc_ref, lambda h, s: copy(h, s, 0).start())

    @pl.when(n < last)
    def _():
        _for_each_run_chunk(tabn_ref, lambda h, s: copy(h, s, 1 - slot).start())

    _for_each_run_chunk(tabc_ref, lambda h, s: copy(h, s, slot).wait())

    for r in range(rows):
        y0 = stage_ref[slot, _tile_rows(p0_ref[0, 0, r], 1), :]
        y1 = stage_ref[slot, _tile_rows(p1_ref[0, 0, r], 1), :]
        moe_ref[pl.ds(r * TOKEN_SLABS, TOKEN_SLABS), :] = y0 * w0_ref[0, 0, r] + y1 * w1_ref[0, 0, r]
    return x_ref[0] + mod_ref[0, 5:6, :] * _load_token_tiles(moe_ref, rows)


def _combine_plan_specs(plan, b, steps):
    last = b * steps - 1
    cur = lambda bi, j: (bi * steps + j, 0, 0)
    nxt = lambda bi, j: (jnp.minimum(bi * steps + j + 1, last), 0, 0)
    smem = lambda w, imap: pl.BlockSpec((1, 1, w), imap, memory_space=pltpu.SMEM)
    t = SEQ_TILE
    specs = [smem(LANES, cur), smem(LANES, nxt), smem(t, cur), smem(t, cur), smem(t, cur), smem(t, cur)]
    return specs, (plan["table"], plan["table"], plan["pos0"], plan["pos1"], plan["w0"], plan["w1"])


def _comb_kernel(tabc_ref, tabn_ref, p0_ref, p1_ref, w0_ref, w1_ref, x_ref, mod_ref, ys_ref, o_ref,
                 stage_ref, moe_ref, sems):
    o_ref[0] = _moe_residual(tabc_ref, tabn_ref, p0_ref, p1_ref, w0_ref, w1_ref, x_ref, mod_ref, ys_ref,
                             stage_ref, moe_ref, sems)


def _comb_call(x1, plan, mod, ys):
    b, s, d = x1.shape
    t = SEQ_TILE
    steps = s // t
    plan_specs, plan_args = _combine_plan_specs(plan, b, steps)
    tok = lambda w: pl.BlockSpec((1, t, w), lambda bi, j: (bi, j, 0))
    return pl.pallas_call(
        _comb_kernel,
        grid=(b, steps),
        in_specs=plan_specs + [tok(d), pl.BlockSpec((1, 6, d), lambda bi, j: (bi, 0, 0)),
                               pl.BlockSpec(memory_space=pl.ANY)],
        out_specs=tok(d),
        out_shape=jax.ShapeDtypeStruct((b, s, d), F32),
        scratch_shapes=_gather_scratch(),
        compiler_params=_params("arbitrary", "arbitrary"),
        name="combine_rows",
    )(*plan_args, x1, mod, ys)


def _dispatch_plan(meta_t, starts, counts, n_rows):
    r = EXPERT_ROWS
    t = SEQ_TILE
    cnt = counts[:, 0].astype(jnp.int32)
    start = starts[:, 0].reshape(-1, N_EXPERTS).astype(jnp.int32)
    in_tile = jnp.concatenate([start[1:], cnt[None, :]], axis=0) - start
    chunks = (in_tile + RUN_CHUNK - 1) // RUN_CHUNK
    span = chunks * RUN_CHUNK
    offset = jnp.cumsum(span, axis=1) - span
    padded = ((cnt + RUN_CHUNK + r - 1) // r) * r
    pad_end = jnp.cumsum(padded)
    pad_start = pad_end - padded
    n_tiles = start.shape[0]
    piece_end = jnp.cumsum(chunks, axis=1)
    k = jnp.arange(MAX_CHUNKS, dtype=jnp.int32)[None, :, None]
    owner = (piece_end[:, None, :] > k) & (piece_end[:, None, :] - chunks[:, None, :] <= k)
    first = (pad_start[None, :] + start - offset)[:, None, :] + k * RUN_CHUNK
    piece_row = jnp.sum(jnp.where(owner, first, 0), axis=2)
    table = jnp.concatenate([piece_row, jnp.broadcast_to(piece_end[:, -1:], (n_tiles, LANES - MAX_CHUNKS))],
                            axis=1).astype(jnp.int32)

    experts = jnp.arange(N_EXPERTS, dtype=jnp.int32)[None, :]
    base = jnp.repeat(offset - start, t, axis=0)

    def position(e, rank):
        return jnp.sum(jnp.where(e[:, None] == experts, base, 0), axis=1) + rank

    as_int = lambda row: meta_t[row].astype(jnp.int32)
    per_tile = lambda v: v.reshape(n_tiles, 1, t)
    plan = {"table": table.reshape(n_tiles, 1, LANES),
            "pos0": per_tile(position(as_int(META_E0), as_int(META_R0))),
            "pos1": per_tile(position(as_int(META_E1), as_int(META_R1))),
            "w0": per_tile(meta_t[META_W0]), "w1": per_tile(meta_t[META_W1])}

    n_used = (pad_end[-1:] // r).astype(jnp.int32)
    blk_row = jnp.minimum(jnp.arange(n_rows // r, dtype=jnp.int32), n_used - 1) * r
    blk_e = jnp.sum((pad_end[None, :] <= blk_row[:, None]).astype(jnp.int32), axis=1)
    pad_first = pad_start + (cnt // r) * r
    pad_last = pad_end - r
    zrow = jnp.concatenate([pad_first, jnp.where(pad_last > pad_first, pad_last, -1), n_used]).astype(jnp.int32)
    return plan, blk_e, n_used, zrow


def kernel(x, c, ada_w, ada_b, norm1_g, norm2_g, w_in, pool_w, pool_scale, q_norm_g, k_norm_g, rel_bias,
           w_out, router_group_w, router_group_b, router_expert_w, router_expert_b,
           moe_w_gate, moe_w_up, moe_w_down):
    b, s, d = x.shape
    depth = ada_w.shape[0]
    n = b * s
    assert d == D_MODEL and b <= 8 and s % SEQ_TILE == 0
    spare_blocks = -(-N_EXPERTS * RUN_CHUNK // EXPERT_ROWS)
    p_rows = n * 2 + (N_EXPERTS + spare_blocks) * EXPERT_ROWS

    c_pad = jnp.zeros((8, d), F32).at[:b].set(c)
    mod_all = _ada_call(c_pad, ada_w, ada_b)
    bias = _group_bias(rel_bias)
    head_block = jnp.arange(ATT_WIDTH) // HEAD_DIM
    bd = (head_block[:, None] == head_block[None, :]).astype(BF16)
    sm_scale = HEAD_DIM ** -0.5 * LOG2_E

    moe = None
    for l in range(depth):
        mod = mod_all[l, :b].reshape(b, 6, d)
        qg = (jnp.tile(q_norm_g[l], N_HEADS) * sm_scale).reshape(1, ATT_WIDTH)
        kg = jnp.tile(k_norm_g[l], N_HEADS).reshape(1, ATT_WIDTH)
        outs = _in_call(x, mod, norm1_g[l].reshape(1, d), w_in, pool_w, l,
                        pool_scale[l].reshape(1, POOL_WIDTH), qg, kg, bd, moe)
        if moe is not None:
            x, outs = outs[0], outs[1:]
        pool, q, k, v = outs
        att = _att_call(q, k, v, bias)

        wr = jnp.zeros((d, LANES), F32)
        wr = wr.at[:, ROUTER_GROUP_LANE:ROUTER_GROUP_LANE + N_GROUPS].set(router_group_w[l])
        wr = wr.at[:, ROUTER_EXPERT_LANE:ROUTER_EXPERT_LANE + N_EXPERTS].set(router_expert_w[l])
        wr_hi = wr.astype(BF16)
        wr_lo = (wr - wr_hi.astype(F32)).astype(BF16)
        br = jnp.zeros((1, LANES), F32)
        br = br.at[0, ROUTER_GROUP_LANE:ROUTER_GROUP_LANE + N_GROUPS].set(router_group_b[l])
        br = br.at[0, ROUTER_EXPERT_LANE:ROUTER_EXPERT_LANE + N_EXPERTS].set(router_expert_b[l])
        x1, h2t, meta_t, starts, counts = _out_call(x, pool, att, mod, norm2_g[l].reshape(1, d), w_out, l,
                                                    jnp.concatenate([wr_hi, wr_lo], axis=1), br)

        plan, blk_e, n_used, zrow = _dispatch_plan(meta_t, starts, counts, p_rows)
        xs = _disp_call(h2t, plan, zrow, p_rows)
        ys = _exp_call(blk_e, n_used, xs, moe_w_gate, moe_w_up, moe_w_down, l)
        x = x1
        moe = (plan, mod, ys)
    return _comb_call(x, moe[0], moe[1], moe[2])
```

```python
import functools

import jax
import jax.numpy as jnp
from jax import lax
from jax.experimental import pallas as pl
from jax.experimental.pallas import tpu as pltpu

F32 = jnp.float32
BF16 = jnp.bfloat16

D_MODEL = 1024
CHUNK = 64
POOL_WIDTH = 512
POOL_WINDOWS = (2, 4, 8, 16)
POOL_GW = 128
POOL_HALO = 16
ATT_WIDTH = 512
N_HEADS = 8
HEAD_DIM = 64
LEFT = 512
MAX_REL = 128
IN_WIDTH = 2048
N_GROUPS = 4
EXPERTS_PER_GROUP = 8
N_EXPERTS = 32
D_EXPERT = 512
EPS = 1e-6
NEG_INF = -1e30

LANES = 128
TOKEN_SLABS = D_MODEL // LANES
SEQ_TILE = 512
ATT_TILE = 4096
ATT_GROUP = 4 * CHUNK
ATT_BAND = LEFT + ATT_GROUP
ATT_VARIANTS = LEFT // ATT_GROUP + 1
LOG2_E = 1.4426950408889634
EXPERT_ROWS = 512
EXPERT_COLS = 256
RUN_CHUNK = 16
MAX_CHUNKS = 2 * SEQ_TILE // RUN_CHUNK + N_EXPERTS
STAGE_ROWS = MAX_CHUNKS * RUN_CHUNK
assert MAX_CHUNKS < LANES
VMEM_LIMIT = 56 * 1024 * 1024

META_E0, META_E1, META_W0, META_W1, META_R0, META_R1 = 0, 1, 2, 3, 4, 5
ROUTER_EXPERT_LANE = 0
ROUTER_GROUP_LANE = N_EXPERTS


def _params(*sem):
    return pltpu.CompilerParams(dimension_semantics=sem, vmem_limit_bytes=VMEM_LIMIT)


def _first_step():
    return jnp.logical_and(pl.program_id(0) == 0, pl.program_id(1) == 0)


def _load_token_tiles(ref, rows):
    return jnp.concatenate([ref[pl.ds(s, rows, stride=TOKEN_SLABS), :] for s in range(TOKEN_SLABS)], axis=1)


def _store_token_tiles(ref, val, first_token=0):
    rows = val.shape[0]
    for s in range(TOKEN_SLABS):
        ref[pl.ds(first_token * TOKEN_SLABS + s, rows, stride=TOKEN_SLABS), :] = val[:, s * LANES:(s + 1) * LANES]


def _ada_kernel(c_ref, w_ref, b_ref, o_ref):
    ca = jax.nn.silu(c_ref[...])
    o_ref[0] = jnp.dot(ca, w_ref[0], precision=lax.Precision.HIGHEST,
                       preferred_element_type=F32) + b_ref[0]


def _ada_call(c_pad, ada_w, ada_b):
    depth = ada_w.shape[0]
    tn = 1024
    return pl.pallas_call(
        _ada_kernel,
        grid=(depth, 6 * D_MODEL // tn),
        in_specs=[
            pl.BlockSpec((8, D_MODEL), lambda l, n: (0, 0)),
            pl.BlockSpec((1, D_MODEL, tn), lambda l, n: (l, 0, n)),
            pl.BlockSpec((1, 1, tn), lambda l, n: (l, 0, n)),
        ],
        out_specs=pl.BlockSpec((1, 8, tn), lambda l, n: (l, 0, n)),
        out_shape=jax.ShapeDtypeStruct((depth, 8, 6 * D_MODEL), F32),
        compiler_params=_params("arbitrary", "arbitrary"),
        name="ada_mod",
    )(c_pad, ada_w, ada_b.reshape(depth, 1, 6 * D_MODEL))


def _in_kernel(x_ref, *refs):
    _in_body(x_ref[0], *refs)


def _in_comb_kernel(tabc_ref, tabn_ref, p0_ref, p1_ref, w0_ref, w1_ref, x1_ref, modp_ref, ys_ref,
                    mod_ref, g_ref, w_ref, pw_ref, ps_ref, qg_ref, kg_ref, bd_ref,
                    xn_ref, pool_ref, q_ref, k_ref, v_ref, ext_ref, wb_ref, stage_ref, moe_ref, sems):
    x = _moe_residual(tabc_ref, tabn_ref, p0_ref, p1_ref, w0_ref, w1_ref, x1_ref, modp_ref, ys_ref,
                      stage_ref, moe_ref, sems)
    xn_ref[0] = x
    _in_body(x, mod_ref, g_ref, w_ref, pw_ref, ps_ref, qg_ref, kg_ref, bd_ref,
             pool_ref, q_ref, k_ref, v_ref, ext_ref, wb_ref)


def _in_body(x, mod_ref, g_ref, w_ref, pw_ref, ps_ref, qg_ref, kg_ref, bd_ref,
             pool_ref, q_ref, k_ref, v_ref, ext_ref, wb_ref):
    j = pl.program_id(1)
    t = x.shape[0]

    @pl.when(_first_step())
    def _():
        wb_ref[...] = w_ref[0].astype(BF16)

    ms = jnp.mean(x * x, axis=-1, keepdims=True)
    y = x * lax.rsqrt(ms + EPS) * g_ref[...]
    h = y * (1.0 + mod_ref[0, 1:2, :]) + mod_ref[0, 0:1, :]
    z = jnp.dot(h.astype(BF16), wb_ref[...], preferred_element_type=F32)

    bd = bd_ref[...]
    q = z[:, POOL_WIDTH:POOL_WIDTH + ATT_WIDTH]
    k = z[:, POOL_WIDTH + ATT_WIDTH:POOL_WIDTH + 2 * ATT_WIDTH]
    qss = jnp.dot((q * q).astype(BF16), bd, preferred_element_type=F32)
    kss = jnp.dot((k * k).astype(BF16), bd, preferred_element_type=F32)
    q_ref[0] = (q * lax.rsqrt(qss * (1.0 / HEAD_DIM) + EPS) * qg_ref[...]).astype(BF16)
    k_ref[0] = (k * lax.rsqrt(kss * (1.0 / HEAD_DIM) + EPS) * kg_ref[...]).astype(BF16)
    v_ref[0] = z[:, POOL_WIDTH + 2 * ATT_WIDTH:].astype(BF16)

    @pl.when(j == 0)
    def _():
        ext_ref[0:POOL_HALO, :] = jnp.zeros((POOL_HALO, POOL_WIDTH), F32)

    ext_ref[POOL_HALO:, :] = z[:, :POOL_WIDTH]
    pos = j * t + lax.broadcasted_iota(jnp.int32, (t, 1), 0)
    outs = []
    for gi, win in enumerate(POOL_WINDOWS):
        a = ext_ref[:, gi * POOL_GW:(gi + 1) * POOL_GW]
        s = a
        shift = 1
        while shift < win:
            s = s + pltpu.roll(s, shift, 0)
            shift *= 2
        cnt = jnp.minimum(pos + 1, win).astype(F32)
        pooled = s[POOL_HALO:, :] / cnt - a[POOL_HALO:, :]
        outs.append(jnp.dot(pooled.astype(BF16), pw_ref[0, gi].astype(BF16), preferred_element_type=F32))
    pool_ref[0] = (jnp.concatenate(outs, axis=1) * ps_ref[...]).astype(BF16)
    ext_ref[0:POOL_HALO, :] = ext_ref[t:t + POOL_HALO, :]


def _in_call(x, mod, g, w_in, pool_w, layer, pool_scale, qg, kg, bd, moe=None):
    b, s, d = x.shape
    t = SEQ_TILE
    steps = s // t
    tok = lambda w: pl.BlockSpec((1, t, w), lambda bi, j: (bi, j, 0))
    const2 = lambda shape: pl.BlockSpec(shape, lambda bi, j: (0, 0))
    mod_spec = pl.BlockSpec((1, 6, d), lambda bi, j: (bi, 0, 0))
    out = jax.ShapeDtypeStruct((b, s, ATT_WIDTH), BF16)
    in_specs = [
        mod_spec,
        const2((1, d)),
        pl.BlockSpec((1, d, IN_WIDTH), lambda bi, j: (layer, 0, 0), pipeline_mode=pl.Buffered(1)),
        pl.BlockSpec((1, len(POOL_WINDOWS), POOL_GW, POOL_GW), lambda bi, j: (layer, 0, 0, 0)),
        const2((1, POOL_WIDTH)),
        const2((1, ATT_WIDTH)),
        const2((1, ATT_WIDTH)),
        const2((ATT_WIDTH, ATT_WIDTH)),
    ]
    args = (mod, g, w_in, pool_w, pool_scale, qg, kg, bd)
    out_specs = [tok(POOL_WIDTH), tok(ATT_WIDTH), tok(ATT_WIDTH), tok(ATT_WIDTH)]
    out_shape = [jax.ShapeDtypeStruct((b, s, POOL_WIDTH), BF16), out, out, out]
    scratch = [pltpu.VMEM((t + POOL_HALO, POOL_WIDTH), F32), pltpu.VMEM((d, IN_WIDTH), BF16)]
    if moe is None:
        return pl.pallas_call(
            _in_kernel, grid=(b, steps), in_specs=[tok(d)] + in_specs, out_specs=out_specs, out_shape=out_shape,
            scratch_shapes=scratch, compiler_params=_params("arbitrary", "arbitrary"), name="in_proj",
        )(x, *args)
    plan, mod_prev, ys = moe
    plan_specs, plan_args = _combine_plan_specs(plan, b, steps)
    return pl.pallas_call(
        _in_comb_kernel, grid=(b, steps),
        in_specs=plan_specs + [tok(d), mod_spec, pl.BlockSpec(memory_space=pl.ANY)] + in_specs,
        out_specs=[tok(d)] + out_specs,
        out_shape=[jax.ShapeDtypeStruct((b, s, d), F32)] + out_shape,
        scratch_shapes=scratch + _gather_scratch(),
        compiler_params=_params("arbitrary", "arbitrary"), name="combine_in_proj",
    )(*plan_args, x, mod_prev, ys, *args)


def _att_kernel(q_ref, kp_ref, kc_ref, vp_ref, vc_ref, bias_ref, o_ref):
    i = pl.program_id(2)
    tq = q_ref.shape[1]
    lane = lax.broadcasted_iota(jnp.int32, (1, LANES), 1)
    low_half = lane < HEAD_DIM
    for g in range(tq // ATT_GROUP):
        r0 = g * ATT_GROUP
        qg = q_ref[0, r0:r0 + ATT_GROUP, :]
        if r0 < LEFT:
            kb = jnp.concatenate([kp_ref[0, r0:LEFT, :], kc_ref[0, 0:r0 + ATT_GROUP, :]], axis=0)
            vb = jnp.concatenate([vp_ref[0, r0:LEFT, :], vc_ref[0, 0:r0 + ATT_GROUP, :]], axis=0)
            variant = jnp.where(i == 0, g + 1, 0)
        else:
            kb = kc_ref[0, r0 - LEFT:r0 + ATT_GROUP, :]
            vb = vc_ref[0, r0 - LEFT:r0 + ATT_GROUP, :]
            variant = 0
        zero = jnp.zeros_like(qg)
        qs = jnp.concatenate([jnp.where(low_half, qg, zero), jnp.where(low_half, zero, qg)], axis=0)
        s = lax.dot_general(qs, kb, (((1,), (1,)), ((), ())), preferred_element_type=F32)
        s = s + bias_ref[variant]
        m = jnp.max(s, axis=-1, keepdims=True)
        e = jnp.exp2(s - m)
        l = jnp.sum(e, axis=-1, keepdims=True)
        pv = jnp.dot(e.astype(BF16), vb, preferred_element_type=F32) / l
        o_ref[0, r0:r0 + ATT_GROUP, :] = jnp.where(low_half, pv[:ATT_GROUP], pv[ATT_GROUP:]).astype(BF16)


def _att_call(q, k, v, bias):
    b, s, _ = q.shape
    tq = min(ATT_TILE, s)
    assert s % tq == 0 and tq >= LEFT and tq % ATT_GROUP == 0
    lb = tq // LEFT
    cur = pl.BlockSpec((1, tq, LANES), lambda bi, hp, i: (bi, i, hp))
    prev = pl.BlockSpec((1, LEFT, LANES), lambda bi, hp, i: (bi, jnp.maximum(i * lb - 1, 0), hp))
    return pl.pallas_call(
        _att_kernel,
        grid=(b, N_HEADS // 2, s // tq),
        in_specs=[cur, prev, cur, prev, cur,
                  pl.BlockSpec((ATT_VARIANTS, 2 * ATT_GROUP, ATT_BAND), lambda bi, hp, i: (0, hp, 0))],
        out_specs=cur,
        out_shape=jax.ShapeDtypeStruct((b, s, ATT_WIDTH), BF16),
        compiler_params=_params("arbitrary", "arbitrary", "arbitrary"),
        name="chunk_attn",
    )(q, k, k, v, v, bias)


def _group_bias(rel_bias):
    qi = jnp.arange(ATT_GROUP)[:, None]
    kj = jnp.arange(ATT_BAND)[None, :]
    first = (qi // CHUNK) * CHUNK
    visible = (kj >= first) & (kj < first + LEFT + CHUNK)
    lo = ATT_GROUP - 1
    assert lo >= MAX_REL and ATT_BAND - 1 >= MAX_REL
    tab = rel_bias.astype(F32)
    by_rel = jnp.concatenate([jnp.repeat(tab[:, :1], lo - MAX_REL, axis=1), tab,
                              jnp.repeat(tab[:, -1:], ATT_BAND - 1 - MAX_REL, axis=1)], axis=1)
    rev = by_rel[:, ::-1] * LOG2_E
    length = lo + ATT_BAND
    period = jnp.concatenate([rev, jnp.zeros((N_HEADS, 1), F32)], axis=1)
    flat = jnp.tile(period, (1, ATT_GROUP))[:, :ATT_GROUP * length]
    bias = flat.reshape(N_HEADS, ATT_GROUP, length)[:, :, lo:lo + ATT_BAND]
    tables = []
    for variant in range(ATT_VARIANTS):
        ok = visible if variant == 0 else visible & (kj >= LEFT - (variant - 1) * ATT_GROUP)
        tables.append(jnp.where(ok[None], bias, NEG_INF))
    return jnp.stack(tables, axis=0).reshape(ATT_VARIANTS, N_HEADS * ATT_GROUP, ATT_BAND)


def _out_kernel(x_ref, pool_ref, att_ref, mod_ref, g_ref, wo_ref, wr_ref, br_ref,
                x1_ref, h2_ref, metat_ref, start_ref, cnt_ref, carry_ref, wob_ref):
    t = x_ref.shape[1]

    @pl.when(_first_step())
    def _():
        carry_ref[...] = jnp.zeros_like(carry_ref)
        wob_ref[...] = wo_ref[0].astype(BF16)

    start_ref[...] = carry_ref[...]

    mix = (jnp.dot(pool_ref[0], wob_ref[:POOL_WIDTH, :], preferred_element_type=F32)
           + jnp.dot(att_ref[0], wob_ref[POOL_WIDTH:, :], preferred_element_type=F32))
    x1 = x_ref[0] + mod_ref[0, 2:3, :] * mix
    x1_ref[0] = x1
    ms = jnp.mean(x1 * x1, axis=-1, keepdims=True)
    y = x1 * lax.rsqrt(ms + EPS) * g_ref[...]
    h2 = y * (1.0 + mod_ref[0, 4:5, :]) + mod_ref[0, 3:4, :]

    _store_token_tiles(h2_ref, h2)
    hb = h2.astype(BF16)

    h_lo = (h2 - hb.astype(F32)).astype(BF16)
    r = (jnp.dot(hb, wr_ref[...], preferred_element_type=F32)
         + jnp.dot(h_lo, wr_ref[...], preferred_element_type=F32))
    logits = r[:, :LANES] + r[:, LANES:] + br_ref[...]

    lt = jnp.transpose(logits)
    sub = lax.broadcasted_iota(jnp.int32, (EXPERTS_PER_GROUP, t), 0).astype(F32)
    ninf = jnp.float32(-jnp.inf)
    first_of = lambda hit: jnp.min(jnp.where(hit, sub, float(EXPERTS_PER_GROUP)), axis=0, keepdims=True)

    grp = lt[ROUTER_GROUP_LANE:ROUTER_GROUP_LANE + EXPERTS_PER_GROUP, :]
    gvalid = sub < N_GROUPS
    gl = jnp.where(gvalid, grp, ninf)
    gmax = jnp.max(gl, axis=0, keepdims=True)
    gsum = jnp.sum(jnp.where(gvalid, jnp.exp(grp - gmax), 0.0), axis=0, keepdims=True)
    g_p = 1.0 / gsum
    g_idx = first_of(gl == gmax)

    m1 = m2 = i1 = i2 = jnp.zeros((1, t), F32)
    for g in range(N_GROUPS):
        eg = lt[ROUTER_EXPERT_LANE + g * EXPERTS_PER_GROUP:ROUTER_EXPERT_LANE + (g + 1) * EXPERTS_PER_GROUP, :]
        a1 = jnp.max(eg, axis=0, keepdims=True)
        j1 = first_of(eg == a1)
        eg2 = jnp.where(sub == j1, ninf, eg)
        a2 = jnp.max(eg2, axis=0, keepdims=True)
        j2 = first_of(eg2 == a2)
        pick = g_idx == g
        m1 = jnp.where(pick, a1, m1)
        m2 = jnp.where(pick, a2, m2)
        i1 = jnp.where(pick, j1 + g * EXPERTS_PER_GROUP, i1)
        i2 = jnp.where(pick, j2 + g * EXPERTS_PER_GROUP, i2)
    e2 = jnp.exp(m2 - m1)
    w1 = g_p / (1.0 + e2)
    w2 = g_p * e2 / (1.0 + e2)

    expert = lax.broadcasted_iota(jnp.int32, (N_EXPERTS, t), 0).astype(F32)
    hot = (expert == i1) | (expert == i2)
    earlier = (lax.broadcasted_iota(jnp.int32, (t, t), 0)
               < lax.broadcasted_iota(jnp.int32, (t, t), 1)).astype(BF16)
    before = jnp.dot(hot.astype(BF16), earlier, preferred_element_type=F32) + carry_ref[:, 0:1]
    r1 = jnp.sum(jnp.where(expert == i1, before, 0.0), axis=0, keepdims=True)
    r2 = jnp.sum(jnp.where(expert == i2, before, 0.0), axis=0, keepdims=True)
    carry_ref[...] = carry_ref[...] + jnp.sum(hot.astype(F32), axis=1, keepdims=True)
    cnt_ref[...] = carry_ref[...]

    fields = {META_E0: i1, META_E1: i2, META_W0: w1, META_W1: w2, META_R0: r1, META_R1: r2}
    metat_ref[...] = jnp.concatenate([fields.get(k, jnp.zeros((1, t), F32)) for k in range(8)], axis=0)


def _out_call(x, pool, att, mod, g, w_out, layer, wr, br):
    b, s, d = x.shape
    t = SEQ_TILE
    steps = s // t
    tok = lambda w: pl.BlockSpec((1, t, w), lambda bi, j: (bi, j, 0))
    const2 = lambda shape: pl.BlockSpec(shape, lambda bi, j: (0, 0))
    return pl.pallas_call(
        _out_kernel,
        grid=(b, steps),
        in_specs=[tok(d), tok(POOL_WIDTH), tok(ATT_WIDTH),
                  pl.BlockSpec((1, 6, d), lambda bi, j: (bi, 0, 0)),
                  const2((1, d)),
                  pl.BlockSpec((1, d, d), lambda bi, j: (layer, 0, 0), pipeline_mode=pl.Buffered(1)),
                  const2((d, 2 * LANES)), const2((1, LANES))],
        out_specs=[tok(d), pl.BlockSpec((t * TOKEN_SLABS, LANES), lambda bi, j: (bi * steps + j, 0)),
                   pl.BlockSpec((8, t), lambda bi, j: (0, bi * steps + j)),
                   pl.BlockSpec((N_EXPERTS, LANES), lambda bi, j: (bi * steps + j, 0)),
                   const2((N_EXPERTS, LANES))],
        out_shape=[jax.ShapeDtypeStruct((b, s, d), F32),
                   jax.ShapeDtypeStruct((b * s * TOKEN_SLABS, LANES), F32),
                   jax.ShapeDtypeStruct((8, b * s), F32),
                   jax.ShapeDtypeStruct((b * steps * N_EXPERTS, LANES), F32),
                   jax.ShapeDtypeStruct((N_EXPERTS, LANES), F32)],
        scratch_shapes=[pltpu.VMEM((N_EXPERTS, LANES), F32), pltpu.VMEM((d, d), BF16)],
        compiler_params=_params("arbitrary", "arbitrary"),
        name="out_proj_router",
    )(x, pool, att, mod, g, w_out, wr, br)


def _tile_rows(first_tile, n_tiles):
    return pl.ds(pl.multiple_of(first_tile * TOKEN_SLABS, TOKEN_SLABS), n_tiles * TOKEN_SLABS)


def _for_each_run_chunk(tab_ref, fn, partial=None):
    n_all, n_part = tab_ref[0, 0, MAX_CHUNKS], tab_ref[0, 0, MAX_CHUNKS + 1]
    lo, hi = {None: (0, n_all), True: (0, n_part), False: (n_part, n_all)}[partial]

    def piece(k, carry):
        fn(tab_ref[0, 0, k], k * RUN_CHUNK)
        return carry
    lax.fori_loop(lo, hi, piece, 0)


def _disp_kernel(zrow_ref, tabc_ref, tabp_ref, tabpp_ref, p0_ref, p1_ref, h_ref, xs_ref,
                 stage_ref, zbuf_ref, sems):
    j = pl.program_id(0)
    last = pl.num_programs(0) - 1
    rows = h_ref.shape[0] // TOKEN_SLABS
    slot = j % 2

    def copy(hbm_tile, stage_tile, s, sem):
        return pltpu.make_async_copy(stage_ref.at[s, _tile_rows(stage_tile, RUN_CHUNK)],
                                     xs_ref.at[_tile_rows(hbm_tile, RUN_CHUNK)], sems.at[sem])

    def drain(tab_ref, s, partial):
        _for_each_run_chunk(tab_ref, lambda h, st: copy(h, st, s, 0 if partial else 2 + s).wait(), partial)

    @pl.when(j == 0)
    def _():
        stage_ref[...] = jnp.zeros_like(stage_ref)
        zbuf_ref[...] = jnp.zeros_like(zbuf_ref)

        def clear(blk_tile):
            return pltpu.make_async_copy(zbuf_ref, xs_ref.at[_tile_rows(blk_tile, EXPERT_ROWS)], sems.at[1])

        def each_clear(act):
            for e in range(2 * N_EXPERTS):
                @pl.when(zrow_ref[e] >= 0)
                def _():
                    act(clear(zrow_ref[e]))

            def tail(blk, carry):
                act(clear(blk * EXPERT_ROWS))
                return carry
            lax.fori_loop(zrow_ref[2 * N_EXPERTS], xs_ref.shape[0] // (EXPERT_ROWS * TOKEN_SLABS), tail, 0)

        each_clear(lambda cp: cp.start())
        each_clear(lambda cp: cp.wait())

    @pl.when(j >= 2)
    def _():
        drain(tabpp_ref, slot, partial=False)

    for r in range(rows):
        tile = h_ref[pl.ds(r * TOKEN_SLABS, TOKEN_SLABS), :]
        stage_ref[slot, _tile_rows(p0_ref[0, 0, r], 1), :] = tile
        stage_ref[slot, _tile_rows(p1_ref[0, 0, r], 1), :] = tile

    @pl.when(j >= 1)
    def _():
        drain(tabp_ref, 1 - slot, partial=True)

    _for_each_run_chunk(tabc_ref, lambda h, st: copy(h, st, slot, 0).start(), partial=True)
    _for_each_run_chunk(tabc_ref, lambda h, st: copy(h, st, slot, 2 + slot).start(), partial=False)

    @pl.when(j == last)
    def _():
        drain(tabc_ref, slot, partial=True)
        drain(tabc_ref, slot, partial=False)

        @pl.when(j >= 1)
        def _():
            drain(tabp_ref, 1 - slot, partial=False)


def _disp_call(h2t, plan, zrow, p_rows):
    t = SEQ_TILE
    n_tiles = h2t.shape[0] // (t * TOKEN_SLABS)
    smem = lambda w, imap: pl.BlockSpec((1, 1, w), imap, memory_space=pltpu.SMEM)
    cur = lambda i, z: (i, 0, 0)
    grid_spec = pltpu.PrefetchScalarGridSpec(
        num_scalar_prefetch=1,
        grid=(n_tiles,),
        in_specs=[smem(LANES, cur), smem(LANES, lambda i, z: (jnp.maximum(i - 1, 0), 0, 0)),
                  smem(LANES, lambda i, z: (jnp.maximum(i - 2, 0), 0, 0)),
                  smem(t, cur), smem(t, cur),
                  pl.BlockSpec((t * TOKEN_SLABS, LANES), lambda i, z: (i, 0))],
        out_specs=pl.BlockSpec(memory_space=pl.ANY),
        scratch_shapes=[pltpu.VMEM((2, STAGE_ROWS * TOKEN_SLABS, LANES), F32),
                        pltpu.VMEM((EXPERT_ROWS * TOKEN_SLABS, LANES), F32),
                        pltpu.SemaphoreType.DMA((4,))],
    )
    table = plan["table"]
    return pl.pallas_call(
        _disp_kernel,
        grid_spec=grid_spec,
        out_shape=jax.ShapeDtypeStruct((p_rows * TOKEN_SLABS, LANES), F32),
        compiler_params=_params("arbitrary"),
        name="dispatch_rows",
    )(zrow, table, table, table, plan["pos0"], plan["pos1"], h2t)


def _exp_kernel(layer, be_ref, nu_ref, xs_ref, wg_ref, wu_ref, wd_ref, y_ref,
                wgf_ref, wuf_ref, wdf_ref, wgb_ref, wub_ref, wdb_ref, sems):
    i = pl.program_id(0)
    rows = xs_ref.shape[0] // TOKEN_SLABS
    used = i < nu_ref[0]
    e = be_ref[i]
    new_expert = jnp.logical_or(i == 0, e != be_ref[jnp.maximum(i - 1, 0)])

    def fetch(expert, slot):
        return [pltpu.make_async_copy(w_ref.at[layer, expert], f_ref.at[slot], sems.at[slot])
                for w_ref, f_ref in ((wg_ref, wgf_ref), (wu_ref, wuf_ref), (wd_ref, wdf_ref))]

    @pl.when(jnp.logical_and(used, new_expert))
    def _():
        slot = e % 2

        @pl.when(i == 0)
        def _():
            for cp in fetch(e, slot):
                cp.start()

        for cp in fetch(e, slot):
            cp.wait()

        @pl.when(e + 1 < N_EXPERTS)
        def _():
            for cp in fetch(e + 1, 1 - slot):
                cp.start()

        wgb_ref[...] = wgf_ref[slot].astype(BF16)
        wub_ref[...] = wuf_ref[slot].astype(BF16)
        wdb_ref[...] = wdf_ref[slot].astype(BF16)

    @pl.when(used)
    def _():
        xb = _load_token_tiles(xs_ref, rows).astype(BF16)
        mids = []
        for c0 in range(0, D_EXPERT, EXPERT_COLS):
            gate = jnp.dot(xb, wgb_ref[:, c0:c0 + EXPERT_COLS], preferred_element_type=F32)
            up = jnp.dot(xb, wub_ref[:, c0:c0 + EXPERT_COLS], preferred_element_type=F32)
            mids.append((jax.nn.silu(gate) * up).astype(BF16))
        mid = jnp.concatenate(mids, axis=1)
        _store_token_tiles(y_ref, jnp.dot(mid, wdb_ref[...], preferred_element_type=F32))

    @pl.when(jnp.logical_not(used))
    def _():
        y_ref[...] = jnp.zeros_like(y_ref)


def _exp_call(blk_e, n_used, xs, wg, wu, wd, layer):
    p = xs.shape[0] // TOKEN_SLABS
    r = EXPERT_ROWS
    blk = (r * TOKEN_SLABS, LANES)
    hbm = pl.BlockSpec(memory_space=pl.ANY)
    up_shape, down_shape = (D_MODEL, D_EXPERT), (D_EXPERT, D_MODEL)
    grid_spec = pltpu.PrefetchScalarGridSpec(
        num_scalar_prefetch=2,
        grid=(p // r,),
        in_specs=[pl.BlockSpec(blk, lambda i, be, nu: (jnp.minimum(i, nu[0] - 1), 0)), hbm, hbm, hbm],
        out_specs=pl.BlockSpec(blk, lambda i, be, nu: (i, 0)),
        scratch_shapes=[pltpu.VMEM((2,) + up_shape, F32), pltpu.VMEM((2,) + up_shape, F32),
                        pltpu.VMEM((2,) + down_shape, F32),
                        pltpu.VMEM(up_shape, BF16), pltpu.VMEM(up_shape, BF16), pltpu.VMEM(down_shape, BF16),
                        pltpu.SemaphoreType.DMA((2,))],
    )
    return pl.pallas_call(
        functools.partial(_exp_kernel, layer),
        grid_spec=grid_spec,
        out_shape=jax.ShapeDtypeStruct(xs.shape, F32),
        compiler_params=_params("arbitrary"),
        name="experts",
    )(blk_e, n_used, xs, wg, wu, wd)


def _gather_scratch():
    return [pltpu.VMEM((2, STAGE_ROWS * TOKEN_SLABS, LANES), F32),
            pltpu.VMEM((SEQ_TILE * TOKEN_SLABS, LANES), F32), pltpu.SemaphoreType.DMA((2,))]


def _moe_residual(tabc_ref, tabn_ref, p0_ref, p1_ref, w0_ref, w1_ref, x_ref, mod_ref, ys_ref,
                  stage_ref, moe_ref, sems):
    rows = x_ref.shape[1]
    n = pl.program_id(0) * pl.num_programs(1) + pl.program_id(1)
    last = pl.num_programs(0) * pl.num_programs(1) - 1
    slot = n % 2

    def copy(hbm_tile, stage_tile, s):
        return pltpu.make_async_copy(ys_ref.at[_tile_rows(hbm_tile, RUN_CHUNK)],
                                     stage_ref.at[s, _tile_rows(stage_tile, RUN_CHUNK)], sems.at[s])

    @pl.when(n == 0)
    def _():
        _for_each_run_chunk(tabc_ref, lambda h, s: copy(h, s, 0).start())

    @pl.when(n < last)
    def _():
        _for_each_run_chunk(tabn_ref, lambda h, s: copy(h, s, 1 - slot).start())

    _for_each_run_chunk(tabc_ref, lambda h, s: copy(h, s, slot).wait())

    for r in range(rows):
        y0 = stage_ref[slot, _tile_rows(p0_ref[0, 0, r], 1), :]
        y1 = stage_ref[slot, _tile_rows(p1_ref[0, 0, r], 1), :]
        moe_ref[pl.ds(r * TOKEN_SLABS, TOKEN_SLABS), :] = y0 * w0_ref[0, 0, r] + y1 * w1_ref[0, 0, r]
    return x_ref[0] + mod_ref[0, 5:6, :] * _load_token_tiles(moe_ref, rows)


def _combine_plan_specs(plan, b, steps):
    last = b * steps - 1
    cur = lambda bi, j: (bi * steps + j, 0, 0)
    nxt = lambda bi, j: (jnp.minimum(bi * steps + j + 1, last), 0, 0)
    smem = lambda w, imap: pl.BlockSpec((1, 1, w), imap, memory_space=pltpu.SMEM)
    t = SEQ_TILE
    specs = [smem(LANES, cur), smem(LANES, nxt), smem(t, cur), smem(t, cur), smem(t, cur), smem(t, cur)]
    return specs, (plan["table"], plan["table"], plan["pos0"], plan["pos1"], plan["w0"], plan["w1"])


def _comb_kernel(tabc_ref, tabn_ref, p0_ref, p1_ref, w0_ref, w1_ref, x_ref, mod_ref, ys_ref, o_ref,
                 stage_ref, moe_ref, sems):
    o_ref[0] = _moe_residual(tabc_ref, tabn_ref, p0_ref, p1_ref, w0_ref, w1_ref, x_ref, mod_ref, ys_ref,
                             stage_ref, moe_ref, sems)


def _comb_call(x1, plan, mod, ys):
    b, s, d = x1.shape
    t = SEQ_TILE
    steps = s // t
    plan_specs, plan_args = _combine_plan_specs(plan, b, steps)
    tok = lambda w: pl.BlockSpec((1, t, w), lambda bi, j: (bi, j, 0))
    return pl.pallas_call(
        _comb_kernel,
        grid=(b, steps),
        in_specs=plan_specs + [tok(d), pl.BlockSpec((1, 6, d), lambda bi, j: (bi, 0, 0)),
                               pl.BlockSpec(memory_space=pl.ANY)],
        out_specs=tok(d),
        out_shape=jax.ShapeDtypeStruct((b, s, d), F32),
        scratch_shapes=_gather_scratch(),
        compiler_params=_params("arbitrary", "arbitrary"),
        name="combine_rows",
    )(*plan_args, x1, mod, ys)


def _dispatch_plan(meta_t, starts, counts, n_rows):
    r = EXPERT_ROWS
    t = SEQ_TILE
    cnt = counts[:, 0].astype(jnp.int32)
    start = starts[:, 0].reshape(-1, N_EXPERTS).astype(jnp.int32)
    in_tile = jnp.concatenate([start[1:], cnt[None, :]], axis=0) - start
    padded = ((cnt + RUN_CHUNK + r - 1) // r) * r
    pad_end = jnp.cumsum(padded)
    pad_start = pad_end - padded
    n_tiles = start.shape[0]
    full = in_tile // RUN_CHUNK
    has_part = (in_tile % RUN_CHUNK > 0).astype(jnp.int32)
    part_rank = jnp.cumsum(has_part, axis=1) - has_part
    n_part = jnp.sum(has_part, axis=1, keepdims=True)
    full_off = jnp.cumsum(full, axis=1) - full
    n_piece = n_part + jnp.sum(full, axis=1, keepdims=True)
    row = pad_start[None, :] + start
    k = jnp.arange(MAX_CHUNKS, dtype=jnp.int32)[None, :, None]
    kf = k - n_part[:, :, None]
    own_part = (has_part[:, None, :] > 0) & (part_rank[:, None, :] == k)
    own_full = (full_off[:, None, :] <= kf) & (kf < (full_off + full)[:, None, :])
    piece_row = jnp.sum(jnp.where(own_part, (row + full * RUN_CHUNK)[:, None, :], 0)
                        + jnp.where(own_full, row[:, None, :] + (kf - full_off[:, None, :]) * RUN_CHUNK, 0), axis=2)
    tail = jnp.concatenate([n_piece, n_part, jnp.zeros((n_tiles, LANES - MAX_CHUNKS - 2), jnp.int32)], axis=1)
    table = jnp.concatenate([piece_row, tail], axis=1).astype(jnp.int32)

    experts = jnp.arange(N_EXPERTS, dtype=jnp.int32)[None, :]
    per_token = lambda v: jnp.repeat(v, t, axis=0)
    tok_start, tok_full = per_token(start), per_token(full)
    tok_part = per_token(part_rank)
    tok_fullbase = per_token(n_part + full_off)

    def position(e, rank):
        pick = lambda v: jnp.sum(jnp.where(e[:, None] == experts, v, 0), axis=1)
        local = rank - pick(tok_start)
        chunk, within = local // RUN_CHUNK, local % RUN_CHUNK
        piece = jnp.where(chunk == pick(tok_full), pick(tok_part), pick(tok_fullbase) + chunk)
        return piece * RUN_CHUNK + within

    as_int = lambda row: meta_t[row].astype(jnp.int32)
    per_tile = lambda v: v.reshape(n_tiles, 1, t)
    plan = {"table": table.reshape(n_tiles, 1, LANES),
            "pos0": per_tile(position(as_int(META_E0), as_int(META_R0))),
            "pos1": per_tile(position(as_int(META_E1), as_int(META_R1))),
            "w0": per_tile(meta_t[META_W0]), "w1": per_tile(meta_t[META_W1])}

    n_used = (pad_end[-1:] // r).astype(jnp.int32)
    blk_row = jnp.minimum(jnp.arange(n_rows // r, dtype=jnp.int32), n_used - 1) * r
    blk_e = jnp.sum((pad_end[None, :] <= blk_row[:, None]).astype(jnp.int32), axis=1)
    pad_first = pad_start + (cnt // r) * r
    pad_last = pad_end - r
    zrow = jnp.concatenate([pad_first, jnp.where(pad_last > pad_first, pad_last, -1), n_used]).astype(jnp.int32)
    return plan, blk_e, n_used, zrow


def kernel(x, c, ada_w, ada_b, norm1_g, norm2_g, w_in, pool_w, pool_scale, q_norm_g, k_norm_g, rel_bias,
           w_out, router_group_w, router_group_b, router_expert_w, router_expert_b,
           moe_w_gate, moe_w_up, moe_w_down):
    b, s, d = x.shape
    depth = ada_w.shape[0]
    n = b * s
    assert d == D_MODEL and b <= 8 and s % SEQ_TILE == 0
    spare_blocks = -(-N_EXPERTS * RUN_CHUNK // EXPERT_ROWS)
    p_rows = n * 2 + (N_EXPERTS + spare_blocks) * EXPERT_ROWS

    c_pad = jnp.zeros((8, d), F32).at[:b].set(c)
    mod_all = _ada_call(c_pad, ada_w, ada_b)
    bias = _group_bias(rel_bias)
    head_block = jnp.arange(ATT_WIDTH) // HEAD_DIM
    bd = (head_block[:, None] == head_block[None, :]).astype(BF16)
    sm_scale = HEAD_DIM ** -0.5 * LOG2_E

    moe = None
    for l in range(depth):
        mod = mod_all[l, :b].reshape(b, 6, d)
        qg = (jnp.tile(q_norm_g[l], N_HEADS) * sm_scale).reshape(1, ATT_WIDTH)
        kg = jnp.tile(k_norm_g[l], N_HEADS).reshape(1, ATT_WIDTH)
        outs = _in_call(x, mod, norm1_g[l].reshape(1, d), w_in, pool_w, l,
                        pool_scale[l].reshape(1, POOL_WIDTH), qg, kg, bd, moe)
        if moe is not None:
            x, outs = outs[0], outs[1:]
        pool, q, k, v = outs
        att = _att_call(q, k, v, bias)

        wr = jnp.zeros((d, LANES), F32)
        wr = wr.at[:, ROUTER_GROUP_LANE:ROUTER_GROUP_LANE + N_GROUPS].set(router_group_w[l])
        wr = wr.at[:, ROUTER_EXPERT_LANE:ROUTER_EXPERT_LANE + N_EXPERTS].set(router_expert_w[l])
        wr_hi = wr.astype(BF16)
        wr_lo = (wr - wr_hi.astype(F32)).astype(BF16)
        br = jnp.zeros((1, LANES), F32)
        br = br.at[0, ROUTER_GROUP_LANE:ROUTER_GROUP_LANE + N_GROUPS].set(router_group_b[l])
        br = br.at[0, ROUTER_EXPERT_LANE:ROUTER_EXPERT_LANE + N_EXPERTS].set(router_expert_b[l])
        x1, h2t, meta_t, starts, counts = _out_call(x, pool, att, mod, norm2_g[l].reshape(1, d), w_out, l,
                                                    jnp.concatenate([wr_hi, wr_lo], axis=1), br)

        plan, blk_e, n_used, zrow = _dispatch_plan(meta_t, starts, counts, p_rows)
        xs = _disp_call(h2t, plan, zrow, p_rows)
        ys = _exp_call(blk_e, n_used, xs, moe_w_gate, moe_w_up, moe_w_down, l)
        x = x1
        moe = (plan, mod, ys)
    return _comb_call(x, moe[0], moe[1], moe[2])
```

```python
import functools

import jax
import jax.numpy as jnp
from jax import lax
from jax.experimental import pallas as pl
from jax.experimental.pallas import tpu as pltpu

F32 = jnp.float32
BF16 = jnp.bfloat16

D_MODEL = 1024
CHUNK = 64
POOL_WIDTH = 512
POOL_WINDOWS = (2, 4, 8, 16)
POOL_GW = 128
POOL_HALO = 16
ATT_WIDTH = 512
N_HEADS = 8
HEAD_DIM = 64
LEFT = 512
MAX_REL = 128
IN_WIDTH = 2048
N_GROUPS = 4
EXPERTS_PER_GROUP = 8
N_EXPERTS = 32
D_EXPERT = 512
EPS = 1e-6
NEG_INF = -1e30

LANES = 128
TOKEN_SLABS = D_MODEL // LANES
SEQ_TILE = 512
ATT_TILE = 4096
ATT_GROUP = 4 * CHUNK
ATT_BAND = LEFT + ATT_GROUP
ATT_VARIANTS = LEFT // ATT_GROUP + 1
LOG2_E = 1.4426950408889634
EXPERT_ROWS = 512
EXPERT_COLS = 256
RUN_CHUNK = 16
MAX_CHUNKS = 2 * SEQ_TILE // RUN_CHUNK + N_EXPERTS
STAGE_ROWS = MAX_CHUNKS * RUN_CHUNK
assert MAX_CHUNKS < LANES
VMEM_LIMIT = 56 * 1024 * 1024

META_E0, META_E1, META_W0, META_W1, META_R0, META_R1 = 0, 1, 2, 3, 4, 5
ROUTER_EXPERT_LANE = 0
ROUTER_GROUP_LANE = N_EXPERTS


def _params(*sem):
    return pltpu.CompilerParams(dimension_semantics=sem, vmem_limit_bytes=VMEM_LIMIT)


def _first_step():
    return jnp.logical_and(pl.program_id(0) == 0, pl.program_id(1) == 0)


def _load_token_tiles(ref, rows):
    return jnp.concatenate([ref[pl.ds(s, rows, stride=TOKEN_SLABS), :] for s in range(TOKEN_SLABS)], axis=1)


def _store_token_tiles(ref, val, first_token=0):
    rows = val.shape[0]
    for s in range(TOKEN_SLABS):
        ref[pl.ds(first_token * TOKEN_SLABS + s, rows, stride=TOKEN_SLABS), :] = val[:, s * LANES:(s + 1) * LANES]


def _ada_kernel(c_ref, w_ref, b_ref, o_ref):
    ca = jax.nn.silu(c_ref[...])
    o_ref[0] = jnp.dot(ca, w_ref[0], precision=lax.Precision.HIGHEST,
                       preferred_element_type=F32) + b_ref[0]


def _ada_call(c_pad, ada_w, ada_b):
    depth = ada_w.shape[0]
    tn = 1024
    return pl.pallas_call(
        _ada_kernel,
        grid=(depth, 6 * D_MODEL // tn),
        in_specs=[
            pl.BlockSpec((8, D_MODEL), lambda l, n: (0, 0)),
            pl.BlockSpec((1, D_MODEL, tn), lambda l, n: (l, 0, n)),
            pl.BlockSpec((1, 1, tn), lambda l, n: (l, 0, n)),
        ],
        out_specs=pl.BlockSpec((1, 8, tn), lambda l, n: (l, 0, n)),
        out_shape=jax.ShapeDtypeStruct((depth, 8, 6 * D_MODEL), F32),
        compiler_params=_params("arbitrary", "arbitrary"),
        name="ada_mod",
    )(c_pad, ada_w, ada_b.reshape(depth, 1, 6 * D_MODEL))


def _in_kernel(x_ref, *refs):
    _in_body(x_ref[0], *refs)


def _in_comb_kernel(tabc_ref, tabn_ref, p0_ref, p1_ref, w0_ref, w1_ref, x1_ref, modp_ref, ys_ref,
                    mod_ref, g_ref, w_ref, pw_ref, ps_ref, qg_ref, kg_ref, bd_ref,
                    xn_ref, pool_ref, q_ref, k_ref, v_ref, ext_ref, wb_ref, stage_ref, moe_ref, sems):
    x = _moe_residual(tabc_ref, tabn_ref, p0_ref, p1_ref, w0_ref, w1_ref, x1_ref, modp_ref, ys_ref,
                      stage_ref, moe_ref, sems)
    xn_ref[0] = x
    _in_body(x, mod_ref, g_ref, w_ref, pw_ref, ps_ref, qg_ref, kg_ref, bd_ref,
             pool_ref, q_ref, k_ref, v_ref, ext_ref, wb_ref)


def _in_body(x, mod_ref, g_ref, w_ref, pw_ref, ps_ref, qg_ref, kg_ref, bd_ref,
             pool_ref, q_ref, k_ref, v_ref, ext_ref, wb_ref):
    j = pl.program_id(1)
    t = x.shape[0]

    @pl.when(_first_step())
    def _():
        wb_ref[...] = w_ref[0].astype(BF16)

    ms = jnp.mean(x * x, axis=-1, keepdims=True)
    y = x * lax.rsqrt(ms + EPS) * g_ref[...]
    h = y * (1.0 + mod_ref[0, 1:2, :]) + mod_ref[0, 0:1, :]
    z = jnp.dot(h.astype(BF16), wb_ref[...], preferred_element_type=F32)

    bd = bd_ref[...]
    q = z[:, POOL_WIDTH:POOL_WIDTH + ATT_WIDTH]
    k = z[:, POOL_WIDTH + ATT_WIDTH:POOL_WIDTH + 2 * ATT_WIDTH]
    qss = jnp.dot((q * q).astype(BF16), bd, preferred_element_type=F32)
    kss = jnp.dot((k * k).astype(BF16), bd, preferred_element_type=F32)
    q_ref[0] = (q * lax.rsqrt(qss * (1.0 / HEAD_DIM) + EPS) * qg_ref[...]).astype(BF16)
    k_ref[0] = (k * lax.rsqrt(kss * (1.0 / HEAD_DIM) + EPS) * kg_ref[...]).astype(BF16)
    v_ref[0] = z[:, POOL_WIDTH + 2 * ATT_WIDTH:].astype(BF16)

    @pl.when(j == 0)
    def _():
        ext_ref[0:POOL_HALO, :] = jnp.zeros((POOL_HALO, POOL_WIDTH), F32)

    ext_ref[POOL_HALO:, :] = z[:, :POOL_WIDTH]
    pos = j * t + lax.broadcasted_iota(jnp.int32, (t, 1), 0)
    outs = []
    for gi, win in enumerate(POOL_WINDOWS):
        a = ext_ref[:, gi * POOL_GW:(gi + 1) * POOL_GW]
        s = a
        shift = 1
        while shift < win:
            s = s + pltpu.roll(s, shift, 0)
            shift *= 2
        cnt = jnp.minimum(pos + 1, win).astype(F32)
        pooled = s[POOL_HALO:, :] / cnt - a[POOL_HALO:, :]
        outs.append(jnp.dot(pooled.astype(BF16), pw_ref[0, gi].astype(BF16), preferred_element_type=F32))
    pool_ref[0] = (jnp.concatenate(outs, axis=1) * ps_ref[...]).astype(BF16)
    ext_ref[0:POOL_HALO, :] = ext_ref[t:t + POOL_HALO, :]


def _in_call(x, mod, g, w_in, pool_w, layer, pool_scale, qg, kg, bd, moe=None):
    b, s, d = x.shape
    t = SEQ_TILE
    steps = s // t
    tok = lambda w: pl.BlockSpec((1, t, w), lambda bi, j: (bi, j, 0))
    const2 = lambda shape: pl.BlockSpec(shape, lambda bi, j: (0, 0))
    mod_spec = pl.BlockSpec((1, 6, d), lambda bi, j: (bi, 0, 0))
    out = jax.ShapeDtypeStruct((b, s, ATT_WIDTH), BF16)
    in_specs = [
        mod_spec,
        const2((1, d)),
        pl.BlockSpec((1, d, IN_WIDTH), lambda bi, j: (layer, 0, 0), pipeline_mode=pl.Buffered(1)),
        pl.BlockSpec((1, len(POOL_WINDOWS), POOL_GW, POOL_GW), lambda bi, j: (layer, 0, 0, 0)),
        const2((1, POOL_WIDTH)),
        const2((1, ATT_WIDTH)),
        const2((1, ATT_WIDTH)),
        const2((ATT_WIDTH, ATT_WIDTH)),
    ]
    args = (mod, g, w_in, pool_w, pool_scale, qg, kg, bd)
    out_specs = [tok(POOL_WIDTH), tok(ATT_WIDTH), tok(ATT_WIDTH), tok(ATT_WIDTH)]
    out_shape = [jax.ShapeDtypeStruct((b, s, POOL_WIDTH), BF16), out, out, out]
    scratch = [pltpu.VMEM((t + POOL_HALO, POOL_WIDTH), F32), pltpu.VMEM((d, IN_WIDTH), BF16)]
    if moe is None:
        return pl.pallas_call(
            _in_kernel, grid=(b, steps), in_specs=[tok(d)] + in_specs, out_specs=out_specs, out_shape=out_shape,
            scratch_shapes=scratch, compiler_params=_params("arbitrary", "arbitrary"), name="in_proj",
        )(x, *args)
    plan, mod_prev, ys = moe
    plan_specs, plan_args = _combine_plan_specs(plan, b, steps)
    return pl.pallas_call(
        _in_comb_kernel, grid=(b, steps),
        in_specs=plan_specs + [tok(d), mod_spec, pl.BlockSpec(memory_space=pl.ANY)] + in_specs,
        out_specs=[tok(d)] + out_specs,
        out_shape=[jax.ShapeDtypeStruct((b, s, d), F32)] + out_shape,
        scratch_shapes=scratch + _gather_scratch(),
        compiler_params=_params("arbitrary", "arbitrary"), name="combine_in_proj",
    )(*plan_args, x, mod_prev, ys, *args)


def _att_kernel(q_ref, kp_ref, kc_ref, vp_ref, vc_ref, bias_ref, o_ref):
    i = pl.program_id(2)
    tq = q_ref.shape[1]
    lane = lax.broadcasted_iota(jnp.int32, (1, LANES), 1)
    low_half = lane < HEAD_DIM
    for g in range(tq // ATT_GROUP):
        r0 = g * ATT_GROUP
        qg = q_ref[0, r0:r0 + ATT_GROUP, :]
        if r0 < LEFT:
            kb = jnp.concatenate([kp_ref[0, r0:LEFT, :], kc_ref[0, 0:r0 + ATT_GROUP, :]], axis=0)
            vb = jnp.concatenate([vp_ref[0, r0:LEFT, :], vc_ref[0, 0:r0 + ATT_GROUP, :]], axis=0)
            variant = jnp.where(i == 0, g + 1, 0)
        else:
            kb = kc_ref[0, r0 - LEFT:r0 + ATT_GROUP, :]
            vb = vc_ref[0, r0 - LEFT:r0 + ATT_GROUP, :]
            variant = 0
        zero = jnp.zeros_like(qg)
        qs = jnp.concatenate([jnp.where(low_half, qg, zero), jnp.where(low_half, zero, qg)], axis=0)
        s = lax.dot_general(qs, kb, (((1,), (1,)), ((), ())), preferred_element_type=F32)
        s = s + bias_ref[variant]
        m = jnp.max(s, axis=-1, keepdims=True)
        e = jnp.exp2(s - m)
        l = jnp.sum(e, axis=-1, keepdims=True)
        pv = jnp.dot(e.astype(BF16), vb, preferred_element_type=F32) / l
        o_ref[0, r0:r0 + ATT_GROUP, :] = jnp.where(low_half, pv[:ATT_GROUP], pv[ATT_GROUP:]).astype(BF16)


def _att_call(q, k, v, bias):
    b, s, _ = q.shape
    tq = min(ATT_TILE, s)
    assert s % tq == 0 and tq >= LEFT and tq % ATT_GROUP == 0
    lb = tq // LEFT
    cur = pl.BlockSpec((1, tq, LANES), lambda bi, hp, i: (bi, i, hp))
    prev = pl.BlockSpec((1, LEFT, LANES), lambda bi, hp, i: (bi, jnp.maximum(i * lb - 1, 0), hp))
    return pl.pallas_call(
        _att_kernel,
        grid=(b, N_HEADS // 2, s // tq),
        in_specs=[cur, prev, cur, prev, cur,
                  pl.BlockSpec((ATT_VARIANTS, 2 * ATT_GROUP, ATT_BAND), lambda bi, hp, i: (0, hp, 0))],
        out_specs=cur,
        out_shape=jax.ShapeDtypeStruct((b, s, ATT_WIDTH), BF16),
        compiler_params=_params("arbitrary", "arbitrary", "arbitrary"),
        name="chunk_attn",
    )(q, k, k, v, v, bias)


def _group_bias(rel_bias):
    qi = jnp.arange(ATT_GROUP)[:, None]
    kj = jnp.arange(ATT_BAND)[None, :]
    first = (qi // CHUNK) * CHUNK
    visible = (kj >= first) & (kj < first + LEFT + CHUNK)
    lo = ATT_GROUP - 1
    assert lo >= MAX_REL and ATT_BAND - 1 >= MAX_REL
    tab = rel_bias.astype(F32)
    by_rel = jnp.concatenate([jnp.repeat(tab[:, :1], lo - MAX_REL, axis=1), tab,
                              jnp.repeat(tab[:, -1:], ATT_BAND - 1 - MAX_REL, axis=1)], axis=1)
    rev = by_rel[:, ::-1] * LOG2_E
    length = lo + ATT_BAND
    period = jnp.concatenate([rev, jnp.zeros((N_HEADS, 1), F32)], axis=1)
    flat = jnp.tile(period, (1, ATT_GROUP))[:, :ATT_GROUP * length]
    bias = flat.reshape(N_HEADS, ATT_GROUP, length)[:, :, lo:lo + ATT_BAND]
    tables = []
    for variant in range(ATT_VARIANTS):
        ok = visible if variant == 0 else visible & (kj >= LEFT - (variant - 1) * ATT_GROUP)
        tables.append(jnp.where(ok[None], bias, NEG_INF))
    return jnp.stack(tables, axis=0).reshape(ATT_VARIANTS, N_HEADS * ATT_GROUP, ATT_BAND)


def _out_kernel(x_ref, pool_ref, att_ref, mod_ref, g_ref, wo_ref, wr_ref, br_ref,
                x1_ref, h2_ref, metat_ref, start_ref, cnt_ref, carry_ref, wob_ref, logit_ref):
    n = pl.program_id(0)
    t = x_ref.shape[1]

    @pl.when(n == 0)
    def _():
        carry_ref[...] = jnp.zeros_like(carry_ref)
        logit_ref[...] = jnp.zeros_like(logit_ref)
        wob_ref[...] = wo_ref[0].astype(BF16)

    _route_tile(jnp.where(n > 0, 1.0, 0.0), logit_ref[...], metat_ref, start_ref, cnt_ref, carry_ref)

    mix = (jnp.dot(pool_ref[0], wob_ref[:POOL_WIDTH, :], preferred_element_type=F32)
           + jnp.dot(att_ref[0], wob_ref[POOL_WIDTH:, :], preferred_element_type=F32))
    x1 = x_ref[0] + mod_ref[0, 2:3, :] * mix
    x1_ref[0] = x1
    ms = jnp.mean(x1 * x1, axis=-1, keepdims=True)
    y = x1 * lax.rsqrt(ms + EPS) * g_ref[...]
    h2 = y * (1.0 + mod_ref[0, 4:5, :]) + mod_ref[0, 3:4, :]

    _store_token_tiles(h2_ref, h2)
    hb = h2.astype(BF16)

    h_lo = (h2 - hb.astype(F32)).astype(BF16)
    r = (jnp.dot(hb, wr_ref[...], preferred_element_type=F32)
         + jnp.dot(h_lo, wr_ref[...], preferred_element_type=F32))
    logit_ref[...] = r[:, :LANES] + r[:, LANES:] + br_ref[...]


def _route_tile(live, logits, metat_ref, start_ref, cnt_ref, carry_ref):
    t = logits.shape[0]
    start_ref[...] = carry_ref[...]

    lt = jnp.transpose(logits)
    sub = lax.broadcasted_iota(jnp.int32, (EXPERTS_PER_GROUP, t), 0).astype(F32)
    ninf = jnp.float32(-jnp.inf)
    first_of = lambda hit: jnp.min(jnp.where(hit, sub, float(EXPERTS_PER_GROUP)), axis=0, keepdims=True)

    grp = lt[ROUTER_GROUP_LANE:ROUTER_GROUP_LANE + EXPERTS_PER_GROUP, :]
    gvalid = sub < N_GROUPS
    gl = jnp.where(gvalid, grp, ninf)
    gmax = jnp.max(gl, axis=0, keepdims=True)
    gsum = jnp.sum(jnp.where(gvalid, jnp.exp(grp - gmax), 0.0), axis=0, keepdims=True)
    g_p = 1.0 / gsum
    g_idx = first_of(gl == gmax)

    m1 = m2 = i1 = i2 = jnp.zeros((1, t), F32)
    for g in range(N_GROUPS):
        eg = lt[ROUTER_EXPERT_LANE + g * EXPERTS_PER_GROUP:ROUTER_EXPERT_LANE + (g + 1) * EXPERTS_PER_GROUP, :]
        a1 = jnp.max(eg, axis=0, keepdims=True)
        j1 = first_of(eg == a1)
        eg2 = jnp.where(sub == j1, ninf, eg)
        a2 = jnp.max(eg2, axis=0, keepdims=True)
        j2 = first_of(eg2 == a2)
        pick = g_idx == g
        m1 = jnp.where(pick, a1, m1)
        m2 = jnp.where(pick, a2, m2)
        i1 = jnp.where(pick, j1 + g * EXPERTS_PER_GROUP, i1)
        i2 = jnp.where(pick, j2 + g * EXPERTS_PER_GROUP, i2)
    e2 = jnp.exp(m2 - m1)
    w1 = g_p / (1.0 + e2)
    w2 = g_p * e2 / (1.0 + e2)

    expert = lax.broadcasted_iota(jnp.int32, (N_EXPERTS, t), 0).astype(F32)
    hot = (expert == i1) | (expert == i2)
    earlier = (lax.broadcasted_iota(jnp.int32, (t, t), 0)
               < lax.broadcasted_iota(jnp.int32, (t, t), 1)).astype(BF16)
    before = jnp.dot(hot.astype(BF16), earlier, preferred_element_type=F32) + carry_ref[:, 0:1]
    r1 = jnp.sum(jnp.where(expert == i1, before, 0.0), axis=0, keepdims=True)
    r2 = jnp.sum(jnp.where(expert == i2, before, 0.0), axis=0, keepdims=True)
    carry_ref[...] = carry_ref[...] + live * jnp.sum(hot.astype(F32), axis=1, keepdims=True)
    cnt_ref[...] = carry_ref[...]

    fields = {META_E0: i1, META_E1: i2, META_W0: w1, META_W1: w2, META_R0: r1, META_R1: r2}
    metat_ref[...] = jnp.concatenate([fields.get(k, jnp.zeros((1, t), F32)) for k in range(8)], axis=0)


def _out_call(x, pool, att, mod, g, w_out, layer, wr, br):
    b, s, d = x.shape
    t = SEQ_TILE
    steps = s // t
    tiles = b * steps
    proj = lambda n: jnp.minimum(n, tiles - 1)
    routed = lambda n: jnp.maximum(n - 1, 0)
    tok = lambda w: pl.BlockSpec((1, t, w), lambda n: (proj(n) // steps, proj(n) % steps, 0))
    const2 = lambda shape: pl.BlockSpec(shape, lambda n: (0, 0))
    return pl.pallas_call(
        _out_kernel,
        grid=(tiles + 1,),
        in_specs=[tok(d), tok(POOL_WIDTH), tok(ATT_WIDTH),
                  pl.BlockSpec((1, 6, d), lambda n: (proj(n) // steps, 0, 0)),
                  const2((1, d)),
                  pl.BlockSpec((1, d, d), lambda n: (layer, 0, 0), pipeline_mode=pl.Buffered(1)),
                  const2((d, 2 * LANES)), const2((1, LANES))],
        out_specs=[tok(d), pl.BlockSpec((t * TOKEN_SLABS, LANES), lambda n: (proj(n), 0)),
                   pl.BlockSpec((8, t), lambda n: (0, routed(n))),
                   pl.BlockSpec((N_EXPERTS, LANES), lambda n: (routed(n), 0)),
                   const2((N_EXPERTS, LANES))],
        out_shape=[jax.ShapeDtypeStruct((b, s, d), F32),
                   jax.ShapeDtypeStruct((b * s * TOKEN_SLABS, LANES), F32),
                   jax.ShapeDtypeStruct((8, b * s), F32),
                   jax.ShapeDtypeStruct((tiles * N_EXPERTS, LANES), F32),
                   jax.ShapeDtypeStruct((N_EXPERTS, LANES), F32)],
        scratch_shapes=[pltpu.VMEM((N_EXPERTS, LANES), F32), pltpu.VMEM((d, d), BF16),
                        pltpu.VMEM((t, LANES), F32)],
        compiler_params=_params("arbitrary"),
        name="out_proj_router",
    )(x, pool, att, mod, g, w_out, wr, br)


def _tile_rows(first_tile, n_tiles):
    return pl.ds(pl.multiple_of(first_tile * TOKEN_SLABS, TOKEN_SLABS), n_tiles * TOKEN_SLABS)


def _for_each_run_chunk(tab_ref, fn):
    def piece(k, carry):
        fn(tab_ref[0, 0, k], k * RUN_CHUNK)
        return carry
    lax.fori_loop(0, tab_ref[0, 0, MAX_CHUNKS], piece, 0)


def _disp_kernel(zrow_ref, tabc_ref, tabp_ref, p0_ref, p1_ref, h_ref, xs_ref, stage_ref, zbuf_ref, sems):
    j = pl.program_id(0)
    rows = h_ref.shape[0] // TOKEN_SLABS
    slot = j % 2

    @pl.when(j == 0)
    def _():
        stage_ref[...] = jnp.zeros_like(stage_ref)
        zbuf_ref[...] = jnp.zeros_like(zbuf_ref)

        def clear(blk_tile):
            return pltpu.make_async_copy(zbuf_ref, xs_ref.at[_tile_rows(blk_tile, EXPERT_ROWS)], sems.at[1])

        def each_clear(act):
            for e in range(2 * N_EXPERTS):
                @pl.when(zrow_ref[e] >= 0)
                def _():
                    act(clear(zrow_ref[e]))

            def tail(blk, carry):
                act(clear(blk * EXPERT_ROWS))
                return carry
            lax.fori_loop(zrow_ref[2 * N_EXPERTS], xs_ref.shape[0] // (EXPERT_ROWS * TOKEN_SLABS), tail, 0)

        each_clear(lambda cp: cp.start())
        each_clear(lambda cp: cp.wait())

    for r in range(rows):
        tile = h_ref[pl.ds(r * TOKEN_SLABS, TOKEN_SLABS), :]
        stage_ref[slot, _tile_rows(p0_ref[0, 0, r], 1), :] = tile
        stage_ref[slot, _tile_rows(p1_ref[0, 0, r], 1), :] = tile

    def copy(hbm_tile, stage_tile, s):
        return pltpu.make_async_copy(stage_ref.at[s, _tile_rows(stage_tile, RUN_CHUNK)],
                                     xs_ref.at[_tile_rows(hbm_tile, RUN_CHUNK)], sems.at[0])

    @pl.when(j > 0)
    def _():
        _for_each_run_chunk(tabp_ref, lambda h, s: copy(h, s, 1 - slot).wait())

    _for_each_run_chunk(tabc_ref, lambda h, s: copy(h, s, slot).start())

    @pl.when(j == pl.num_programs(0) - 1)
    def _():
        _for_each_run_chunk(tabc_ref, lambda h, s: copy(h, s, slot).wait())


def _disp_call(h2t, plan, zrow, p_rows):
    t = SEQ_TILE
    n_tiles = h2t.shape[0] // (t * TOKEN_SLABS)
    smem = lambda w, imap: pl.BlockSpec((1, 1, w), imap, memory_space=pltpu.SMEM)
    cur = lambda i, z: (i, 0, 0)
    grid_spec = pltpu.PrefetchScalarGridSpec(
        num_scalar_prefetch=1,
        grid=(n_tiles,),
        in_specs=[smem(LANES, cur), smem(LANES, lambda i, z: (jnp.maximum(i - 1, 0), 0, 0)),
                  smem(t, cur), smem(t, cur),
                  pl.BlockSpec((t * TOKEN_SLABS, LANES), lambda i, z: (i, 0))],
        out_specs=pl.BlockSpec(memory_space=pl.ANY),
        scratch_shapes=[pltpu.VMEM((2, STAGE_ROWS * TOKEN_SLABS, LANES), F32),
                        pltpu.VMEM((EXPERT_ROWS * TOKEN_SLABS, LANES), F32),
                        pltpu.SemaphoreType.DMA((2,))],
    )
    return pl.pallas_call(
        _disp_kernel,
        grid_spec=grid_spec,
        out_shape=jax.ShapeDtypeStruct((p_rows * TOKEN_SLABS, LANES), F32),
        compiler_params=_params("arbitrary"),
        name="dispatch_rows",
    )(zrow, plan["table"], plan["table"], plan["pos0"], plan["pos1"], h2t)


def _exp_kernel(layer, be_ref, nu_ref, xs_ref, wg_ref, wu_ref, wd_ref, y_ref,
                wgf_ref, wuf_ref, wdf_ref, wgb_ref, wub_ref, wdb_ref, sems):
    i = pl.program_id(0)
    rows = xs_ref.shape[0] // TOKEN_SLABS
    used = i < nu_ref[0]
    e = be_ref[i]
    new_expert = jnp.logical_or(i == 0, e != be_ref[jnp.maximum(i - 1, 0)])

    def fetch(expert, slot):
        return [pltpu.make_async_copy(w_ref.at[layer, expert], f_ref.at[slot], sems.at[slot])
                for w_ref, f_ref in ((wg_ref, wgf_ref), (wu_ref, wuf_ref), (wd_ref, wdf_ref))]

    @pl.when(jnp.logical_and(used, new_expert))
    def _():
        slot = e % 2

        @pl.when(i == 0)
        def _():
            for cp in fetch(e, slot):
                cp.start()

        for cp in fetch(e, slot):
            cp.wait()

        @pl.when(e + 1 < N_EXPERTS)
        def _():
            for cp in fetch(e + 1, 1 - slot):
                cp.start()

        wgb_ref[...] = wgf_ref[slot].astype(BF16)
        wub_ref[...] = wuf_ref[slot].astype(BF16)
        wdb_ref[...] = wdf_ref[slot].astype(BF16)

    @pl.when(used)
    def _():
        xb = _load_token_tiles(xs_ref, rows).astype(BF16)
        mids = []
        for c0 in range(0, D_EXPERT, EXPERT_COLS):
            gate = jnp.dot(xb, wgb_ref[:, c0:c0 + EXPERT_COLS], preferred_element_type=F32)
            up = jnp.dot(xb, wub_ref[:, c0:c0 + EXPERT_COLS], preferred_element_type=F32)
            mids.append((jax.nn.silu(gate) * up).astype(BF16))
        mid = jnp.concatenate(mids, axis=1)
        _store_token_tiles(y_ref, jnp.dot(mid, wdb_ref[...], preferred_element_type=F32))

    @pl.when(jnp.logical_not(used))
    def _():
        y_ref[...] = jnp.zeros_like(y_ref)


def _exp_call(blk_e, n_used, xs, wg, wu, wd, layer):
    p = xs.shape[0] // TOKEN_SLABS
    r = EXPERT_ROWS
    blk = (r * TOKEN_SLABS, LANES)
    hbm = pl.BlockSpec(memory_space=pl.ANY)
    up_shape, down_shape = (D_MODEL, D_EXPERT), (D_EXPERT, D_MODEL)
    grid_spec = pltpu.PrefetchScalarGridSpec(
        num_scalar_prefetch=2,
        grid=(p // r,),
        in_specs=[pl.BlockSpec(blk, lambda i, be, nu: (jnp.minimum(i, nu[0] - 1), 0)), hbm, hbm, hbm],
        out_specs=pl.BlockSpec(blk, lambda i, be, nu: (i, 0)),
        scratch_shapes=[pltpu.VMEM((2,) + up_shape, F32), pltpu.VMEM((2,) + up_shape, F32),
                        pltpu.VMEM((2,) + down_shape, F32),
                        pltpu.VMEM(up_shape, BF16), pltpu.VMEM(up_shape, BF16), pltpu.VMEM(down_shape, BF16),
                        pltpu.SemaphoreType.DMA((2,))],
    )
    return pl.pallas_call(
        functools.partial(_exp_kernel, layer),
        grid_spec=grid_spec,
        out_shape=jax.ShapeDtypeStruct(xs.shape, F32),
        compiler_params=_params("arbitrary"),
        name="experts",
    )(blk_e, n_used, xs, wg, wu, wd)


def _gather_scratch():
    return [pltpu.VMEM((2, STAGE_ROWS * TOKEN_SLABS, LANES), F32),
            pltpu.VMEM((SEQ_TILE * TOKEN_SLABS, LANES), F32), pltpu.SemaphoreType.DMA((2,))]


def _moe_residual(tabc_ref, tabn_ref, p0_ref, p1_ref, w0_ref, w1_ref, x_ref, mod_ref, ys_ref,
                  stage_ref, moe_ref, sems):
    rows = x_ref.shape[1]
    n = pl.program_id(0) * pl.num_programs(1) + pl.program_id(1)
    last = pl.num_programs(0) * pl.num_programs(1) - 1
    slot = n % 2

    def copy(hbm_tile, stage_tile, s):
        return pltpu.make_async_copy(ys_ref.at[_tile_rows(hbm_tile, RUN_CHUNK)],
                                     stage_ref.at[s, _tile_rows(stage_tile, RUN_CHUNK)], sems.at[s])

    @pl.when(n == 0)
    def _():
        _for_each_run_chunk(tabc_ref, lambda h, s: copy(h, s, 0).start())

    @pl.when(n < last)
    def _():
        _for_each_run_chunk(tabn_ref, lambda h, s: copy(h, s, 1 - slot).start())

    _for_each_run_chunk(tabc_ref, lambda h, s: copy(h, s, slot).wait())

    for r in range(rows):
        y0 = stage_ref[slot, _tile_rows(p0_ref[0, 0, r], 1), :]
        y1 = stage_ref[slot, _tile_rows(p1_ref[0, 0, r], 1), :]
        moe_ref[pl.ds(r * TOKEN_SLABS, TOKEN_SLABS), :] = y0 * w0_ref[0, 0, r] + y1 * w1_ref[0, 0, r]
    return x_ref[0] + mod_ref[0, 5:6, :] * _load_token_tiles(moe_ref, rows)


def _combine_plan_specs(plan, b, steps):
    last = b * steps - 1
    cur = lambda bi, j: (bi * steps + j, 0, 0)
    nxt = lambda bi, j: (jnp.minimum(bi * steps + j + 1, last), 0, 0)
    smem = lambda w, imap: pl.BlockSpec((1, 1, w), imap, memory_space=pltpu.SMEM)
    t = SEQ_TILE
    specs = [smem(LANES, cur), smem(LANES, nxt), smem(t, cur), smem(t, cur), smem(t, cur), smem(t, cur)]
    return specs, (plan["table"], plan["table"], plan["pos0"], plan["pos1"], plan["w0"], plan["w1"])


def _comb_kernel(tabc_ref, tabn_ref, p0_ref, p1_ref, w0_ref, w1_ref, x_ref, mod_ref, ys_ref, o_ref,
                 stage_ref, moe_ref, sems):
    o_ref[0] = _moe_residual(tabc_ref, tabn_ref, p0_ref, p1_ref, w0_ref, w1_ref, x_ref, mod_ref, ys_ref,
                             stage_ref, moe_ref, sems)


def _comb_call(x1, plan, mod, ys):
    b, s, d = x1.shape
    t = SEQ_TILE
    steps = s // t
    plan_specs, plan_args = _combine_plan_specs(plan, b, steps)
    tok = lambda w: pl.BlockSpec((1, t, w), lambda bi, j: (bi, j, 0))
    return pl.pallas_call(
        _comb_kernel,
        grid=(b, steps),
        in_specs=plan_specs + [tok(d), pl.BlockSpec((1, 6, d), lambda bi, j: (bi, 0, 0)),
                               pl.BlockSpec(memory_space=pl.ANY)],
        out_specs=tok(d),
        out_shape=jax.ShapeDtypeStruct((b, s, d), F32),
        scratch_shapes=_gather_scratch(),
        compiler_params=_params("arbitrary", "arbitrary"),
        name="combine_rows",
    )(*plan_args, x1, mod, ys)


def _dispatch_plan(meta_t, starts, counts, n_rows):
    r = EXPERT_ROWS
    t = SEQ_TILE
    cnt = counts[:, 0].astype(jnp.int32)
    start = starts[:, 0].reshape(-1, N_EXPERTS).astype(jnp.int32)
    in_tile = jnp.concatenate([start[1:], cnt[None, :]], axis=0) - start
    chunks = (in_tile + RUN_CHUNK - 1) // RUN_CHUNK
    span = chunks * RUN_CHUNK
    offset = jnp.cumsum(span, axis=1) - span
    padded = ((cnt + RUN_CHUNK + r - 1) // r) * r
    pad_end = jnp.cumsum(padded)
    pad_start = pad_end - padded
    n_tiles = start.shape[0]
    piece_end = jnp.cumsum(chunks, axis=1)
    k = jnp.arange(MAX_CHUNKS, dtype=jnp.int32)[None, :, None]
    owner = (piece_end[:, None, :] > k) & (piece_end[:, None, :] - chunks[:, None, :] <= k)
    first = (pad_start[None, :] + start - offset)[:, None, :] + k * RUN_CHUNK
    piece_row = jnp.sum(jnp.where(owner, first, 0), axis=2)
    table = jnp.concatenate([piece_row, jnp.broadcast_to(piece_end[:, -1:], (n_tiles, LANES - MAX_CHUNKS))],
                            axis=1).astype(jnp.int32)

    experts = jnp.arange(N_EXPERTS, dtype=jnp.int32)[None, :]
    base = jnp.repeat(offset - start, t, axis=0)

    def position(e, rank):
        return jnp.sum(jnp.where(e[:, None] == experts, base, 0), axis=1) + rank

    as_int = lambda row: meta_t[row].astype(jnp.int32)
    per_tile = lambda v: v.reshape(n_tiles, 1, t)
    plan = {"table": table.reshape(n_tiles, 1, LANES),
            "pos0": per_tile(position(as_int(META_E0), as_int(META_R0))),
            "pos1": per_tile(position(as_int(META_E1), as_int(META_R1))),
            "w0": per_tile(meta_t[META_W0]), "w1": per_tile(meta_t[META_W1])}

    n_used = (pad_end[-1:] // r).astype(jnp.int32)
    blk_row = jnp.minimum(jnp.arange(n_rows // r, dtype=jnp.int32), n_used - 1) * r
    blk_e = jnp.sum((pad_end[None, :] <= blk_row[:, None]).astype(jnp.int32), axis=1)
    pad_first = pad_start + (cnt // r) * r
    pad_last = pad_end - r
    zrow = jnp.concatenate([pad_first, jnp.where(pad_last > pad_first, pad_last, -1), n_used]).astype(jnp.int32)
    return plan, blk_e, n_used, zrow


def kernel(x, c, ada_w, ada_b, norm1_g, norm2_g, w_in, pool_w, pool_scale, q_norm_g, k_norm_g, rel_bias,
           w_out, router_group_w, router_group_b, router_expert_w, router_expert_b,
           moe_w_gate, moe_w_up, moe_w_down):
    b, s, d = x.shape
    depth = ada_w.shape[0]
    n = b * s
    assert d == D_MODEL and b <= 8 and s % SEQ_TILE == 0
    spare_blocks = -(-N_EXPERTS * RUN_CHUNK // EXPERT_ROWS)
    p_rows = n * 2 + (N_EXPERTS + spare_blocks) * EXPERT_ROWS

    c_pad = jnp.zeros((8, d), F32).at[:b].set(c)
    mod_all = _ada_call(c_pad, ada_w, ada_b)
    bias = _group_bias(rel_bias)
    head_block = jnp.arange(ATT_WIDTH) // HEAD_DIM
    bd = (head_block[:, None] == head_block[None, :]).astype(BF16)
    sm_scale = HEAD_DIM ** -0.5 * LOG2_E

    moe = None
    for l in range(depth):
        mod = mod_all[l, :b].reshape(b, 6, d)
        qg = (jnp.tile(q_norm_g[l], N_HEADS) * sm_scale).reshape(1, ATT_WIDTH)
        kg = jnp.tile(k_norm_g[l], N_HEADS).reshape(1, ATT_WIDTH)
        outs = _in_call(x, mod, norm1_g[l].reshape(1, d), w_in, pool_w, l,
                        pool_scale[l].reshape(1, POOL_WIDTH), qg, kg, bd, moe)
        if moe is not None:
            x, outs = outs[0], outs[1:]
        pool, q, k, v = outs
        att = _att_call(q, k, v, bias)

        wr = jnp.zeros((d, LANES), F32)
        wr = wr.at[:, ROUTER_GROUP_LANE:ROUTER_GROUP_LANE + N_GROUPS].set(router_group_w[l])
        wr = wr.at[:, ROUTER_EXPERT_LANE:ROUTER_EXPERT_LANE + N_EXPERTS].set(router_expert_w[l])
        wr_hi = wr.astype(BF16)
        wr_lo = (wr - wr_hi.astype(F32)).astype(BF16)
        br = jnp.zeros((1, LANES), F32)
        br = br.at[0, ROUTER_GROUP_LANE:ROUTER_GROUP_LANE + N_GROUPS].set(router_group_b[l])
        br = br.at[0, ROUTER_EXPERT_LANE:ROUTER_EXPERT_LANE + N_EXPERTS].set(router_expert_b[l])
        x1, h2t, meta_t, starts, counts = _out_call(x, pool, att, mod, norm2_g[l].reshape(1, d), w_out, l,
                                                    jnp.concatenate([wr_hi, wr_lo], axis=1), br)

        plan, blk_e, n_used, zrow = _dispatch_plan(meta_t, starts, counts, p_rows)
        xs = _disp_call(h2t, plan, zrow, p_rows)
        ys = _exp_call(blk_e, n_used, xs, moe_w_gate, moe_w_up, moe_w_down, l)
        x = x1
        moe = (plan, mod, ys)
    return _comb_call(x, moe[0], moe[1], moe[2])
```

```python
import functools

import jax
import jax.numpy as jnp
from jax import lax
from jax.experimental import pallas as pl
from jax.experimental.pallas import tpu as pltpu

F32 = jnp.float32
BF16 = jnp.bfloat16

D_MODEL = 1024
CHUNK = 64
POOL_WIDTH = 512
POOL_WINDOWS = (2, 4, 8, 16)
POOL_GW = 128
POOL_HALO = 16
ATT_WIDTH = 512
N_HEADS = 8
HEAD_DIM = 64
LEFT = 512
MAX_REL = 128
IN_WIDTH = 2048
N_GROUPS = 4
EXPERTS_PER_GROUP = 8
N_EXPERTS = 32
D_EXPERT = 512
EPS = 1e-6
NEG_INF = -1e30

LANES = 128
TOKEN_SLABS = D_MODEL // LANES
SEQ_TILE = 512
ATT_TILE = 4096
ATT_GROUP = 4 * CHUNK
ATT_BAND = LEFT + ATT_GROUP
ATT_VARIANTS = LEFT // ATT_GROUP + 1
LOG2_E = 1.4426950408889634
EXPERT_ROWS = 512
EXPERT_COLS = 256
RUN_CHUNK = 16
MAX_CHUNKS = 2 * SEQ_TILE // RUN_CHUNK + N_EXPERTS
STAGE_ROWS = MAX_CHUNKS * RUN_CHUNK
assert MAX_CHUNKS < LANES
VMEM_LIMIT = 56 * 1024 * 1024

META_E0, META_E1, META_W0, META_W1, META_R0, META_R1 = 0, 1, 2, 3, 4, 5
ROUTER_EXPERT_LANE = 0
ROUTER_GROUP_LANE = N_EXPERTS


def _params(*sem):
    return pltpu.CompilerParams(dimension_semantics=sem, vmem_limit_bytes=VMEM_LIMIT)


def _load_token_tiles(ref, rows):
    return jnp.concatenate([ref[pl.ds(s, rows, stride=TOKEN_SLABS), :] for s in range(TOKEN_SLABS)], axis=1)


def _store_token_tiles(ref, val, first_token=0):
    rows = val.shape[0]
    for s in range(TOKEN_SLABS):
        ref[pl.ds(first_token * TOKEN_SLABS + s, rows, stride=TOKEN_SLABS), :] = val[:, s * LANES:(s + 1) * LANES]


def _ada_kernel(c_ref, w_ref, b_ref, o_ref):
    ca = jax.nn.silu(c_ref[...])
    o_ref[0] = jnp.dot(ca, w_ref[0], precision=lax.Precision.HIGHEST,
                       preferred_element_type=F32) + b_ref[0]


def _ada_call(c_pad, ada_w, ada_b):
    depth = ada_w.shape[0]
    tn = 1024
    return pl.pallas_call(
        _ada_kernel,
        grid=(depth, 6 * D_MODEL // tn),
        in_specs=[
            pl.BlockSpec((8, D_MODEL), lambda l, n: (0, 0)),
            pl.BlockSpec((1, D_MODEL, tn), lambda l, n: (l, 0, n)),
            pl.BlockSpec((1, 1, tn), lambda l, n: (l, 0, n)),
        ],
        out_specs=pl.BlockSpec((1, 8, tn), lambda l, n: (l, 0, n)),
        out_shape=jax.ShapeDtypeStruct((depth, 8, 6 * D_MODEL), F32),
        compiler_params=_params("arbitrary", "arbitrary"),
        name="ada_mod",
    )(c_pad, ada_w, ada_b.reshape(depth, 1, 6 * D_MODEL))


def _in_kernel(steps, x_ref, *refs):
    _in_body(steps, x_ref[0], *refs)


def _in_comb_kernel(steps, tabc_ref, tabn_ref, p0_ref, p1_ref, w0_ref, w1_ref, x1_ref, modp_ref, ys_ref,
                    mod_ref, g_ref, w_ref, pw_ref, ps_ref, qg_ref, kg_ref, bd_ref,
                    xn_ref, pool_ref, q_ref, k_ref, v_ref, ext_ref, wb_ref, z_ref, stage_ref, moe_ref, sems):
    n = pl.program_id(0)
    tiles = pl.num_programs(0) - 1
    x = _moe_residual(n, tiles - 1, tabc_ref, tabn_ref, p0_ref, p1_ref, w0_ref, w1_ref, x1_ref, modp_ref,
                      ys_ref, stage_ref, moe_ref, sems)

    @pl.when(n < tiles)
    def _():
        xn_ref[0] = x

    _in_body(steps, x, mod_ref, g_ref, w_ref, pw_ref, ps_ref, qg_ref, kg_ref, bd_ref,
             pool_ref, q_ref, k_ref, v_ref, ext_ref, wb_ref, z_ref)


def _in_body(steps, x, mod_ref, g_ref, w_ref, pw_ref, ps_ref, qg_ref, kg_ref, bd_ref,
             pool_ref, q_ref, k_ref, v_ref, ext_ref, wb_ref, z_ref):
    n = pl.program_id(0)

    @pl.when(n == 0)
    def _():
        wb_ref[...] = w_ref[0].astype(BF16)
        z_ref[...] = jnp.zeros_like(z_ref)
        ext_ref[...] = jnp.zeros_like(ext_ref)

    def step(slot):
        _in_finish(z_ref.at[1 - slot], jnp.maximum(n - 1, 0) % steps, pw_ref, ps_ref, qg_ref, kg_ref, bd_ref,
                   pool_ref, q_ref, k_ref, v_ref, ext_ref)
        ms = jnp.mean(x * x, axis=-1, keepdims=True)
        y = x * lax.rsqrt(ms + EPS) * g_ref[...]
        h = y * (1.0 + mod_ref[0, 1:2, :]) + mod_ref[0, 0:1, :]
        z_ref[slot] = jnp.dot(h.astype(BF16), wb_ref[...], preferred_element_type=F32)

    for slot in range(2):
        pl.when(n % 2 == slot)(functools.partial(step, slot))


def _in_finish(z_ref, j, pw_ref, ps_ref, qg_ref, kg_ref, bd_ref, pool_ref, q_ref, k_ref, v_ref, ext_ref):
    t = z_ref.shape[0]
    bd = bd_ref[...]
    q = z_ref[:, POOL_WIDTH:POOL_WIDTH + ATT_WIDTH]
    k = z_ref[:, POOL_WIDTH + ATT_WIDTH:POOL_WIDTH + 2 * ATT_WIDTH]
    qss = jnp.dot((q * q).astype(BF16), bd, preferred_element_type=F32)
    kss = jnp.dot((k * k).astype(BF16), bd, preferred_element_type=F32)
    q_ref[0] = (q * lax.rsqrt(qss * (1.0 / HEAD_DIM) + EPS) * qg_ref[...]).astype(BF16)
    k_ref[0] = (k * lax.rsqrt(kss * (1.0 / HEAD_DIM) + EPS) * kg_ref[...]).astype(BF16)
    v_ref[0] = z_ref[:, POOL_WIDTH + 2 * ATT_WIDTH:].astype(BF16)

    ext_ref[0:POOL_HALO, :] = jnp.where(j == 0, 0.0, ext_ref[0:POOL_HALO, :])
    ext_ref[POOL_HALO:, :] = z_ref[:, :POOL_WIDTH]
    pos = j * t + lax.broadcasted_iota(jnp.int32, (t, 1), 0)
    outs = []
    for gi, win in enumerate(POOL_WINDOWS):
        a = ext_ref[:, gi * POOL_GW:(gi + 1) * POOL_GW]
        s = a
        shift = 1
        while shift < win:
            s = s + pltpu.roll(s, shift, 0)
            shift *= 2
        cnt = jnp.minimum(pos + 1, win).astype(F32)
        pooled = s[POOL_HALO:, :] / cnt - a[POOL_HALO:, :]
        outs.append(jnp.dot(pooled.astype(BF16), pw_ref[0, gi].astype(BF16), preferred_element_type=F32))
    pool_ref[0] = (jnp.concatenate(outs, axis=1) * ps_ref[...]).astype(BF16)
    ext_ref[0:POOL_HALO, :] = ext_ref[t:t + POOL_HALO, :]


def _in_call(x, mod, g, w_in, pool_w, layer, pool_scale, qg, kg, bd, moe=None):
    b, s, d = x.shape
    t = SEQ_TILE
    steps = s // t
    tiles = b * steps
    proj = lambda n: jnp.minimum(n, tiles - 1)
    done = lambda n: jnp.maximum(n - 1, 0)
    tok_in = lambda w: pl.BlockSpec((1, t, w), lambda n: (proj(n) // steps, proj(n) % steps, 0))
    tok_out = lambda w: pl.BlockSpec((1, t, w), lambda n: (done(n) // steps, done(n) % steps, 0))
    const2 = lambda shape: pl.BlockSpec(shape, lambda n: (0, 0))
    mod_spec = pl.BlockSpec((1, 6, d), lambda n: (proj(n) // steps, 0, 0))
    out = jax.ShapeDtypeStruct((b, s, ATT_WIDTH), BF16)
    in_specs = [
        mod_spec,
        const2((1, d)),
        pl.BlockSpec((1, d, IN_WIDTH), lambda n: (layer, 0, 0), pipeline_mode=pl.Buffered(1)),
        pl.BlockSpec((1, len(POOL_WINDOWS), POOL_GW, POOL_GW), lambda n: (layer, 0, 0, 0)),
        const2((1, POOL_WIDTH)),
        const2((1, ATT_WIDTH)),
        const2((1, ATT_WIDTH)),
        const2((ATT_WIDTH, ATT_WIDTH)),
    ]
    args = (mod, g, w_in, pool_w, pool_scale, qg, kg, bd)
    out_specs = [tok_out(POOL_WIDTH), tok_out(ATT_WIDTH), tok_out(ATT_WIDTH), tok_out(ATT_WIDTH)]
    out_shape = [jax.ShapeDtypeStruct((b, s, POOL_WIDTH), BF16), out, out, out]
    scratch = [pltpu.VMEM((t + POOL_HALO, POOL_WIDTH), F32), pltpu.VMEM((d, IN_WIDTH), BF16),
               pltpu.VMEM((2, t, IN_WIDTH), F32)]
    if moe is None:
        return pl.pallas_call(
            functools.partial(_in_kernel, steps), grid=(tiles + 1,), in_specs=[tok_in(d)] + in_specs,
            out_specs=out_specs, out_shape=out_shape, scratch_shapes=scratch,
            compiler_params=_params("arbitrary"), name="in_proj",
        )(x, *args)
    plan, mod_prev, ys = moe
    plan_specs, plan_args = _combine_plan_specs(plan, tiles)
    return pl.pallas_call(
        functools.partial(_in_comb_kernel, steps), grid=(tiles + 1,),
        in_specs=plan_specs + [tok_in(d), mod_spec, pl.BlockSpec(memory_space=pl.ANY)] + in_specs,
        out_specs=[tok_in(d)] + out_specs,
        out_shape=[jax.ShapeDtypeStruct((b, s, d), F32)] + out_shape,
        scratch_shapes=scratch + _gather_scratch(),
        compiler_params=_params("arbitrary"), name="combine_in_proj",
    )(*plan_args, x, mod_prev, ys, *args)


def _att_kernel(q_ref, kp_ref, kc_ref, vp_ref, vc_ref, bias_ref, o_ref):
    i = pl.program_id(2)
    tq = q_ref.shape[1]
    lane = lax.broadcasted_iota(jnp.int32, (1, LANES), 1)
    low_half = lane < HEAD_DIM
    for g in range(tq // ATT_GROUP):
        r0 = g * ATT_GROUP
        qg = q_ref[0, r0:r0 + ATT_GROUP, :]
        if r0 < LEFT:
            kb = jnp.concatenate([kp_ref[0, r0:LEFT, :], kc_ref[0, 0:r0 + ATT_GROUP, :]], axis=0)
            vb = jnp.concatenate([vp_ref[0, r0:LEFT, :], vc_ref[0, 0:r0 + ATT_GROUP, :]], axis=0)
            variant = jnp.where(i == 0, g + 1, 0)
        else:
            kb = kc_ref[0, r0 - LEFT:r0 + ATT_GROUP, :]
            vb = vc_ref[0, r0 - LEFT:r0 + ATT_GROUP, :]
            variant = 0
        zero = jnp.zeros_like(qg)
        qs = jnp.concatenate([jnp.where(low_half, qg, zero), jnp.where(low_half, zero, qg)], axis=0)
        s = lax.dot_general(qs, kb, (((1,), (1,)), ((), ())), preferred_element_type=F32)
        s = s + bias_ref[variant]
        m = jnp.max(s, axis=-1, keepdims=True)
        e = jnp.exp2(s - m)
        l = jnp.sum(e, axis=-1, keepdims=True)
        pv = jnp.dot(e.astype(BF16), vb, preferred_element_type=F32) / l
        o_ref[0, r0:r0 + ATT_GROUP, :] = jnp.where(low_half, pv[:ATT_GROUP], pv[ATT_GROUP:]).astype(BF16)


def _att_call(q, k, v, bias):
    b, s, _ = q.shape
    tq = min(ATT_TILE, s)
    assert s % tq == 0 and tq >= LEFT and tq % ATT_GROUP == 0
    lb = tq // LEFT
    cur = pl.BlockSpec((1, tq, LANES), lambda bi, hp, i: (bi, i, hp))
    prev = pl.BlockSpec((1, LEFT, LANES), lambda bi, hp, i: (bi, jnp.maximum(i * lb - 1, 0), hp))
    return pl.pallas_call(
        _att_kernel,
        grid=(b, N_HEADS // 2, s // tq),
        in_specs=[cur, prev, cur, prev, cur,
                  pl.BlockSpec((ATT_VARIANTS, 2 * ATT_GROUP, ATT_BAND), lambda bi, hp, i: (0, hp, 0))],
        out_specs=cur,
        out_shape=jax.ShapeDtypeStruct((b, s, ATT_WIDTH), BF16),
        compiler_params=_params("arbitrary", "arbitrary", "arbitrary"),
        name="chunk_attn",
    )(q, k, k, v, v, bias)


def _group_bias(rel_bias):
    qi = jnp.arange(ATT_GROUP)[:, None]
    kj = jnp.arange(ATT_BAND)[None, :]
    first = (qi // CHUNK) * CHUNK
    visible = (kj >= first) & (kj < first + LEFT + CHUNK)
    lo = ATT_GROUP - 1
    assert lo >= MAX_REL and ATT_BAND - 1 >= MAX_REL
    tab = rel_bias.astype(F32)
    by_rel = jnp.concatenate([jnp.repeat(tab[:, :1], lo - MAX_REL, axis=1), tab,
                              jnp.repeat(tab[:, -1:], ATT_BAND - 1 - MAX_REL, axis=1)], axis=1)
    rev = by_rel[:, ::-1] * LOG2_E
    length = lo + ATT_BAND
    period = jnp.concatenate([rev, jnp.zeros((N_HEADS, 1), F32)], axis=1)
    flat = jnp.tile(period, (1, ATT_GROUP))[:, :ATT_GROUP * length]
    bias = flat.reshape(N_HEADS, ATT_GROUP, length)[:, :, lo:lo + ATT_BAND]
    tables = []
    for variant in range(ATT_VARIANTS):
        ok = visible if variant == 0 else visible & (kj >= LEFT - (variant - 1) * ATT_GROUP)
        tables.append(jnp.where(ok[None], bias, NEG_INF))
    return jnp.stack(tables, axis=0).reshape(ATT_VARIANTS, N_HEADS * ATT_GROUP, ATT_BAND)


def _out_kernel(x_ref, pool_ref, att_ref, mod_ref, g_ref, wo_ref, wr_ref, br_ref,
                x1_ref, h2_ref, metat_ref, start_ref, cnt_ref, carry_ref, wob_ref, logit_ref):
    n = pl.program_id(0)
    t = x_ref.shape[1]

    @pl.when(n == 0)
    def _():
        carry_ref[...] = jnp.zeros_like(carry_ref)
        logit_ref[...] = jnp.zeros_like(logit_ref)
        wob_ref[...] = wo_ref[0].astype(BF16)

    _route_tile(jnp.where(n > 0, 1.0, 0.0), logit_ref[...], metat_ref, start_ref, cnt_ref, carry_ref)

    mix = (jnp.dot(pool_ref[0], wob_ref[:POOL_WIDTH, :], preferred_element_type=F32)
           + jnp.dot(att_ref[0], wob_ref[POOL_WIDTH:, :], preferred_element_type=F32))
    x1 = x_ref[0] + mod_ref[0, 2:3, :] * mix
    x1_ref[0] = x1
    ms = jnp.mean(x1 * x1, axis=-1, keepdims=True)
    y = x1 * lax.rsqrt(ms + EPS) * g_ref[...]
    h2 = y * (1.0 + mod_ref[0, 4:5, :]) + mod_ref[0, 3:4, :]

    _store_token_tiles(h2_ref, h2)
    hb = h2.astype(BF16)

    h_lo = (h2 - hb.astype(F32)).astype(BF16)
    r = (jnp.dot(hb, wr_ref[...], preferred_element_type=F32)
         + jnp.dot(h_lo, wr_ref[...], preferred_element_type=F32))
    logit_ref[...] = r[:, :LANES] + r[:, LANES:] + br_ref[...]


def _route_tile(live, logits, metat_ref, start_ref, cnt_ref, carry_ref):
    t = logits.shape[0]
    start_ref[...] = carry_ref[...]

    lt = jnp.transpose(logits)
    sub = lax.broadcasted_iota(jnp.int32, (EXPERTS_PER_GROUP, t), 0).astype(F32)
    ninf = jnp.float32(-jnp.inf)
    first_of = lambda hit: jnp.min(jnp.where(hit, sub, float(EXPERTS_PER_GROUP)), axis=0, keepdims=True)

    grp = lt[ROUTER_GROUP_LANE:ROUTER_GROUP_LANE + EXPERTS_PER_GROUP, :]
    gvalid = sub < N_GROUPS
    gl = jnp.where(gvalid, grp, ninf)
    gmax = jnp.max(gl, axis=0, keepdims=True)
    gsum = jnp.sum(jnp.where(gvalid, jnp.exp(grp - gmax), 0.0), axis=0, keepdims=True)
    g_p = 1.0 / gsum
    g_idx = first_of(gl == gmax)

    m1 = m2 = i1 = i2 = jnp.zeros((1, t), F32)
    for g in range(N_GROUPS):
        eg = lt[ROUTER_EXPERT_LANE + g * EXPERTS_PER_GROUP:ROUTER_EXPERT_LANE + (g + 1) * EXPERTS_PER_GROUP, :]
        a1 = jnp.max(eg, axis=0, keepdims=True)
        j1 = first_of(eg == a1)
        eg2 = jnp.where(sub == j1, ninf, eg)
        a2 = jnp.max(eg2, axis=0, keepdims=True)
        j2 = first_of(eg2 == a2)
        pick = g_idx == g
        m1 = jnp.where(pick, a1, m1)
        m2 = jnp.where(pick, a2, m2)
        i1 = jnp.where(pick, j1 + g * EXPERTS_PER_GROUP, i1)
        i2 = jnp.where(pick, j2 + g * EXPERTS_PER_GROUP, i2)
    e2 = jnp.exp(m2 - m1)
    w1 = g_p / (1.0 + e2)
    w2 = g_p * e2 / (1.0 + e2)

    expert = lax.broadcasted_iota(jnp.int32, (N_EXPERTS, t), 0).astype(F32)
    hot = (expert == i1) | (expert == i2)
    earlier = (lax.broadcasted_iota(jnp.int32, (t, t), 0)
               < lax.broadcasted_iota(jnp.int32, (t, t), 1)).astype(BF16)
    before = jnp.dot(hot.astype(BF16), earlier, preferred_element_type=F32) + carry_ref[:, 0:1]
    r1 = jnp.sum(jnp.where(expert == i1, before, 0.0), axis=0, keepdims=True)
    r2 = jnp.sum(jnp.where(expert == i2, before, 0.0), axis=0, keepdims=True)
    carry_ref[...] = carry_ref[...] + live * jnp.sum(hot.astype(F32), axis=1, keepdims=True)
    cnt_ref[...] = carry_ref[...]

    fields = {META_E0: i1, META_E1: i2, META_W0: w1, META_W1: w2, META_R0: r1, META_R1: r2}
    metat_ref[...] = jnp.concatenate([fields.get(k, jnp.zeros((1, t), F32)) for k in range(8)], axis=0)


def _out_call(x, pool, att, mod, g, w_out, layer, wr, br):
    b, s, d = x.shape
    t = SEQ_TILE
    steps = s // t
    tiles = b * steps
    proj = lambda n: jnp.minimum(n, tiles - 1)
    routed = lambda n: jnp.maximum(n - 1, 0)
    tok = lambda w: pl.BlockSpec((1, t, w), lambda n: (proj(n) // steps, proj(n) % steps, 0))
    const2 = lambda shape: pl.BlockSpec(shape, lambda n: (0, 0))
    return pl.pallas_call(
        _out_kernel,
        grid=(tiles + 1,),
        in_specs=[tok(d), tok(POOL_WIDTH), tok(ATT_WIDTH),
                  pl.BlockSpec((1, 6, d), lambda n: (proj(n) // steps, 0, 0)),
                  const2((1, d)),
                  pl.BlockSpec((1, d, d), lambda n: (layer, 0, 0), pipeline_mode=pl.Buffered(1)),
                  const2((d, 2 * LANES)), const2((1, LANES))],
        out_specs=[tok(d), pl.BlockSpec((t * TOKEN_SLABS, LANES), lambda n: (proj(n), 0)),
                   pl.BlockSpec((8, t), lambda n: (0, routed(n))),
                   pl.BlockSpec((N_EXPERTS, LANES), lambda n: (routed(n), 0)),
                   const2((N_EXPERTS, LANES))],
        out_shape=[jax.ShapeDtypeStruct((b, s, d), F32),
                   jax.ShapeDtypeStruct((b * s * TOKEN_SLABS, LANES), F32),
                   jax.ShapeDtypeStruct((8, b * s), F32),
                   jax.ShapeDtypeStruct((tiles * N_EXPERTS, LANES), F32),
                   jax.ShapeDtypeStruct((N_EXPERTS, LANES), F32)],
        scratch_shapes=[pltpu.VMEM((N_EXPERTS, LANES), F32), pltpu.VMEM((d, d), BF16),
                        pltpu.VMEM((t, LANES), F32)],
        compiler_params=_params("arbitrary"),
        name="out_proj_router",
    )(x, pool, att, mod, g, w_out, wr, br)


def _tile_rows(first_tile, n_tiles):
    return pl.ds(pl.multiple_of(first_tile * TOKEN_SLABS, TOKEN_SLABS), n_tiles * TOKEN_SLABS)


def _for_each_run_chunk(tab_ref, fn):
    def piece(k, carry):
        fn(tab_ref[0, 0, k], k * RUN_CHUNK)
        return carry
    lax.fori_loop(0, tab_ref[0, 0, MAX_CHUNKS], piece, 0)


def _disp_kernel(zrow_ref, tabc_ref, tabp_ref, p0_ref, p1_ref, h_ref, xs_ref, stage_ref, zbuf_ref, sems):
    j = pl.program_id(0)
    rows = h_ref.shape[0] // TOKEN_SLABS
    slot = j % 2

    @pl.when(j == 0)
    def _():
        stage_ref[...] = jnp.zeros_like(stage_ref)
        zbuf_ref[...] = jnp.zeros_like(zbuf_ref)

        def clear(blk_tile):
            return pltpu.make_async_copy(zbuf_ref, xs_ref.at[_tile_rows(blk_tile, EXPERT_ROWS)], sems.at[1])

        def each_clear(act):
            for e in range(2 * N_EXPERTS):
                @pl.when(zrow_ref[e] >= 0)
                def _():
                    act(clear(zrow_ref[e]))

            def tail(blk, carry):
                act(clear(blk * EXPERT_ROWS))
                return carry
            lax.fori_loop(zrow_ref[2 * N_EXPERTS], xs_ref.shape[0] // (EXPERT_ROWS * TOKEN_SLABS), tail, 0)

        each_clear(lambda cp: cp.start())
        each_clear(lambda cp: cp.wait())

    for r in range(rows):
        tile = h_ref[pl.ds(r * TOKEN_SLABS, TOKEN_SLABS), :]
        stage_ref[slot, _tile_rows(p0_ref[0, 0, r], 1), :] = tile
        stage_ref[slot, _tile_rows(p1_ref[0, 0, r], 1), :] = tile

    def copy(hbm_tile, stage_tile, s):
        return pltpu.make_async_copy(stage_ref.at[s, _tile_rows(stage_tile, RUN_CHUNK)],
                                     xs_ref.at[_tile_rows(hbm_tile, RUN_CHUNK)], sems.at[0])

    @pl.when(j > 0)
    def _():
        _for_each_run_chunk(tabp_ref, lambda h, s: copy(h, s, 1 - slot).wait())

    _for_each_run_chunk(tabc_ref, lambda h, s: copy(h, s, slot).start())

    @pl.when(j == pl.num_programs(0) - 1)
    def _():
        _for_each_run_chunk(tabc_ref, lambda h, s: copy(h, s, slot).wait())


def _disp_call(h2t, plan, zrow, p_rows):
    t = SEQ_TILE
    n_tiles = h2t.shape[0] // (t * TOKEN_SLABS)
    smem = lambda w, imap: pl.BlockSpec((1, 1, w), imap, memory_space=pltpu.SMEM)
    cur = lambda i, z: (i, 0, 0)
    grid_spec = pltpu.PrefetchScalarGridSpec(
        num_scalar_prefetch=1,
        grid=(n_tiles,),
        in_specs=[smem(LANES, cur), smem(LANES, lambda i, z: (jnp.maximum(i - 1, 0), 0, 0)),
                  smem(t, cur), smem(t, cur),
                  pl.BlockSpec((t * TOKEN_SLABS, LANES), lambda i, z: (i, 0))],
        out_specs=pl.BlockSpec(memory_space=pl.ANY),
        scratch_shapes=[pltpu.VMEM((2, STAGE_ROWS * TOKEN_SLABS, LANES), F32),
                        pltpu.VMEM((EXPERT_ROWS * TOKEN_SLABS, LANES), F32),
                        pltpu.SemaphoreType.DMA((2,))],
    )
    return pl.pallas_call(
        _disp_kernel,
        grid_spec=grid_spec,
        out_shape=jax.ShapeDtypeStruct((p_rows * TOKEN_SLABS, LANES), F32),
        compiler_params=_params("arbitrary"),
        name="dispatch_rows",
    )(zrow, plan["table"], plan["table"], plan["pos0"], plan["pos1"], h2t)


def _exp_kernel(layer, be_ref, nu_ref, xs_ref, wg_ref, wu_ref, wd_ref, y_ref,
                wgf_ref, wuf_ref, wdf_ref, wgb_ref, wub_ref, wdb_ref, sems):
    i = pl.program_id(0)
    rows = xs_ref.shape[0] // TOKEN_SLABS
    used = i < nu_ref[0]
    e = be_ref[i]
    new_expert = jnp.logical_or(i == 0, e != be_ref[jnp.maximum(i - 1, 0)])

    def fetch(expert, slot):
        return [pltpu.make_async_copy(w_ref.at[layer, expert], f_ref.at[slot], sems.at[slot])
                for w_ref, f_ref in ((wg_ref, wgf_ref), (wu_ref, wuf_ref), (wd_ref, wdf_ref))]

    @pl.when(jnp.logical_and(used, new_expert))
    def _():
        slot = e % 2

        @pl.when(i == 0)
        def _():
            for cp in fetch(e, slot):
                cp.start()

        for cp in fetch(e, slot):
            cp.wait()

        @pl.when(e + 1 < N_EXPERTS)
        def _():
            for cp in fetch(e + 1, 1 - slot):
                cp.start()

        wgb_ref[...] = wgf_ref[slot].astype(BF16)
        wub_ref[...] = wuf_ref[slot].astype(BF16)
        wdb_ref[...] = wdf_ref[slot].astype(BF16)

    @pl.when(used)
    def _():
        xb = _load_token_tiles(xs_ref, rows).astype(BF16)
        mids = []
        for c0 in range(0, D_EXPERT, EXPERT_COLS):
            gate = jnp.dot(xb, wgb_ref[:, c0:c0 + EXPERT_COLS], preferred_element_type=F32)
            up = jnp.dot(xb, wub_ref[:, c0:c0 + EXPERT_COLS], preferred_element_type=F32)
            mids.append((jax.nn.silu(gate) * up).astype(BF16))
        mid = jnp.concatenate(mids, axis=1)
        _store_token_tiles(y_ref, jnp.dot(mid, wdb_ref[...], preferred_element_type=F32))

    @pl.when(jnp.logical_not(used))
    def _():
        y_ref[...] = jnp.zeros_like(y_ref)


def _exp_call(blk_e, n_used, xs, wg, wu, wd, layer):
    p = xs.shape[0] // TOKEN_SLABS
    r = EXPERT_ROWS
    blk = (r * TOKEN_SLABS, LANES)
    hbm = pl.BlockSpec(memory_space=pl.ANY)
    up_shape, down_shape = (D_MODEL, D_EXPERT), (D_EXPERT, D_MODEL)
    grid_spec = pltpu.PrefetchScalarGridSpec(
        num_scalar_prefetch=2,
        grid=(p // r,),
        in_specs=[pl.BlockSpec(blk, lambda i, be, nu: (jnp.minimum(i, nu[0] - 1), 0)), hbm, hbm, hbm],
        out_specs=pl.BlockSpec(blk, lambda i, be, nu: (i, 0)),
        scratch_shapes=[pltpu.VMEM((2,) + up_shape, F32), pltpu.VMEM((2,) + up_shape, F32),
                        pltpu.VMEM((2,) + down_shape, F32),
                        pltpu.VMEM(up_shape, BF16), pltpu.VMEM(up_shape, BF16), pltpu.VMEM(down_shape, BF16),
                        pltpu.SemaphoreType.DMA((2,))],
    )
    return pl.pallas_call(
        functools.partial(_exp_kernel, layer),
        grid_spec=grid_spec,
        out_shape=jax.ShapeDtypeStruct(xs.shape, F32),
        compiler_params=_params("arbitrary"),
        name="experts",
    )(blk_e, n_used, xs, wg, wu, wd)


def _gather_scratch():
    return [pltpu.VMEM((2, STAGE_ROWS * TOKEN_SLABS, LANES), F32),
            pltpu.VMEM((SEQ_TILE * TOKEN_SLABS, LANES), F32), pltpu.SemaphoreType.DMA((2,))]


def _moe_residual(n, last, tabc_ref, tabn_ref, p0_ref, p1_ref, w0_ref, w1_ref, x_ref, mod_ref, ys_ref,
                  stage_ref, moe_ref, sems):
    rows = x_ref.shape[1]
    slot = n % 2

    def copy(hbm_tile, stage_tile, s):
        return pltpu.make_async_copy(ys_ref.at[_tile_rows(hbm_tile, RUN_CHUNK)],
                                     stage_ref.at[s, _tile_rows(stage_tile, RUN_CHUNK)], sems.at[s])

    @pl.when(n == 0)
    def _():
        _for_each_run_chunk(tabc_ref, lambda h, s: copy(h, s, 0).start())

    @pl.when(n < last)
    def _():
        _for_each_run_chunk(tabn_ref, lambda h, s: copy(h, s, 1 - slot).start())

    @pl.when(n <= last)
    def _():
        _for_each_run_chunk(tabc_ref, lambda h, s: copy(h, s, slot).wait())

    for r in range(rows):
        y0 = stage_ref[slot, _tile_rows(p0_ref[0, 0, r], 1), :]
        y1 = stage_ref[slot, _tile_rows(p1_ref[0, 0, r], 1), :]
        moe_ref[pl.ds(r * TOKEN_SLABS, TOKEN_SLABS), :] = y0 * w0_ref[0, 0, r] + y1 * w1_ref[0, 0, r]
    return x_ref[0] + mod_ref[0, 5:6, :] * _load_token_tiles(moe_ref, rows)


def _combine_plan_specs(plan, tiles):
    cur = lambda n: (jnp.minimum(n, tiles - 1), 0, 0)
    nxt = lambda n: (jnp.minimum(n + 1, tiles - 1), 0, 0)
    smem = lambda w, imap: pl.BlockSpec((1, 1, w), imap, memory_space=pltpu.SMEM)
    t = SEQ_TILE
    specs = [smem(LANES, cur), smem(LANES, nxt), smem(t, cur), smem(t, cur), smem(t, cur), smem(t, cur)]
    return specs, (plan["table"], plan["table"], plan["pos0"], plan["pos1"], plan["w0"], plan["w1"])


def _comb_kernel(tabc_ref, tabn_ref, p0_ref, p1_ref, w0_ref, w1_ref, x_ref, mod_ref, ys_ref, o_ref,
                 stage_ref, moe_ref, sems):
    o_ref[0] = _moe_residual(pl.program_id(0), pl.num_programs(0) - 1, tabc_ref, tabn_ref, p0_ref, p1_ref,
                             w0_ref, w1_ref, x_ref, mod_ref, ys_ref, stage_ref, moe_ref, sems)


def _comb_call(x1, plan, mod, ys):
    b, s, d = x1.shape
    t = SEQ_TILE
    steps = s // t
    plan_specs, plan_args = _combine_plan_specs(plan, b * steps)
    tok = lambda w: pl.BlockSpec((1, t, w), lambda n: (n // steps, n % steps, 0))
    return pl.pallas_call(
        _comb_kernel,
        grid=(b * steps,),
        in_specs=plan_specs + [tok(d), pl.BlockSpec((1, 6, d), lambda n: (n // steps, 0, 0)),
                               pl.BlockSpec(memory_space=pl.ANY)],
        out_specs=tok(d),
        out_shape=jax.ShapeDtypeStruct((b, s, d), F32),
        scratch_shapes=_gather_scratch(),
        compiler_params=_params("arbitrary"),
        name="combine_rows",
    )(*plan_args, x1, mod, ys)


def _dispatch_plan(meta_t, starts, counts, n_rows):
    r = EXPERT_ROWS
    t = SEQ_TILE
    cnt = counts[:, 0].astype(jnp.int32)
    start = starts[:, 0].reshape(-1, N_EXPERTS).astype(jnp.int32)
    in_tile = jnp.concatenate([start[1:], cnt[None, :]], axis=0) - start
    chunks = (in_tile + RUN_CHUNK - 1) // RUN_CHUNK
    span = chunks * RUN_CHUNK
    offset = jnp.cumsum(span, axis=1) - span
    padded = ((cnt + RUN_CHUNK + r - 1) // r) * r
    pad_end = jnp.cumsum(padded)
    pad_start = pad_end - padded
    n_tiles = start.shape[0]
    piece_end = jnp.cumsum(chunks, axis=1)
    k = jnp.arange(MAX_CHUNKS, dtype=jnp.int32)[None, :, None]
    owner = (piece_end[:, None, :] > k) & (piece_end[:, None, :] - chunks[:, None, :] <= k)
    first = (pad_start[None, :] + start - offset)[:, None, :] + k * RUN_CHUNK
    piece_row = jnp.sum(jnp.where(owner, first, 0), axis=2)
    table = jnp.concatenate([piece_row, jnp.broadcast_to(piece_end[:, -1:], (n_tiles, LANES - MAX_CHUNKS))],
                            axis=1).astype(jnp.int32)

    experts = jnp.arange(N_EXPERTS, dtype=jnp.int32)[None, :]
    base = jnp.repeat(offset - start, t, axis=0)

    def position(e, rank):
        return jnp.sum(jnp.where(e[:, None] == experts, base, 0), axis=1) + rank

    as_int = lambda row: meta_t[row].astype(jnp.int32)
    per_tile = lambda v: v.reshape(n_tiles, 1, t)
    plan = {"table": table.reshape(n_tiles, 1, LANES),
            "pos0": per_tile(position(as_int(META_E0), as_int(META_R0))),
            "pos1": per_tile(position(as_int(META_E1), as_int(META_R1))),
            "w0": per_tile(meta_t[META_W0]), "w1": per_tile(meta_t[META_W1])}

    n_used = (pad_end[-1:] // r).astype(jnp.int32)
    blk_row = jnp.minimum(jnp.arange(n_rows // r, dtype=jnp.int32), n_used - 1) * r
    blk_e = jnp.sum((pad_end[None, :] <= blk_row[:, None]).astype(jnp.int32), axis=1)
    pad_first = pad_start + (cnt // r) * r
    pad_last = pad_end - r
    zrow = jnp.concatenate([pad_first, jnp.where(pad_last > pad_first, pad_last, -1), n_used]).astype(jnp.int32)
    return plan, blk_e, n_used, zrow


def kernel(x, c, ada_w, ada_b, norm1_g, norm2_g, w_in, pool_w, pool_scale, q_norm_g, k_norm_g, rel_bias,
           w_out, router_group_w, router_group_b, router_expert_w, router_expert_b,
           moe_w_gate, moe_w_up, moe_w_down):
    b, s, d = x.shape
    depth = ada_w.shape[0]
    n = b * s
    assert d == D_MODEL and b <= 8 and s % SEQ_TILE == 0
    spare_blocks = -(-N_EXPERTS * RUN_CHUNK // EXPERT_ROWS)
    p_rows = n * 2 + (N_EXPERTS + spare_blocks) * EXPERT_ROWS

    c_pad = jnp.zeros((8, d), F32).at[:b].set(c)
    mod_all = _ada_call(c_pad, ada_w, ada_b)
    bias = _group_bias(rel_bias)
    head_block = jnp.arange(ATT_WIDTH) // HEAD_DIM
    bd = (head_block[:, None] == head_block[None, :]).astype(BF16)
    sm_scale = HEAD_DIM ** -0.5 * LOG2_E

    moe = None
    for l in range(depth):
        mod = mod_all[l, :b].reshape(b, 6, d)
        qg = (jnp.tile(q_norm_g[l], N_HEADS) * sm_scale).reshape(1, ATT_WIDTH)
        kg = jnp.tile(k_norm_g[l], N_HEADS).reshape(1, ATT_WIDTH)
        outs = _in_call(x, mod, norm1_g[l].reshape(1, d), w_in, pool_w, l,
                        pool_scale[l].reshape(1, POOL_WIDTH), qg, kg, bd, moe)
        if moe is not None:
            x, outs = outs[0], outs[1:]
        pool, q, k, v = outs
        att = _att_call(q, k, v, bias)

        wr = jnp.zeros((d, LANES), F32)
        wr = wr.at[:, ROUTER_GROUP_LANE:ROUTER_GROUP_LANE + N_GROUPS].set(router_group_w[l])
        wr = wr.at[:, ROUTER_EXPERT_LANE:ROUTER_EXPERT_LANE + N_EXPERTS].set(router_expert_w[l])
        wr_hi = wr.astype(BF16)
        wr_lo = (wr - wr_hi.astype(F32)).astype(BF16)
        br = jnp.zeros((1, LANES), F32)
        br = br.at[0, ROUTER_GROUP_LANE:ROUTER_GROUP_LANE + N_GROUPS].set(router_group_b[l])
        br = br.at[0, ROUTER_EXPERT_LANE:ROUTER_EXPERT_LANE + N_EXPERTS].set(router_expert_b[l])
        x1, h2t, meta_t, starts, counts = _out_call(x, pool, att, mod, norm2_g[l].reshape(1, d), w_out, l,
                                                    jnp.concatenate([wr_hi, wr_lo], axis=1), br)

        plan, blk_e, n_used, zrow = _dispatch_plan(meta_t, starts, counts, p_rows)
        xs = _disp_call(h2t, plan, zrow, p_rows)
        ys = _exp_call(blk_e, n_used, xs, moe_w_gate, moe_w_up, moe_w_down, l)
        x = x1
        moe = (plan, mod, ys)
    return _comb_call(x, moe[0], moe[1], moe[2])
```

```python
import functools

import jax
import jax.numpy as jnp
from jax import lax
from jax.experimental import pallas as pl
from jax.experimental.pallas import tpu as pltpu

F32 = jnp.float32
BF16 = jnp.bfloat16

D_MODEL = 1024
CHUNK = 64
POOL_WIDTH = 512
POOL_WINDOWS = (2, 4, 8, 16)
POOL_GW = 128
POOL_HALO = 16
ATT_WIDTH = 512
N_HEADS = 8
HEAD_DIM = 64
LEFT = 512
MAX_REL = 128
IN_WIDTH = 2048
N_GROUPS = 4
EXPERTS_PER_GROUP = 8
N_EXPERTS = 32
D_EXPERT = 512
EPS = 1e-6
NEG_INF = -1e30

LANES = 128
TOKEN_SLABS = D_MODEL // LANES
SEQ_TILE = 512
ATT_TILE = 4096
ATT_GROUP = 4 * CHUNK
ATT_BAND = LEFT + ATT_GROUP
ATT_VARIANTS = LEFT // ATT_GROUP + 1
LOG2_E = 1.4426950408889634
EXPERT_ROWS = 512
EXPERT_COLS = 256
RUN_CHUNK = 16
MAX_CHUNKS = 2 * SEQ_TILE // RUN_CHUNK + N_EXPERTS
STAGE_ROWS = MAX_CHUNKS * RUN_CHUNK
assert MAX_CHUNKS < LANES
VMEM_LIMIT = 56 * 1024 * 1024

META_E0, META_E1, META_W0, META_W1, META_R0, META_R1 = 0, 1, 2, 3, 4, 5
ROUTER_EXPERT_LANE = 0
ROUTER_GROUP_LANE = N_EXPERTS


def _params(*sem):
    return pltpu.CompilerParams(dimension_semantics=sem, vmem_limit_bytes=VMEM_LIMIT)


def _first_step():
    return jnp.logical_and(pl.program_id(0) == 0, pl.program_id(1) == 0)


def _load_token_tiles(ref, rows):
    return jnp.concatenate([ref[pl.ds(s, rows, stride=TOKEN_SLABS), :] for s in range(TOKEN_SLABS)], axis=1)


def _store_token_tiles(ref, val, first_token=0):
    rows = val.shape[0]
    for s in range(TOKEN_SLABS):
        ref[pl.ds(first_token * TOKEN_SLABS + s, rows, stride=TOKEN_SLABS), :] = val[:, s * LANES:(s + 1) * LANES]


def _ada_kernel(c_ref, w_ref, b_ref, o_ref):
    ca = jax.nn.silu(c_ref[...])
    o_ref[0] = jnp.dot(ca, w_ref[0], precision=lax.Precision.HIGHEST,
                       preferred_element_type=F32) + b_ref[0]


def _ada_call(c_pad, ada_w, ada_b):
    depth = ada_w.shape[0]
    tn = 1024
    return pl.pallas_call(
        _ada_kernel,
        grid=(depth, 6 * D_MODEL // tn),
        in_specs=[
            pl.BlockSpec((8, D_MODEL), lambda l, n: (0, 0)),
            pl.BlockSpec((1, D_MODEL, tn), lambda l, n: (l, 0, n)),
            pl.BlockSpec((1, 1, tn), lambda l, n: (l, 0, n)),
        ],
        out_specs=pl.BlockSpec((1, 8, tn), lambda l, n: (l, 0, n)),
        out_shape=jax.ShapeDtypeStruct((depth, 8, 6 * D_MODEL), F32),
        compiler_params=_params("arbitrary", "arbitrary"),
        name="ada_mod",
    )(c_pad, ada_w, ada_b.reshape(depth, 1, 6 * D_MODEL))


def _in_kernel(x_ref, *refs):
    _in_body(x_ref[0], *refs)


def _in_comb_kernel(tabc_ref, tabn_ref, p0_ref, p1_ref, w0_ref, w1_ref, x1_ref, modp_ref, ys_ref,
                    mod_ref, g_ref, w_ref, pw_ref, ps_ref, qg_ref, kg_ref, bd_ref,
                    xn_ref, pool_ref, q_ref, k_ref, v_ref, ext_ref, wb_ref, stage_ref, moe_ref, sems):
    x = _moe_residual(tabc_ref, tabn_ref, p0_ref, p1_ref, w0_ref, w1_ref, x1_ref, modp_ref, ys_ref,
                      stage_ref, moe_ref, sems)
    xn_ref[0] = x
    _in_body(x, mod_ref, g_ref, w_ref, pw_ref, ps_ref, qg_ref, kg_ref, bd_ref,
             pool_ref, q_ref, k_ref, v_ref, ext_ref, wb_ref)


def _in_body(x, mod_ref, g_ref, w_ref, pw_ref, ps_ref, qg_ref, kg_ref, bd_ref,
             pool_ref, q_ref, k_ref, v_ref, ext_ref, wb_ref):
    j = pl.program_id(1)
    t = x.shape[0]

    @pl.when(_first_step())
    def _():
        wb_ref[...] = w_ref[0].astype(BF16)

    ms = jnp.mean(x * x, axis=-1, keepdims=True)
    y = x * lax.rsqrt(ms + EPS) * g_ref[...]
    h = y * (1.0 + mod_ref[0, 1:2, :]) + mod_ref[0, 0:1, :]
    z = jnp.dot(h.astype(BF16), wb_ref[...], preferred_element_type=F32)

    bd = bd_ref[...]
    q = z[:, POOL_WIDTH:POOL_WIDTH + ATT_WIDTH]
    k = z[:, POOL_WIDTH + ATT_WIDTH:POOL_WIDTH + 2 * ATT_WIDTH]
    qss = jnp.dot((q * q).astype(BF16), bd, preferred_element_type=F32)
    kss = jnp.dot((k * k).astype(BF16), bd, preferred_element_type=F32)
    q_ref[0] = (q * lax.rsqrt(qss * (1.0 / HEAD_DIM) + EPS) * qg_ref[...]).astype(BF16)
    k_ref[0] = (k * lax.rsqrt(kss * (1.0 / HEAD_DIM) + EPS) * kg_ref[...]).astype(BF16)
    v_ref[0] = z[:, POOL_WIDTH + 2 * ATT_WIDTH:].astype(BF16)

    @pl.when(j == 0)
    def _():
        ext_ref[0:POOL_HALO, :] = jnp.zeros((POOL_HALO, POOL_WIDTH), F32)

    ext_ref[POOL_HALO:, :] = z[:, :POOL_WIDTH]
    pos = j * t + lax.broadcasted_iota(jnp.int32, (t, 1), 0)
    outs = []
    for gi, win in enumerate(POOL_WINDOWS):
        a = ext_ref[:, gi * POOL_GW:(gi + 1) * POOL_GW]
        s = a
        shift = 1
        while shift < win:
            s = s + pltpu.roll(s, shift, 0)
            shift *= 2
        cnt = jnp.minimum(pos + 1, win).astype(F32)
        pooled = s[POOL_HALO:, :] / cnt - a[POOL_HALO:, :]
        outs.append(jnp.dot(pooled.astype(BF16), pw_ref[0, gi].astype(BF16), preferred_element_type=F32))
    pool_ref[0] = (jnp.concatenate(outs, axis=1) * ps_ref[...]).astype(BF16)
    ext_ref[0:POOL_HALO, :] = ext_ref[t:t + POOL_HALO, :]


def _in_call(x, mod, g, w_in, pool_w, layer, pool_scale, qg, kg, bd, moe=None):
    b, s, d = x.shape
    t = SEQ_TILE
    steps = s // t
    tok = lambda w: pl.BlockSpec((1, t, w), lambda bi, j: (bi, j, 0))
    const2 = lambda shape: pl.BlockSpec(shape, lambda bi, j: (0, 0))
    mod_spec = pl.BlockSpec((1, 6, d), lambda bi, j: (bi, 0, 0))
    out = jax.ShapeDtypeStruct((b, s, ATT_WIDTH), BF16)
    in_specs = [
        mod_spec,
        const2((1, d)),
        pl.BlockSpec((1, d, IN_WIDTH), lambda bi, j: (layer, 0, 0), pipeline_mode=pl.Buffered(1)),
        pl.BlockSpec((1, len(POOL_WINDOWS), POOL_GW, POOL_GW), lambda bi, j: (layer, 0, 0, 0)),
        const2((1, POOL_WIDTH)),
        const2((1, ATT_WIDTH)),
        const2((1, ATT_WIDTH)),
        const2((ATT_WIDTH, ATT_WIDTH)),
    ]
    args = (mod, g, w_in, pool_w, pool_scale, qg, kg, bd)
    out_specs = [tok(POOL_WIDTH), tok(ATT_WIDTH), tok(ATT_WIDTH), tok(ATT_WIDTH)]
    out_shape = [jax.ShapeDtypeStruct((b, s, POOL_WIDTH), BF16), out, out, out]
    scratch = [pltpu.VMEM((t + POOL_HALO, POOL_WIDTH), F32), pltpu.VMEM((d, IN_WIDTH), BF16)]
    if moe is None:
        return pl.pallas_call(
            _in_kernel, grid=(b, steps), in_specs=[tok(d)] + in_specs, out_specs=out_specs, out_shape=out_shape,
            scratch_shapes=scratch, compiler_params=_params("arbitrary", "arbitrary"), name="in_proj",
        )(x, *args)
    plan, mod_prev, ys = moe
    plan_specs, plan_args = _combine_plan_specs(plan, b, steps)
    return pl.pallas_call(
        _in_comb_kernel, grid=(b, steps),
        in_specs=plan_specs + [tok(d), mod_spec, pl.BlockSpec(memory_space=pl.ANY)] + in_specs,
        out_specs=[tok(d)] + out_specs,
        out_shape=[jax.ShapeDtypeStruct((b, s, d), F32)] + out_shape,
        scratch_shapes=scratch + _gather_scratch(),
        compiler_params=_params("arbitrary", "arbitrary"), name="combine_in_proj",
    )(*plan_args, x, mod_prev, ys, *args)


def _att_kernel(q_ref, kp_ref, kc_ref, vp_ref, vc_ref, bias_ref, o_ref):
    i = pl.program_id(2)
    tq = q_ref.shape[1]
    lane = lax.broadcasted_iota(jnp.int32, (1, LANES), 1)
    low_half = lane < HEAD_DIM
    for g in range(tq // ATT_GROUP):
        r0 = g * ATT_GROUP
        qg = q_ref[0, r0:r0 + ATT_GROUP, :]
        if r0 < LEFT:
            kb = jnp.concatenate([kp_ref[0, r0:LEFT, :], kc_ref[0, 0:r0 + ATT_GROUP, :]], axis=0)
            vb = jnp.concatenate([vp_ref[0, r0:LEFT, :], vc_ref[0, 0:r0 + ATT_GROUP, :]], axis=0)
            variant = jnp.where(i == 0, g + 1, 0)
        else:
            kb = kc_ref[0, r0 - LEFT:r0 + ATT_GROUP, :]
            vb = vc_ref[0, r0 - LEFT:r0 + ATT_GROUP, :]
            variant = 0
        zero = jnp.zeros_like(qg)
        qs = jnp.concatenate([jnp.where(low_half, qg, zero), jnp.where(low_half, zero, qg)], axis=0)
        s = lax.dot_general(qs, kb, (((1,), (1,)), ((), ())), preferred_element_type=F32)
        s = s + bias_ref[variant]
        m = jnp.max(s, axis=-1, keepdims=True)
        e = jnp.exp2(s - m)
        l = jnp.sum(e, axis=-1, keepdims=True)
        pv = jnp.dot(e.astype(BF16), vb, preferred_element_type=F32) / l
        o_ref[0, r0:r0 + ATT_GROUP, :] = jnp.where(low_half, pv[:ATT_GROUP], pv[ATT_GROUP:]).astype(BF16)


def _att_call(q, k, v, bias):
    b, s, _ = q.shape
    tq = min(ATT_TILE, s)
    assert s % tq == 0 and tq >= LEFT and tq % ATT_GROUP == 0
    lb = tq // LEFT
    cur = pl.BlockSpec((1, tq, LANES), lambda bi, hp, i: (bi, i, hp))
    prev = pl.BlockSpec((1, LEFT, LANES), lambda bi, hp, i: (bi, jnp.maximum(i * lb - 1, 0), hp))
    return pl.pallas_call(
        _att_kernel,
        grid=(b, N_HEADS // 2, s // tq),
        in_specs=[cur, prev, cur, prev, cur,
                  pl.BlockSpec((ATT_VARIANTS, 2 * ATT_GROUP, ATT_BAND), lambda bi, hp, i: (0, hp, 0))],
        out_specs=cur,
        out_shape=jax.ShapeDtypeStruct((b, s, ATT_WIDTH), BF16),
        compiler_params=_params("arbitrary", "arbitrary", "arbitrary"),
        name="chunk_attn",
    )(q, k, k, v, v, bias)


def _group_bias(rel_bias):
    qi = jnp.arange(ATT_GROUP)[:, None]
    kj = jnp.arange(ATT_BAND)[None, :]
    first = (qi // CHUNK) * CHUNK
    visible = (kj >= first) & (kj < first + LEFT + CHUNK)
    lo = ATT_GROUP - 1
    assert lo >= MAX_REL and ATT_BAND - 1 >= MAX_REL
    tab = rel_bias.astype(F32)
    by_rel = jnp.concatenate([jnp.repeat(tab[:, :1], lo - MAX_REL, axis=1), tab,
                              jnp.repeat(tab[:, -1:], ATT_BAND - 1 - MAX_REL, axis=1)], axis=1)
    rev = by_rel[:, ::-1] * LOG2_E
    length = lo + ATT_BAND
    period = jnp.concatenate([rev, jnp.zeros((N_HEADS, 1), F32)], axis=1)
    flat = jnp.tile(period, (1, ATT_GROUP))[:, :ATT_GROUP * length]
    bias = flat.reshape(N_HEADS, ATT_GROUP, length)[:, :, lo:lo + ATT_BAND]
    tables = []
    for variant in range(ATT_VARIANTS):
        ok = visible if variant == 0 else visible & (kj >= LEFT - (variant - 1) * ATT_GROUP)
        tables.append(jnp.where(ok[None], bias, NEG_INF))
    return jnp.stack(tables, axis=0).reshape(ATT_VARIANTS, N_HEADS * ATT_GROUP, ATT_BAND)


def _out_kernel(x_ref, pool_ref, att_ref, mod_ref, g_ref, wo_ref, wr_ref, br_ref,
                x1_ref, h2_ref, metat_ref, start_ref, cnt_ref, carry_ref, wob_ref, logit_ref, mix_ref):
    n = pl.program_id(0)

    @pl.when(n == 0)
    def _():
        carry_ref[...] = jnp.zeros_like(carry_ref)
        logit_ref[...] = jnp.zeros_like(logit_ref)
        mix_ref[...] = jnp.zeros_like(mix_ref)
        wob_ref[...] = wo_ref[0].astype(BF16)

    _route_tile(jnp.where(n > 1, 1.0, 0.0), logit_ref[...], metat_ref, start_ref, cnt_ref, carry_ref)

    x1 = x_ref[0] + mod_ref[0, 2:3, :] * mix_ref[...]
    mix_ref[...] = (jnp.dot(pool_ref[0], wob_ref[:POOL_WIDTH, :], preferred_element_type=F32)
                    + jnp.dot(att_ref[0], wob_ref[POOL_WIDTH:, :], preferred_element_type=F32))
    x1_ref[0] = x1
    ms = jnp.mean(x1 * x1, axis=-1, keepdims=True)
    y = x1 * lax.rsqrt(ms + EPS) * g_ref[...]
    h2 = y * (1.0 + mod_ref[0, 4:5, :]) + mod_ref[0, 3:4, :]

    _store_token_tiles(h2_ref, h2)
    hb = h2.astype(BF16)

    h_lo = (h2 - hb.astype(F32)).astype(BF16)
    r = (jnp.dot(hb, wr_ref[...], preferred_element_type=F32)
         + jnp.dot(h_lo, wr_ref[...], preferred_element_type=F32))
    logit_ref[...] = r[:, :LANES] + r[:, LANES:] + br_ref[...]


def _route_tile(live, logits, metat_ref, start_ref, cnt_ref, carry_ref):
    t = logits.shape[0]
    start_ref[...] = carry_ref[...]

    lt = jnp.transpose(logits)
    sub = lax.broadcasted_iota(jnp.int32, (EXPERTS_PER_GROUP, t), 0).astype(F32)
    ninf = jnp.float32(-jnp.inf)
    first_of = lambda hit: jnp.min(jnp.where(hit, sub, float(EXPERTS_PER_GROUP)), axis=0, keepdims=True)

    grp = lt[ROUTER_GROUP_LANE:ROUTER_GROUP_LANE + EXPERTS_PER_GROUP, :]
    gvalid = sub < N_GROUPS
    gl = jnp.where(gvalid, grp, ninf)
    gmax = jnp.max(gl, axis=0, keepdims=True)
    gsum = jnp.sum(jnp.where(gvalid, jnp.exp(grp - gmax), 0.0), axis=0, keepdims=True)
    g_p = 1.0 / gsum
    g_idx = first_of(gl == gmax)

    m1 = m2 = i1 = i2 = jnp.zeros((1, t), F32)
    for g in range(N_GROUPS):
        eg = lt[ROUTER_EXPERT_LANE + g * EXPERTS_PER_GROUP:ROUTER_EXPERT_LANE + (g + 1) * EXPERTS_PER_GROUP, :]
        a1 = jnp.max(eg, axis=0, keepdims=True)
        j1 = first_of(eg == a1)
        eg2 = jnp.where(sub == j1, ninf, eg)
        a2 = jnp.max(eg2, axis=0, keepdims=True)
        j2 = first_of(eg2 == a2)
        pick = g_idx == g
        m1 = jnp.where(pick, a1, m1)
        m2 = jnp.where(pick, a2, m2)
        i1 = jnp.where(pick, j1 + g * EXPERTS_PER_GROUP, i1)
        i2 = jnp.where(pick, j2 + g * EXPERTS_PER_GROUP, i2)
    e2 = jnp.exp(m2 - m1)
    w1 = g_p / (1.0 + e2)
    w2 = g_p * e2 / (1.0 + e2)

    expert = lax.broadcasted_iota(jnp.int32, (N_EXPERTS, t), 0).astype(F32)
    hot = (expert == i1) | (expert == i2)
    earlier = (lax.broadcasted_iota(jnp.int32, (t, t), 0)
               < lax.broadcasted_iota(jnp.int32, (t, t), 1)).astype(BF16)
    before = jnp.dot(hot.astype(BF16), earlier, preferred_element_type=F32) + carry_ref[:, 0:1]
    r1 = jnp.sum(jnp.where(expert == i1, before, 0.0), axis=0, keepdims=True)
    r2 = jnp.sum(jnp.where(expert == i2, before, 0.0), axis=0, keepdims=True)
    carry_ref[...] = carry_ref[...] + live * jnp.sum(hot.astype(F32), axis=1, keepdims=True)
    cnt_ref[...] = carry_ref[...]

    fields = {META_E0: i1, META_E1: i2, META_W0: w1, META_W1: w2, META_R0: r1, META_R1: r2}
    metat_ref[...] = jnp.concatenate([fields.get(k, jnp.zeros((1, t), F32)) for k in range(8)], axis=0)


def _out_call(x, pool, att, mod, g, w_out, layer, wr, br):
    b, s, d = x.shape
    t = SEQ_TILE
    steps = s // t
    tiles = b * steps
    stage = lambda lag: (lambda n: jnp.clip(n - lag, 0, tiles - 1))
    proj, norm, routed = stage(0), stage(1), stage(2)
    tok = lambda w, at: pl.BlockSpec((1, t, w), lambda n: (at(n) // steps, at(n) % steps, 0))
    const2 = lambda shape: pl.BlockSpec(shape, lambda n: (0, 0))
    return pl.pallas_call(
        _out_kernel,
        grid=(tiles + 2,),
        in_specs=[tok(d, norm), tok(POOL_WIDTH, proj), tok(ATT_WIDTH, proj),
                  pl.BlockSpec((1, 6, d), lambda n: (norm(n) // steps, 0, 0)),
                  const2((1, d)),
                  pl.BlockSpec((1, d, d), lambda n: (layer, 0, 0), pipeline_mode=pl.Buffered(1)),
                  const2((d, 2 * LANES)), const2((1, LANES))],
        out_specs=[tok(d, norm), pl.BlockSpec((t * TOKEN_SLABS, LANES), lambda n: (norm(n), 0)),
                   pl.BlockSpec((8, t), lambda n: (0, routed(n))),
                   pl.BlockSpec((N_EXPERTS, LANES), lambda n: (routed(n), 0)),
                   const2((N_EXPERTS, LANES))],
        out_shape=[jax.ShapeDtypeStruct((b, s, d), F32),
                   jax.ShapeDtypeStruct((b * s * TOKEN_SLABS, LANES), F32),
                   jax.ShapeDtypeStruct((8, b * s), F32),
                   jax.ShapeDtypeStruct((tiles * N_EXPERTS, LANES), F32),
                   jax.ShapeDtypeStruct((N_EXPERTS, LANES), F32)],
        scratch_shapes=[pltpu.VMEM((N_EXPERTS, LANES), F32), pltpu.VMEM((d, d), BF16),
                        pltpu.VMEM((t, LANES), F32), pltpu.VMEM((t, d), F32)],
        compiler_params=_params("arbitrary"),
        name="out_proj_router",
    )(x, pool, att, mod, g, w_out, wr, br)


def _tile_rows(first_tile, n_tiles):
    return pl.ds(pl.multiple_of(first_tile * TOKEN_SLABS, TOKEN_SLABS), n_tiles * TOKEN_SLABS)


def _for_each_run_chunk(tab_ref, fn):
    def piece(k, carry):
        fn(tab_ref[0, 0, k], k * RUN_CHUNK)
        return carry
    lax.fori_loop(0, tab_ref[0, 0, MAX_CHUNKS], piece, 0)


def _disp_kernel(zrow_ref, tabc_ref, tabp_ref, p0_ref, p1_ref, h_ref, xs_ref, stage_ref, zbuf_ref, sems):
    j = pl.program_id(0)
    rows = h_ref.shape[0] // TOKEN_SLABS
    slot = j % 2

    @pl.when(j == 0)
    def _():
        stage_ref[...] = jnp.zeros_like(stage_ref)
        zbuf_ref[...] = jnp.zeros_like(zbuf_ref)

        def clear(blk_tile):
            return pltpu.make_async_copy(zbuf_ref, xs_ref.at[_tile_rows(blk_tile, EXPERT_ROWS)], sems.at[1])

        def each_clear(act):
            for e in range(2 * N_EXPERTS):
                @pl.when(zrow_ref[e] >= 0)
                def _():
                    act(clear(zrow_ref[e]))

            def tail(blk, carry):
                act(clear(blk * EXPERT_ROWS))
                return carry
            lax.fori_loop(zrow_ref[2 * N_EXPERTS], xs_ref.shape[0] // (EXPERT_ROWS * TOKEN_SLABS), tail, 0)

        each_clear(lambda cp: cp.start())
        each_clear(lambda cp: cp.wait())

    for r in range(rows):
        tile = h_ref[pl.ds(r * TOKEN_SLABS, TOKEN_SLABS), :]
        stage_ref[slot, _tile_rows(p0_ref[0, 0, r], 1), :] = tile
        stage_ref[slot, _tile_rows(p1_ref[0, 0, r], 1), :] = tile

    def copy(hbm_tile, stage_tile, s):
        return pltpu.make_async_copy(stage_ref.at[s, _tile_rows(stage_tile, RUN_CHUNK)],
                                     xs_ref.at[_tile_rows(hbm_tile, RUN_CHUNK)], sems.at[0])

    @pl.when(j > 0)
    def _():
        _for_each_run_chunk(tabp_ref, lambda h, s: copy(h, s, 1 - slot).wait())

    _for_each_run_chunk(tabc_ref, lambda h, s: copy(h, s, slot).start())

    @pl.when(j == pl.num_programs(0) - 1)
    def _():
        _for_each_run_chunk(tabc_ref, lambda h, s: copy(h, s, slot).wait())


def _disp_call(h2t, plan, zrow, p_rows):
    t = SEQ_TILE
    n_tiles = h2t.shape[0] // (t * TOKEN_SLABS)
    smem = lambda w, imap: pl.BlockSpec((1, 1, w), imap, memory_space=pltpu.SMEM)
    cur = lambda i, z: (i, 0, 0)
    grid_spec = pltpu.PrefetchScalarGridSpec(
        num_scalar_prefetch=1,
        grid=(n_tiles,),
        in_specs=[smem(LANES, cur), smem(LANES, lambda i, z: (jnp.maximum(i - 1, 0), 0, 0)),
                  smem(t, cur), smem(t, cur),
                  pl.BlockSpec((t * TOKEN_SLABS, LANES), lambda i, z: (i, 0))],
        out_specs=pl.BlockSpec(memory_space=pl.ANY),
        scratch_shapes=[pltpu.VMEM((2, STAGE_ROWS * TOKEN_SLABS, LANES), F32),
                        pltpu.VMEM((EXPERT_ROWS * TOKEN_SLABS, LANES), F32),
                        pltpu.SemaphoreType.DMA((2,))],
    )
    return pl.pallas_call(
        _disp_kernel,
        grid_spec=grid_spec,
        out_shape=jax.ShapeDtypeStruct((p_rows * TOKEN_SLABS, LANES), F32),
        compiler_params=_params("arbitrary"),
        name="dispatch_rows",
    )(zrow, plan["table"], plan["table"], plan["pos0"], plan["pos1"], h2t)


def _exp_kernel(layer, be_ref, nu_ref, xs_ref, wg_ref, wu_ref, wd_ref, y_ref,
                wgf_ref, wuf_ref, wdf_ref, wgb_ref, wub_ref, wdb_ref, sems):
    i = pl.program_id(0)
    rows = xs_ref.shape[0] // TOKEN_SLABS
    used = i < nu_ref[0]
    e = be_ref[i]
    new_expert = jnp.logical_or(i == 0, e != be_ref[jnp.maximum(i - 1, 0)])

    def fetch(expert, slot):
        return [pltpu.make_async_copy(w_ref.at[layer, expert], f_ref.at[slot], sems.at[slot])
                for w_ref, f_ref in ((wg_ref, wgf_ref), (wu_ref, wuf_ref), (wd_ref, wdf_ref))]

    @pl.when(jnp.logical_and(used, new_expert))
    def _():
        slot = e % 2

        @pl.when(i == 0)
        def _():
            for cp in fetch(e, slot):
                cp.start()

        for cp in fetch(e, slot):
            cp.wait()

        @pl.when(e + 1 < N_EXPERTS)
        def _():
            for cp in fetch(e + 1, 1 - slot):
                cp.start()

        wgb_ref[...] = wgf_ref[slot].astype(BF16)
        wub_ref[...] = wuf_ref[slot].astype(BF16)
        wdb_ref[...] = wdf_ref[slot].astype(BF16)

    @pl.when(used)
    def _():
        xb = _load_token_tiles(xs_ref, rows).astype(BF16)
        mids = []
        for c0 in range(0, D_EXPERT, EXPERT_COLS):
            gate = jnp.dot(xb, wgb_ref[:, c0:c0 + EXPERT_COLS], preferred_element_type=F32)
            up = jnp.dot(xb, wub_ref[:, c0:c0 + EXPERT_COLS], preferred_element_type=F32)
            mids.append((jax.nn.silu(gate) * up).astype(BF16))
        mid = jnp.concatenate(mids, axis=1)
        _store_token_tiles(y_ref, jnp.dot(mid, wdb_ref[...], preferred_element_type=F32))

    @pl.when(jnp.logical_not(used))
    def _():
        y_ref[...] = jnp.zeros_like(y_ref)


def _exp_call(blk_e, n_used, xs, wg, wu, wd, layer):
    p = xs.shape[0] // TOKEN_SLABS
    r = EXPERT_ROWS
    blk = (r * TOKEN_SLABS, LANES)
    hbm = pl.BlockSpec(memory_space=pl.ANY)
    up_shape, down_shape = (D_MODEL, D_EXPERT), (D_EXPERT, D_MODEL)
    grid_spec = pltpu.PrefetchScalarGridSpec(
        num_scalar_prefetch=2,
        grid=(p // r,),
        in_specs=[pl.BlockSpec(blk, lambda i, be, nu: (jnp.minimum(i, nu[0] - 1), 0)), hbm, hbm, hbm],
        out_specs=pl.BlockSpec(blk, lambda i, be, nu: (i, 0)),
        scratch_shapes=[pltpu.VMEM((2,) + up_shape, F32), pltpu.VMEM((2,) + up_shape, F32),
                        pltpu.VMEM((2,) + down_shape, F32),
                        pltpu.VMEM(up_shape, BF16), pltpu.VMEM(up_shape, BF16), pltpu.VMEM(down_shape, BF16),
                        pltpu.SemaphoreType.DMA((2,))],
    )
    return pl.pallas_call(
        functools.partial(_exp_kernel, layer),
        grid_spec=grid_spec,
        out_shape=jax.ShapeDtypeStruct(xs.shape, F32),
        compiler_params=_params("arbitrary"),
        name="experts",
    )(blk_e, n_used, xs, wg, wu, wd)


def _gather_scratch():
    return [pltpu.VMEM((2, STAGE_ROWS * TOKEN_SLABS, LANES), F32),
            pltpu.VMEM((SEQ_TILE * TOKEN_SLABS, LANES), F32), pltpu.SemaphoreType.DMA((2,))]


def _moe_residual(tabc_ref, tabn_ref, p0_ref, p1_ref, w0_ref, w1_ref, x_ref, mod_ref, ys_ref,
                  stage_ref, moe_ref, sems):
    rows = x_ref.shape[1]
    n = pl.program_id(0) * pl.num_programs(1) + pl.program_id(1)
    last = pl.num_programs(0) * pl.num_programs(1) - 1
    slot = n % 2

    def copy(hbm_tile, stage_tile, s):
        return pltpu.make_async_copy(ys_ref.at[_tile_rows(hbm_tile, RUN_CHUNK)],
                                     stage_ref.at[s, _tile_rows(stage_tile, RUN_CHUNK)], sems.at[s])

    @pl.when(n == 0)
    def _():
        _for_each_run_chunk(tabc_ref, lambda h, s: copy(h, s, 0).start())

    @pl.when(n < last)
    def _():
        _for_each_run_chunk(tabn_ref, lambda h, s: copy(h, s, 1 - slot).start())

    _for_each_run_chunk(tabc_ref, lambda h, s: copy(h, s, slot).wait())

    for r in range(rows):
        y0 = stage_ref[slot, _tile_rows(p0_ref[0, 0, r], 1), :]
        y1 = stage_ref[slot, _tile_rows(p1_ref[0, 0, r], 1), :]
        moe_ref[pl.ds(r * TOKEN_SLABS, TOKEN_SLABS), :] = y0 * w0_ref[0, 0, r] + y1 * w1_ref[0, 0, r]
    return x_ref[0] + mod_ref[0, 5:6, :] * _load_token_tiles(moe_ref, rows)


def _combine_plan_specs(plan, b, steps):
    last = b * steps - 1
    cur = lambda bi, j: (bi * steps + j, 0, 0)
    nxt = lambda bi, j: (jnp.minimum(bi * steps + j + 1, last), 0, 0)
    smem = lambda w, imap: pl.BlockSpec((1, 1, w), imap, memory_space=pltpu.SMEM)
    t = SEQ_TILE
    specs = [smem(LANES, cur), smem(LANES, nxt), smem(t, cur), smem(t, cur), smem(t, cur), smem(t, cur)]
    return specs, (plan["table"], plan["table"], plan["pos0"], plan["pos1"], plan["w0"], plan["w1"])


def _comb_kernel(tabc_ref, tabn_ref, p0_ref, p1_ref, w0_ref, w1_ref, x_ref, mod_ref, ys_ref, o_ref,
                 stage_ref, moe_ref, sems):
    o_ref[0] = _moe_residual(tabc_ref, tabn_ref, p0_ref, p1_ref, w0_ref, w1_ref, x_ref, mod_ref, ys_ref,
                             stage_ref, moe_ref, sems)


def _comb_call(x1, plan, mod, ys):
    b, s, d = x1.shape
    t = SEQ_TILE
    steps = s // t
    plan_specs, plan_args = _combine_plan_specs(plan, b, steps)
    tok = lambda w: pl.BlockSpec((1, t, w), lambda bi, j: (bi, j, 0))
    return pl.pallas_call(
        _comb_kernel,
        grid=(b, steps),
        in_specs=plan_specs + [tok(d), pl.BlockSpec((1, 6, d), lambda bi, j: (bi, 0, 0)),
                               pl.BlockSpec(memory_space=pl.ANY)],
        out_specs=tok(d),
        out_shape=jax.ShapeDtypeStruct((b, s, d), F32),
        scratch_shapes=_gather_scratch(),
        compiler_params=_params("arbitrary", "arbitrary"),
        name="combine_rows",
    )(*plan_args, x1, mod, ys)


def _dispatch_plan(meta_t, starts, counts, n_rows):
    r = EXPERT_ROWS
    t = SEQ_TILE
    cnt = counts[:, 0].astype(jnp.int32)
    start = starts[:, 0].reshape(-1, N_EXPERTS).astype(jnp.int32)
    in_tile = jnp.concatenate([start[1:], cnt[None, :]], axis=0) - start
    chunks = (in_tile + RUN_CHUNK - 1) // RUN_CHUNK
    span = chunks * RUN_CHUNK
    offset = jnp.cumsum(span, axis=1) - span
    padded = ((cnt + RUN_CHUNK + r - 1) // r) * r
    pad_end = jnp.cumsum(padded)
    pad_start = pad_end - padded
    n_tiles = start.shape[0]
    piece_end = jnp.cumsum(chunks, axis=1)
    k = jnp.arange(MAX_CHUNKS, dtype=jnp.int32)[None, :, None]
    owner = (piece_end[:, None, :] > k) & (piece_end[:, None, :] - chunks[:, None, :] <= k)
    first = (pad_start[None, :] + start - offset)[:, None, :] + k * RUN_CHUNK
    piece_row = jnp.sum(jnp.where(owner, first, 0), axis=2)
    table = jnp.concatenate([piece_row, jnp.broadcast_to(piece_end[:, -1:], (n_tiles, LANES - MAX_CHUNKS))],
                            axis=1).astype(jnp.int32)

    experts = jnp.arange(N_EXPERTS, dtype=jnp.int32)[None, :]
    base = jnp.repeat(offset - start, t, axis=0)

    def position(e, rank):
        return jnp.sum(jnp.where(e[:, None] == experts, base, 0), axis=1) + rank

    as_int = lambda row: meta_t[row].astype(jnp.int32)
    per_tile = lambda v: v.reshape(n_tiles, 1, t)
    plan = {"table": table.reshape(n_tiles, 1, LANES),
            "pos0": per_tile(position(as_int(META_E0), as_int(META_R0))),
            "pos1": per_tile(position(as_int(META_E1), as_int(META_R1))),
            "w0": per_tile(meta_t[META_W0]), "w1": per_tile(meta_t[META_W1])}

    n_used = (pad_end[-1:] // r).astype(jnp.int32)
    blk_row = jnp.minimum(jnp.arange(n_rows // r, dtype=jnp.int32), n_used - 1) * r
    blk_e = jnp.sum((pad_end[None, :] <= blk_row[:, None]).astype(jnp.int32), axis=1)
    pad_first = pad_start + (cnt // r) * r
    pad_last = pad_end - r
    zrow = jnp.concatenate([pad_first, jnp.where(pad_last > pad_first, pad_last, -1), n_used]).astype(jnp.int32)
    return plan, blk_e, n_used, zrow


def kernel(x, c, ada_w, ada_b, norm1_g, norm2_g, w_in, pool_w, pool_scale, q_norm_g, k_norm_g, rel_bias,
           w_out, router_group_w, router_group_b, router_expert_w, router_expert_b,
           moe_w_gate, moe_w_up, moe_w_down):
    b, s, d = x.shape
    depth = ada_w.shape[0]
    n = b * s
    assert d == D_MODEL and b <= 8 and s % SEQ_TILE == 0
    spare_blocks = -(-N_EXPERTS * RUN_CHUNK // EXPERT_ROWS)
    p_rows = n * 2 + (N_EXPERTS + spare_blocks) * EXPERT_ROWS

    c_pad = jnp.zeros((8, d), F32).at[:b].set(c)
    mod_all = _ada_call(c_pad, ada_w, ada_b)
    bias = _group_bias(rel_bias)
    head_block = jnp.arange(ATT_WIDTH) // HEAD_DIM
    bd = (head_block[:, None] == head_block[None, :]).astype(BF16)
    sm_scale = HEAD_DIM ** -0.5 * LOG2_E

    moe = None
    for l in range(depth):
        mod = mod_all[l, :b].reshape(b, 6, d)
        qg = (jnp.tile(q_norm_g[l], N_HEADS) * sm_scale).reshape(1, ATT_WIDTH)
        kg = jnp.tile(k_norm_g[l], N_HEADS).reshape(1, ATT_WIDTH)
        outs = _in_call(x, mod, norm1_g[l].reshape(1, d), w_in, pool_w, l,
                        pool_scale[l].reshape(1, POOL_WIDTH), qg, kg, bd, moe)
        if moe is not None:
            x, outs = outs[0], outs[1:]
        pool, q, k, v = outs
        att = _att_call(q, k, v, bias)

        wr = jnp.zeros((d, LANES), F32)
        wr = wr.at[:, ROUTER_GROUP_LANE:ROUTER_GROUP_LANE + N_GROUPS].set(router_group_w[l])
        wr = wr.at[:, ROUTER_EXPERT_LANE:ROUTER_EXPERT_LANE + N_EXPERTS].set(router_expert_w[l])
        wr_hi = wr.astype(BF16)
        wr_lo = (wr - wr_hi.astype(F32)).astype(BF16)
        br = jnp.zeros((1, LANES), F32)
        br = br.at[0, ROUTER_GROUP_LANE:ROUTER_GROUP_LANE + N_GROUPS].set(router_group_b[l])
        br = br.at[0, ROUTER_EXPERT_LANE:ROUTER_EXPERT_LANE + N_EXPERTS].set(router_expert_b[l])
        x1, h2t, meta_t, starts, counts = _out_call(x, pool, att, mod, norm2_g[l].reshape(1, d), w_out, l,
                                                    jnp.concatenate([wr_hi, wr_lo], axis=1), br)

        plan, blk_e, n_used, zrow = _dispatch_plan(meta_t, starts, counts, p_rows)
        xs = _disp_call(h2t, plan, zrow, p_rows)
        ys = _exp_call(blk_e, n_used, xs, moe_w_gate, moe_w_up, moe_w_down, l)
        x = x1
        moe = (plan, mod, ys)
    return _comb_call(x, moe[0], moe[1], moe[2])
```

```python
import functools

import jax
import jax.numpy as jnp
from jax import lax
from jax.experimental import pallas as pl
from jax.experimental.pallas import tpu as pltpu

F32 = jnp.float32
BF16 = jnp.bfloat16

D_MODEL = 1024
CHUNK = 64
POOL_WIDTH = 512
POOL_WINDOWS = (2, 4, 8, 16)
POOL_GW = 128
POOL_HALO = 16
ATT_WIDTH = 512
N_HEADS = 8
HEAD_DIM = 64
LEFT = 512
MAX_REL = 128
IN_WIDTH = 2048
N_GROUPS = 4
EXPERTS_PER_GROUP = 8
N_EXPERTS = 32
D_EXPERT = 512
EPS = 1e-6
NEG_INF = -1e30

LANES = 128
TOKEN_SLABS = D_MODEL // LANES
SEQ_TILE = 512
ATT_TILE = 4096
ATT_GROUP = 4 * CHUNK
ATT_BAND = LEFT + ATT_GROUP
ATT_VARIANTS = LEFT // ATT_GROUP + 1
LOG2_E = 1.4426950408889634
EXPERT_ROWS = 512
EXPERT_COLS = 256
RUN_CHUNK = 16
MAX_CHUNKS = 2 * SEQ_TILE // RUN_CHUNK + N_EXPERTS
STAGE_ROWS = MAX_CHUNKS * RUN_CHUNK
assert MAX_CHUNKS < LANES
VMEM_LIMIT = 56 * 1024 * 1024

META_E0, META_E1, META_W0, META_W1, META_R0, META_R1 = 0, 1, 2, 3, 4, 5
ROUTER_EXPERT_LANE = 0
ROUTER_GROUP_LANE = N_EXPERTS


def _params(*sem):
    return pltpu.CompilerParams(dimension_semantics=sem, vmem_limit_bytes=VMEM_LIMIT)


def _first_step():
    return jnp.logical_and(pl.program_id(0) == 0, pl.program_id(1) == 0)


def _load_token_tiles(ref, rows):
    return jnp.concatenate([ref[pl.ds(s, rows, stride=TOKEN_SLABS), :] for s in range(TOKEN_SLABS)], axis=1)


def _store_token_tiles(ref, val, first_token=0):
    rows = val.shape[0]
    for s in range(TOKEN_SLABS):
        ref[pl.ds(first_token * TOKEN_SLABS + s, rows, stride=TOKEN_SLABS), :] = val[:, s * LANES:(s + 1) * LANES]


def _ada_kernel(c_ref, w_ref, b_ref, o_ref):
    ca = jax.nn.silu(c_ref[...])
    o_ref[0] = jnp.dot(ca, w_ref[0], precision=lax.Precision.HIGHEST,
                       preferred_element_type=F32) + b_ref[0]


def _ada_call(c_pad, ada_w, ada_b):
    depth = ada_w.shape[0]
    tn = 1024
    return pl.pallas_call(
        _ada_kernel,
        grid=(depth, 6 * D_MODEL // tn),
        in_specs=[
            pl.BlockSpec((8, D_MODEL), lambda l, n: (0, 0)),
            pl.BlockSpec((1, D_MODEL, tn), lambda l, n: (l, 0, n)),
            pl.BlockSpec((1, 1, tn), lambda l, n: (l, 0, n)),
        ],
        out_specs=pl.BlockSpec((1, 8, tn), lambda l, n: (l, 0, n)),
        out_shape=jax.ShapeDtypeStruct((depth, 8, 6 * D_MODEL), F32),
        compiler_params=_params("arbitrary", "arbitrary"),
        name="ada_mod",
    )(c_pad, ada_w, ada_b.reshape(depth, 1, 6 * D_MODEL))


def _in_kernel(x_ref, *refs):
    _in_body(x_ref[0], *refs)


def _in_comb_kernel(tabc_ref, tabn_ref, p0_ref, p1_ref, w0_ref, w1_ref, x1_ref, modp_ref, ys_ref,
                    mod_ref, g_ref, w_ref, pw_ref, ps_ref, qg_ref, kg_ref, bd_ref,
                    xn_ref, pool_ref, q_ref, k_ref, v_ref, ext_ref, wb_ref, stage_ref, moe_ref, sems):
    x = _moe_residual(tabc_ref, tabn_ref, p0_ref, p1_ref, w0_ref, w1_ref, x1_ref, modp_ref, ys_ref,
                      stage_ref, moe_ref, sems)
    xn_ref[0] = x
    _in_body(x, mod_ref, g_ref, w_ref, pw_ref, ps_ref, qg_ref, kg_ref, bd_ref,
             pool_ref, q_ref, k_ref, v_ref, ext_ref, wb_ref)


def _in_body(x, mod_ref, g_ref, w_ref, pw_ref, ps_ref, qg_ref, kg_ref, bd_ref,
             pool_ref, q_ref, k_ref, v_ref, ext_ref, wb_ref):
    j = pl.program_id(1)
    t = x.shape[0]

    @pl.when(_first_step())
    def _():
        wb_ref[...] = w_ref[0].astype(BF16)

    ms = jnp.mean(x * x, axis=-1, keepdims=True)
    y = x * lax.rsqrt(ms + EPS) * g_ref[...]
    h = y * (1.0 + mod_ref[0, 1:2, :]) + mod_ref[0, 0:1, :]
    z = jnp.dot(h.astype(BF16), wb_ref[...], preferred_element_type=F32)

    bd = bd_ref[...]
    q = z[:, POOL_WIDTH:POOL_WIDTH + ATT_WIDTH]
    k = z[:, POOL_WIDTH + ATT_WIDTH:POOL_WIDTH + 2 * ATT_WIDTH]
    qss = jnp.dot((q * q).astype(BF16), bd, preferred_element_type=F32)
    kss = jnp.dot((k * k).astype(BF16), bd, preferred_element_type=F32)
    q_ref[0] = (q * lax.rsqrt(qss * (1.0 / HEAD_DIM) + EPS) * qg_ref[...]).astype(BF16)
    k_ref[0] = (k * lax.rsqrt(kss * (1.0 / HEAD_DIM) + EPS) * kg_ref[...]).astype(BF16)
    v_ref[0] = z[:, POOL_WIDTH + 2 * ATT_WIDTH:].astype(BF16)

    @pl.when(j == 0)
    def _():
        ext_ref[0:POOL_HALO, :] = jnp.zeros((POOL_HALO, POOL_WIDTH), F32)

    ext_ref[POOL_HALO:, :] = z[:, :POOL_WIDTH]
    pos = j * t + lax.broadcasted_iota(jnp.int32, (t, 1), 0)
    outs = []
    for gi, win in enumerate(POOL_WINDOWS):
        a = ext_ref[:, gi * POOL_GW:(gi + 1) * POOL_GW]
        s = a
        shift = 1
        while shift < win:
            s = s + pltpu.roll(s, shift, 0)
            shift *= 2
        cnt = jnp.minimum(pos + 1, win).astype(F32)
        pooled = s[POOL_HALO:, :] / cnt - a[POOL_HALO:, :]
        outs.append(jnp.dot(pooled.astype(BF16), pw_ref[0, gi].astype(BF16), preferred_element_type=F32))
    pool_ref[0] = (jnp.concatenate(outs, axis=1) * ps_ref[...]).astype(BF16)
    ext_ref[0:POOL_HALO, :] = ext_ref[t:t + POOL_HALO, :]


def _in_call(x, mod, g, w_in, pool_w, layer, pool_scale, qg, kg, bd, moe=None):
    b, s, d = x.shape
    t = SEQ_TILE
    steps = s // t
    tok = lambda w: pl.BlockSpec((1, t, w), lambda bi, j: (bi, j, 0))
    const2 = lambda shape: pl.BlockSpec(shape, lambda bi, j: (0, 0))
    mod_spec = pl.BlockSpec((1, 6, d), lambda bi, j: (bi, 0, 0))
    out = jax.ShapeDtypeStruct((b, s, ATT_WIDTH), BF16)
    in_specs = [
        mod_spec,
        const2((1, d)),
        pl.BlockSpec((1, d, IN_WIDTH), lambda bi, j: (layer, 0, 0), pipeline_mode=pl.Buffered(1)),
        pl.BlockSpec((1, len(POOL_WINDOWS), POOL_GW, POOL_GW), lambda bi, j: (layer, 0, 0, 0)),
        const2((1, POOL_WIDTH)),
        const2((1, ATT_WIDTH)),
        const2((1, ATT_WIDTH)),
        const2((ATT_WIDTH, ATT_WIDTH)),
    ]
    args = (mod, g, w_in, pool_w, pool_scale, qg, kg, bd)
    out_specs = [tok(POOL_WIDTH), tok(ATT_WIDTH), tok(ATT_WIDTH), tok(ATT_WIDTH)]
    out_shape = [jax.ShapeDtypeStruct((b, s, POOL_WIDTH), BF16), out, out, out]
    scratch = [pltpu.VMEM((t + POOL_HALO, POOL_WIDTH), F32), pltpu.VMEM((d, IN_WIDTH), BF16)]
    if moe is None:
        return pl.pallas_call(
            _in_kernel, grid=(b, steps), in_specs=[tok(d)] + in_specs, out_specs=out_specs, out_shape=out_shape,
            scratch_shapes=scratch, compiler_params=_params("arbitrary", "arbitrary"), name="in_proj",
        )(x, *args)
    plan, mod_prev, ys = moe
    plan_specs, plan_args = _combine_plan_specs(plan, b, steps)
    return pl.pallas_call(
        _in_comb_kernel, grid=(b, steps),
        in_specs=plan_specs + [tok(d), mod_spec, pl.BlockSpec(memory_space=pl.ANY)] + in_specs,
        out_specs=[tok(d)] + out_specs,
        out_shape=[jax.ShapeDtypeStruct((b, s, d), F32)] + out_shape,
        scratch_shapes=scratch + _gather_scratch(),
        compiler_params=_params("arbitrary", "arbitrary"), name="combine_in_proj",
    )(*plan_args, x, mod_prev, ys, *args)


def _att_kernel(q_ref, kp_ref, kc_ref, vp_ref, vc_ref, bias_ref, o_ref):
    i = pl.program_id(2)
    tq = q_ref.shape[1]
    lane = lax.broadcasted_iota(jnp.int32, (1, LANES), 1)
    low_half = lane < HEAD_DIM
    for g in range(tq // ATT_GROUP):
        r0 = g * ATT_GROUP
        qg = q_ref[0, r0:r0 + ATT_GROUP, :]
        if r0 < LEFT:
            kb = jnp.concatenate([kp_ref[0, r0:LEFT, :], kc_ref[0, 0:r0 + ATT_GROUP, :]], axis=0)
            vb = jnp.concatenate([vp_ref[0, r0:LEFT, :], vc_ref[0, 0:r0 + ATT_GROUP, :]], axis=0)
            variant = jnp.where(i == 0, g + 1, 0)
        else:
            kb = kc_ref[0, r0 - LEFT:r0 + ATT_GROUP, :]
            vb = vc_ref[0, r0 - LEFT:r0 + ATT_GROUP, :]
            variant = 0
        zero = jnp.zeros_like(qg)
        qs = jnp.concatenate([jnp.where(low_half, qg, zero), jnp.where(low_half, zero, qg)], axis=0)
        s = lax.dot_general(qs, kb, (((1,), (1,)), ((), ())), preferred_element_type=F32)
        s = s + bias_ref[variant]
        m = jnp.max(s, axis=-1, keepdims=True)
        e = jnp.exp2(s - m)
        l = jnp.sum(e, axis=-1, keepdims=True)
        pv = jnp.dot(e.astype(BF16), vb, preferred_element_type=F32) / l
        o_ref[0, r0:r0 + ATT_GROUP, :] = jnp.where(low_half, pv[:ATT_GROUP], pv[ATT_GROUP:]).astype(BF16)


def _att_call(q, k, v, bias):
    b, s, _ = q.shape
    tq = min(ATT_TILE, s)
    assert s % tq == 0 and tq >= LEFT and tq % ATT_GROUP == 0
    lb = tq // LEFT
    cur = pl.BlockSpec((1, tq, LANES), lambda bi, hp, i: (bi, i, hp))
    prev = pl.BlockSpec((1, LEFT, LANES), lambda bi, hp, i: (bi, jnp.maximum(i * lb - 1, 0), hp))
    return pl.pallas_call(
        _att_kernel,
        grid=(b, N_HEADS // 2, s // tq),
        in_specs=[cur, prev, cur, prev, cur,
                  pl.BlockSpec((ATT_VARIANTS, 2 * ATT_GROUP, ATT_BAND), lambda bi, hp, i: (0, hp, 0))],
        out_specs=cur,
        out_shape=jax.ShapeDtypeStruct((b, s, ATT_WIDTH), BF16),
        compiler_params=_params("arbitrary", "arbitrary", "arbitrary"),
        name="chunk_attn",
    )(q, k, k, v, v, bias)


def _group_bias(rel_bias):
    qi = jnp.arange(ATT_GROUP)[:, None]
    kj = jnp.arange(ATT_BAND)[None, :]
    first = (qi // CHUNK) * CHUNK
    visible = (kj >= first) & (kj < first + LEFT + CHUNK)
    lo = ATT_GROUP - 1
    assert lo >= MAX_REL and ATT_BAND - 1 >= MAX_REL
    tab = rel_bias.astype(F32)
    by_rel = jnp.concatenate([jnp.repeat(tab[:, :1], lo - MAX_REL, axis=1), tab,
                              jnp.repeat(tab[:, -1:], ATT_BAND - 1 - MAX_REL, axis=1)], axis=1)
    rev = by_rel[:, ::-1] * LOG2_E
    length = lo + ATT_BAND
    period = jnp.concatenate([rev, jnp.zeros((N_HEADS, 1), F32)], axis=1)
    flat = jnp.tile(period, (1, ATT_GROUP))[:, :ATT_GROUP * length]
    bias = flat.reshape(N_HEADS, ATT_GROUP, length)[:, :, lo:lo + ATT_BAND]
    tables = []
    for variant in range(ATT_VARIANTS):
        ok = visible if variant == 0 else visible & (kj >= LEFT - (variant - 1) * ATT_GROUP)
        tables.append(jnp.where(ok[None], bias, NEG_INF))
    return jnp.stack(tables, axis=0).reshape(ATT_VARIANTS, N_HEADS * ATT_GROUP, ATT_BAND)


def _out_kernel(x_ref, pool_ref, att_ref, mod_ref, g_ref, wo_ref, wr_ref, br_ref,
                x1_ref, h2_ref, metat_ref, start_ref, cnt_ref, carry_ref, wob_ref, logit_ref, mix_ref):
    n = pl.program_id(0)

    @pl.when(n == 0)
    def _():
        carry_ref[...] = jnp.zeros_like(carry_ref)
        logit_ref[...] = jnp.zeros_like(logit_ref)
        mix_ref[...] = jnp.zeros_like(mix_ref)
        wob_ref[...] = wo_ref[0].astype(BF16)

    _route_tile(jnp.where(n > 1, 1.0, 0.0), logit_ref[...], metat_ref, start_ref, cnt_ref, carry_ref)

    x1 = x_ref[0] + mod_ref[0, 2:3, :] * mix_ref[...]
    mix_ref[...] = (jnp.dot(pool_ref[0], wob_ref[:POOL_WIDTH, :], preferred_element_type=F32)
                    + jnp.dot(att_ref[0], wob_ref[POOL_WIDTH:, :], preferred_element_type=F32))
    x1_ref[0] = x1
    ms = jnp.mean(x1 * x1, axis=-1, keepdims=True)
    y = x1 * lax.rsqrt(ms + EPS) * g_ref[...]
    h2 = y * (1.0 + mod_ref[0, 4:5, :]) + mod_ref[0, 3:4, :]

    _store_token_tiles(h2_ref, h2)
    hb = h2.astype(BF16)

    h_lo = (h2 - hb.astype(F32)).astype(BF16)
    r = (jnp.dot(hb, wr_ref[...], preferred_element_type=F32)
         + jnp.dot(h_lo, wr_ref[...], preferred_element_type=F32))
    logit_ref[...] = r[:, :LANES] + r[:, LANES:] + br_ref[...]


def _route_tile(live, logits, metat_ref, start_ref, cnt_ref, carry_ref):
    t = logits.shape[0]
    start_ref[...] = carry_ref[...]

    lt = jnp.transpose(logits)
    sub = lax.broadcasted_iota(jnp.int32, (EXPERTS_PER_GROUP, t), 0).astype(F32)
    ninf = jnp.float32(-jnp.inf)
    first_of = lambda hit: jnp.min(jnp.where(hit, sub, float(EXPERTS_PER_GROUP)), axis=0, keepdims=True)

    grp = lt[ROUTER_GROUP_LANE:ROUTER_GROUP_LANE + EXPERTS_PER_GROUP, :]
    gvalid = sub < N_GROUPS
    gl = jnp.where(gvalid, grp, ninf)
    gmax = jnp.max(gl, axis=0, keepdims=True)
    gsum = jnp.sum(jnp.where(gvalid, jnp.exp(grp - gmax), 0.0), axis=0, keepdims=True)
    g_p = 1.0 / gsum
    g_idx = first_of(gl == gmax)

    m1 = m2 = i1 = i2 = jnp.zeros((1, t), F32)
    for g in range(N_GROUPS):
        eg = lt[ROUTER_EXPERT_LANE + g * EXPERTS_PER_GROUP:ROUTER_EXPERT_LANE + (g + 1) * EXPERTS_PER_GROUP, :]
        a1 = jnp.max(eg, axis=0, keepdims=True)
        j1 = first_of(eg == a1)
        eg2 = jnp.where(sub == j1, ninf, eg)
        a2 = jnp.max(eg2, axis=0, keepdims=True)
        j2 = first_of(eg2 == a2)
        pick = g_idx == g
        m1 = jnp.where(pick, a1, m1)
        m2 = jnp.where(pick, a2, m2)
        i1 = jnp.where(pick, j1 + g * EXPERTS_PER_GROUP, i1)
        i2 = jnp.where(pick, j2 + g * EXPERTS_PER_GROUP, i2)
    e2 = jnp.exp(m2 - m1)
    w1 = g_p / (1.0 + e2)
    w2 = g_p * e2 / (1.0 + e2)

    expert = lax.broadcasted_iota(jnp.int32, (N_EXPERTS, t), 0).astype(F32)
    hot = (expert == i1) | (expert == i2)
    earlier = (lax.broadcasted_iota(jnp.int32, (t, t), 0)
               < lax.broadcasted_iota(jnp.int32, (t, t), 1)).astype(BF16)
    before = jnp.dot(hot.astype(BF16), earlier, preferred_element_type=F32) + carry_ref[:, 0:1]
    r1 = jnp.sum(jnp.where(expert == i1, before, 0.0), axis=0, keepdims=True)
    r2 = jnp.sum(jnp.where(expert == i2, before, 0.0), axis=0, keepdims=True)
    carry_ref[...] = carry_ref[...] + live * jnp.sum(hot.astype(F32), axis=1, keepdims=True)
    cnt_ref[...] = carry_ref[...]

    fields = {META_E0: i1, META_E1: i2, META_W0: w1, META_W1: w2, META_R0: r1, META_R1: r2}
    metat_ref[...] = jnp.concatenate([fields.get(k, jnp.zeros((1, t), F32)) for k in range(8)], axis=0)


def _out_call(x, pool, att, mod, g, w_out, layer, wr, br):
    b, s, d = x.shape
    t = SEQ_TILE
    steps = s // t
    tiles = b * steps
    stage = lambda lag: (lambda n: jnp.clip(n - lag, 0, tiles - 1))
    proj, norm, routed = stage(0), stage(1), stage(2)
    tok = lambda w, at: pl.BlockSpec((1, t, w), lambda n: (at(n) // steps, at(n) % steps, 0))
    const2 = lambda shape: pl.BlockSpec(shape, lambda n: (0, 0))
    return pl.pallas_call(
        _out_kernel,
        grid=(tiles + 2,),
        in_specs=[tok(d, norm), tok(POOL_WIDTH, proj), tok(ATT_WIDTH, proj),
                  pl.BlockSpec((1, 6, d), lambda n: (norm(n) // steps, 0, 0)),
                  const2((1, d)),
                  pl.BlockSpec((1, d, d), lambda n: (layer, 0, 0), pipeline_mode=pl.Buffered(1)),
                  const2((d, 2 * LANES)), const2((1, LANES))],
        out_specs=[tok(d, norm), pl.BlockSpec((t * TOKEN_SLABS, LANES), lambda n: (norm(n), 0)),
                   pl.BlockSpec((8, t), lambda n: (0, routed(n))),
                   pl.BlockSpec((N_EXPERTS, LANES), lambda n: (routed(n), 0)),
                   const2((N_EXPERTS, LANES))],
        out_shape=[jax.ShapeDtypeStruct((b, s, d), F32),
                   jax.ShapeDtypeStruct((b * s * TOKEN_SLABS, LANES), F32),
                   jax.ShapeDtypeStruct((8, b * s), F32),
                   jax.ShapeDtypeStruct((tiles * N_EXPERTS, LANES), F32),
                   jax.ShapeDtypeStruct((N_EXPERTS, LANES), F32)],
        scratch_shapes=[pltpu.VMEM((N_EXPERTS, LANES), F32), pltpu.VMEM((d, d), BF16),
                        pltpu.VMEM((t, LANES), F32), pltpu.VMEM((t, d), F32)],
        compiler_params=_params("arbitrary"),
        name="out_proj_router",
    )(x, pool, att, mod, g, w_out, wr, br)


def _tile_rows(first_tile, n_tiles):
    return pl.ds(pl.multiple_of(first_tile * TOKEN_SLABS, TOKEN_SLABS), n_tiles * TOKEN_SLABS)


def _for_each_run_chunk(tab_ref, fn):
    def piece(k, carry):
        fn(tab_ref[0, 0, k], k * RUN_CHUNK)
        return carry
    lax.fori_loop(0, tab_ref[0, 0, MAX_CHUNKS], piece, 0)


def _disp_kernel(zrow_ref, tabc_ref, tabp_ref, p0_ref, p1_ref, h_ref, xs_ref, stage_ref, zbuf_ref, sems):
    j = pl.program_id(0)
    rows = h_ref.shape[0] // TOKEN_SLABS
    slot = j % 2

    @pl.when(j == 0)
    def _():
        stage_ref[...] = jnp.zeros_like(stage_ref)
        zbuf_ref[...] = jnp.zeros_like(zbuf_ref)

        def clear(blk_tile):
            return pltpu.make_async_copy(zbuf_ref, xs_ref.at[_tile_rows(blk_tile, EXPERT_ROWS)], sems.at[1])

        def each_clear(act):
            for e in range(2 * N_EXPERTS):
                @pl.when(zrow_ref[e] >= 0)
                def _():
                    act(clear(zrow_ref[e]))

            def tail(blk, carry):
                act(clear(blk * EXPERT_ROWS))
                return carry
            lax.fori_loop(zrow_ref[2 * N_EXPERTS], xs_ref.shape[0] // (EXPERT_ROWS * TOKEN_SLABS), tail, 0)

        each_clear(lambda cp: cp.start())
        each_clear(lambda cp: cp.wait())

    for r in range(rows):
        tile = h_ref[pl.ds(r * TOKEN_SLABS, TOKEN_SLABS), :]
        stage_ref[slot, _tile_rows(p0_ref[0, 0, r], 1), :] = tile
        stage_ref[slot, _tile_rows(p1_ref[0, 0, r], 1), :] = tile

    def copy(hbm_tile, stage_tile, s):
        return pltpu.make_async_copy(stage_ref.at[s, _tile_rows(stage_tile, RUN_CHUNK)],
                                     xs_ref.at[_tile_rows(hbm_tile, RUN_CHUNK)], sems.at[0])

    @pl.when(j > 0)
    def _():
        _for_each_run_chunk(tabp_ref, lambda h, s: copy(h, s, 1 - slot).wait())

    _for_each_run_chunk(tabc_ref, lambda h, s: copy(h, s, slot).start())

    @pl.when(j == pl.num_programs(0) - 1)
    def _():
        _for_each_run_chunk(tabc_ref, lambda h, s: copy(h, s, slot).wait())


def _disp_call(h2t, plan, zrow, p_rows):
    t = SEQ_TILE
    n_tiles = h2t.shape[0] // (t * TOKEN_SLABS)
    smem = lambda w, imap: pl.BlockSpec((1, 1, w), imap, memory_space=pltpu.SMEM)
    cur = lambda i, z: (i, 0, 0)
    grid_spec = pltpu.PrefetchScalarGridSpec(
        num_scalar_prefetch=1,
        grid=(n_tiles,),
        in_specs=[smem(LANES, cur), smem(LANES, lambda i, z: (jnp.maximum(i - 1, 0), 0, 0)),
                  smem(t, cur), smem(t, cur),
                  pl.BlockSpec((t * TOKEN_SLABS, LANES), lambda i, z: (i, 0))],
        out_specs=pl.BlockSpec(memory_space=pl.ANY),
        scratch_shapes=[pltpu.VMEM((2, STAGE_ROWS * TOKEN_SLABS, LANES), F32),
                        pltpu.VMEM((EXPERT_ROWS * TOKEN_SLABS, LANES), F32),
                        pltpu.SemaphoreType.DMA((2,))],
    )
    return pl.pallas_call(
        _disp_kernel,
        grid_spec=grid_spec,
        out_shape=jax.ShapeDtypeStruct((p_rows * TOKEN_SLABS, LANES), F32),
        compiler_params=_params("arbitrary"),
        name="dispatch_rows",
    )(zrow, plan["table"], plan["table"], plan["pos0"], plan["pos1"], h2t)


def _exp_kernel(layer, be_ref, nu_ref, xs_ref, wg_ref, wu_ref, wd_ref, y_ref,
                wgf_ref, wuf_ref, wdf_ref, wgb_ref, wub_ref, wdb_ref, yrow_ref, sems):
    i = pl.program_id(0)
    rows = xs_ref.shape[0] // TOKEN_SLABS
    used = i < nu_ref[0]
    e = be_ref[i]
    new_expert = jnp.logical_or(i == 0, e != be_ref[jnp.maximum(i - 1, 0)])

    def fetch(expert, slot):
        return [pltpu.make_async_copy(w_ref.at[layer, expert], f_ref.at[slot], sems.at[slot])
                for w_ref, f_ref in ((wg_ref, wgf_ref), (wu_ref, wuf_ref), (wd_ref, wdf_ref))]

    @pl.when(jnp.logical_and(used, new_expert))
    def _():
        slot = e % 2

        @pl.when(i == 0)
        def _():
            for cp in fetch(e, slot):
                cp.start()

        for cp in fetch(e, slot):
            cp.wait()

        @pl.when(e + 1 < N_EXPERTS)
        def _():
            for cp in fetch(e + 1, 1 - slot):
                cp.start()

        wgb_ref[...] = wgf_ref[slot].astype(BF16)
        wub_ref[...] = wuf_ref[slot].astype(BF16)
        wdb_ref[...] = wdf_ref[slot].astype(BF16)

    @pl.when(i == 0)
    def _():
        yrow_ref[...] = jnp.zeros_like(yrow_ref)

    @pl.when(used)
    def _():
        _store_token_tiles(y_ref, yrow_ref[...])
        xb = _load_token_tiles(xs_ref, rows).astype(BF16)
        mids = []
        for c0 in range(0, D_EXPERT, EXPERT_COLS):
            gate = jnp.dot(xb, wgb_ref[:, c0:c0 + EXPERT_COLS], preferred_element_type=F32)
            up = jnp.dot(xb, wub_ref[:, c0:c0 + EXPERT_COLS], preferred_element_type=F32)
            mids.append((jax.nn.silu(gate) * up).astype(BF16))
        mid = jnp.concatenate(mids, axis=1)
        yrow_ref[...] = jnp.dot(mid, wdb_ref[...], preferred_element_type=F32)

    @pl.when(i == nu_ref[0])
    def _():
        _store_token_tiles(y_ref, yrow_ref[...])

    @pl.when(i > nu_ref[0])
    def _():
        y_ref[...] = jnp.zeros_like(y_ref)


def _exp_call(blk_e, n_used, xs, wg, wu, wd, layer):
    p = xs.shape[0] // TOKEN_SLABS
    r = EXPERT_ROWS
    blk = (r * TOKEN_SLABS, LANES)
    hbm = pl.BlockSpec(memory_space=pl.ANY)
    up_shape, down_shape = (D_MODEL, D_EXPERT), (D_EXPERT, D_MODEL)
    grid_spec = pltpu.PrefetchScalarGridSpec(
        num_scalar_prefetch=2,
        grid=(p // r + 1,),
        in_specs=[pl.BlockSpec(blk, lambda i, be, nu: (jnp.minimum(i, nu[0] - 1), 0)), hbm, hbm, hbm],
        out_specs=pl.BlockSpec(blk, lambda i, be, nu: (jnp.maximum(i - 1, 0), 0)),
        scratch_shapes=[pltpu.VMEM((2,) + up_shape, F32), pltpu.VMEM((2,) + up_shape, F32),
                        pltpu.VMEM((2,) + down_shape, F32),
                        pltpu.VMEM(up_shape, BF16), pltpu.VMEM(up_shape, BF16), pltpu.VMEM(down_shape, BF16),
                        pltpu.VMEM((r, D_MODEL), F32), pltpu.SemaphoreType.DMA((2,))],
    )
    blk_e = jnp.concatenate([blk_e, blk_e[-1:]])
    return pl.pallas_call(
        functools.partial(_exp_kernel, layer),
        grid_spec=grid_spec,
        out_shape=jax.ShapeDtypeStruct(xs.shape, F32),
        compiler_params=_params("arbitrary"),
        name="experts",
    )(blk_e, n_used, xs, wg, wu, wd)


def _gather_scratch():
    return [pltpu.VMEM((2, STAGE_ROWS * TOKEN_SLABS, LANES), F32),
            pltpu.VMEM((SEQ_TILE * TOKEN_SLABS, LANES), F32), pltpu.SemaphoreType.DMA((2,))]


def _moe_residual(tabc_ref, tabn_ref, p0_ref, p1_ref, w0_ref, w1_ref, x_ref, mod_ref, ys_ref,
                  stage_ref, moe_ref, sems):
    rows = x_ref.shape[1]
    n = pl.program_id(0) * pl.num_programs(1) + pl.program_id(1)
    last = pl.num_programs(0) * pl.num_programs(1) - 1
    slot = n % 2

    def copy(hbm_tile, stage_tile, s):
        return pltpu.make_async_copy(ys_ref.at[_tile_rows(hbm_tile, RUN_CHUNK)],
                                     stage_ref.at[s, _tile_rows(stage_tile, RUN_CHUNK)], sems.at[s])

    @pl.when(n == 0)
    def _():
        _for_each_run_chunk(tabc_ref, lambda h, s: copy(h, s, 0).start())

    @pl.when(n < last)
    def _():
        _for_each_run_chunk(tabn_ref, lambda h, s: copy(h, s, 1 - slot).start())

    _for_each_run_chunk(tabc_ref, lambda h, s: copy(h, s, slot).wait())

    for r in range(rows):
        y0 = stage_ref[slot, _tile_rows(p0_ref[0, 0, r], 1), :]
        y1 = stage_ref[slot, _tile_rows(p1_ref[0, 0, r], 1), :]
        moe_ref[pl.ds(r * TOKEN_SLABS, TOKEN_SLABS), :] = y0 * w0_ref[0, 0, r] + y1 * w1_ref[0, 0, r]
    return x_ref[0] + mod_ref[0, 5:6, :] * _load_token_tiles(moe_ref, rows)


def _combine_plan_specs(plan, b, steps):
    last = b * steps - 1
    cur = lambda bi, j: (bi * steps + j, 0, 0)
    nxt = lambda bi, j: (jnp.minimum(bi * steps + j + 1, last), 0, 0)
    smem = lambda w, imap: pl.BlockSpec((1, 1, w), imap, memory_space=pltpu.SMEM)
    t = SEQ_TILE
    specs = [smem(LANES, cur), smem(LANES, nxt), smem(t, cur), smem(t, cur), smem(t, cur), smem(t, cur)]
    return specs, (plan["table"], plan["table"], plan["pos0"], plan["pos1"], plan["w0"], plan["w1"])


def _comb_kernel(tabc_ref, tabn_ref, p0_ref, p1_ref, w0_ref, w1_ref, x_ref, mod_ref, ys_ref, o_ref,
                 stage_ref, moe_ref, sems):
    o_ref[0] = _moe_residual(tabc_ref, tabn_ref, p0_ref, p1_ref, w0_ref, w1_ref, x_ref, mod_ref, ys_ref,
                             stage_ref, moe_ref, sems)


def _comb_call(x1, plan, mod, ys):
    b, s, d = x1.shape
    t = SEQ_TILE
    steps = s // t
    plan_specs, plan_args = _combine_plan_specs(plan, b, steps)
    tok = lambda w: pl.BlockSpec((1, t, w), lambda bi, j: (bi, j, 0))
    return pl.pallas_call(
        _comb_kernel,
        grid=(b, steps),
        in_specs=plan_specs + [tok(d), pl.BlockSpec((1, 6, d), lambda bi, j: (bi, 0, 0)),
                               pl.BlockSpec(memory_space=pl.ANY)],
        out_specs=tok(d),
        out_shape=jax.ShapeDtypeStruct((b, s, d), F32),
        scratch_shapes=_gather_scratch(),
        compiler_params=_params("arbitrary", "arbitrary"),
        name="combine_rows",
    )(*plan_args, x1, mod, ys)


def _dispatch_plan(meta_t, starts, counts, n_rows):
    r = EXPERT_ROWS
    t = SEQ_TILE
    cnt = counts[:, 0].astype(jnp.int32)
    start = starts[:, 0].reshape(-1, N_EXPERTS).astype(jnp.int32)
    in_tile = jnp.concatenate([start[1:], cnt[None, :]], axis=0) - start
    chunks = (in_tile + RUN_CHUNK - 1) // RUN_CHUNK
    span = chunks * RUN_CHUNK
    offset = jnp.cumsum(span, axis=1) - span
    padded = ((cnt + RUN_CHUNK + r - 1) // r) * r
    pad_end = jnp.cumsum(padded)
    pad_start = pad_end - padded
    n_tiles = start.shape[0]
    piece_end = jnp.cumsum(chunks, axis=1)
    k = jnp.arange(MAX_CHUNKS, dtype=jnp.int32)[None, :, None]
    owner = (piece_end[:, None, :] > k) & (piece_end[:, None, :] - chunks[:, None, :] <= k)
    first = (pad_start[None, :] + start - offset)[:, None, :] + k * RUN_CHUNK
    piece_row = jnp.sum(jnp.where(owner, first, 0), axis=2)
    table = jnp.concatenate([piece_row, jnp.broadcast_to(piece_end[:, -1:], (n_tiles, LANES - MAX_CHUNKS))],
                            axis=1).astype(jnp.int32)

    experts = jnp.arange(N_EXPERTS, dtype=jnp.int32)[None, :]
    base = jnp.repeat(offset - start, t, axis=0)

    def position(e, rank):
        return jnp.sum(jnp.where(e[:, None] == experts, base, 0), axis=1) + rank

    as_int = lambda row: meta_t[row].astype(jnp.int32)
    per_tile = lambda v: v.reshape(n_tiles, 1, t)
    plan = {"table": table.reshape(n_tiles, 1, LANES),
            "pos0": per_tile(position(as_int(META_E0), as_int(META_R0))),
            "pos1": per_tile(position(as_int(META_E1), as_int(META_R1))),
            "w0": per_tile(meta_t[META_W0]), "w1": per_tile(meta_t[META_W1])}

    n_used = (pad_end[-1:] // r).astype(jnp.int32)
    blk_row = jnp.minimum(jnp.arange(n_rows // r, dtype=jnp.int32), n_used - 1) * r
    blk_e = jnp.sum((pad_end[None, :] <= blk_row[:, None]).astype(jnp.int32), axis=1)
    pad_first = pad_start + (cnt // r) * r
    pad_last = pad_end - r
    zrow = jnp.concatenate([pad_first, jnp.where(pad_last > pad_first, pad_last, -1), n_used]).astype(jnp.int32)
    return plan, blk_e, n_used, zrow


def kernel(x, c, ada_w, ada_b, norm1_g, norm2_g, w_in, pool_w, pool_scale, q_norm_g, k_norm_g, rel_bias,
           w_out, router_group_w, router_group_b, router_expert_w, router_expert_b,
           moe_w_gate, moe_w_up, moe_w_down):
    b, s, d = x.shape
    depth = ada_w.shape[0]
    n = b * s
    assert d == D_MODEL and b <= 8 and s % SEQ_TILE == 0
    spare_blocks = -(-N_EXPERTS * RUN_CHUNK // EXPERT_ROWS)
    p_rows = n * 2 + (N_EXPERTS + spare_blocks) * EXPERT_ROWS

    c_pad = jnp.zeros((8, d), F32).at[:b].set(c)
    mod_all = _ada_call(c_pad, ada_w, ada_b)
    bias = _group_bias(rel_bias)
    head_block = jnp.arange(ATT_WIDTH) // HEAD_DIM
    bd = (head_block[:, None] == head_block[None, :]).astype(BF16)
    sm_scale = HEAD_DIM ** -0.5 * LOG2_E

    moe = None
    for l in range(depth):
        mod = mod_all[l, :b].reshape(b, 6, d)
        qg = (jnp.tile(q_norm_g[l], N_HEADS) * sm_scale).reshape(1, ATT_WIDTH)
        kg = jnp.tile(k_norm_g[l], N_HEADS).reshape(1, ATT_WIDTH)
        outs = _in_call(x, mod, norm1_g[l].reshape(1, d), w_in, pool_w, l,
                        pool_scale[l].reshape(1, POOL_WIDTH), qg, kg, bd, moe)
        if moe is not None:
            x, outs = outs[0], outs[1:]
        pool, q, k, v = outs
        att = _att_call(q, k, v, bias)

        wr = jnp.zeros((d, LANES), F32)
        wr = wr.at[:, ROUTER_GROUP_LANE:ROUTER_GROUP_LANE + N_GROUPS].set(router_group_w[l])
        wr = wr.at[:, ROUTER_EXPERT_LANE:ROUTER_EXPERT_LANE + N_EXPERTS].set(router_expert_w[l])
        wr_hi = wr.astype(BF16)
        wr_lo = (wr - wr_hi.astype(F32)).astype(BF16)
        br = jnp.zeros((1, LANES), F32)
        br = br.at[0, ROUTER_GROUP_LANE:ROUTER_GROUP_LANE + N_GROUPS].set(router_group_b[l])
        br = br.at[0, ROUTER_EXPERT_LANE:ROUTER_EXPERT_LANE + N_EXPERTS].set(router_expert_b[l])
        x1, h2t, meta_t, starts, counts = _out_call(x, pool, att, mod, norm2_g[l].reshape(1, d), w_out, l,
                                                    jnp.concatenate([wr_hi, wr_lo], axis=1), br)

        plan, blk_e, n_used, zrow = _dispatch_plan(meta_t, starts, counts, p_rows)
        xs = _disp_call(h2t, plan, zrow, p_rows)
        ys = _exp_call(blk_e, n_used, xs, moe_w_gate, moe_w_up, moe_w_down, l)
        x = x1
        moe = (plan, mod, ys)
    return _comb_call(x, moe[0], moe[1], moe[2])
```
